```python
import jax
import jax.numpy as jnp
from jax import lax
import numpy as np

D_MODEL = 1024
BATCH = 1
SEQ = 16384
DEPTH = 1
DEC_BATCH = 128
DEC_SEQ = 1
PAST_LEN = 8192
PAGE_SIZE = 128

A_WINDOWS = (128, 512, 2048)
A_DILATIONS = (1, 4, 16)
A_GROUPS = 3
A_HEADS = 8
A_HEAD_DIM = 64
A_NKEY = A_WINDOWS[0] // A_DILATIONS[0]
ROPE_THETA = 10000.0
HG_HEADS = 4
HG_K = 128
HG_V = 128
HG_CHUNK = 64
MEM_LEN = 256
XA_HEADS = 4
XA_HEAD_DIM = 128
EPS = 1e-6

A_QKV = A_GROUPS * A_HEADS * A_HEAD_DIM
A_OUT = A_HEADS * A_HEAD_DIM
HG_QK = HG_HEADS * HG_K
HG_OUT = HG_HEADS * HG_V
XA_W = XA_HEADS * XA_HEAD_DIM
IN_SPLITS = (A_QKV, A_QKV, A_QKV, A_OUT, HG_QK, HG_QK, HG_OUT, HG_OUT, XA_W, XA_W, D_MODEL, D_MODEL, D_MODEL)
IN_OFFSETS = tuple(sum(IN_SPLITS[:i + 1]) for i in range(len(IN_SPLITS) - 1))
N_IN = sum(IN_SPLITS)

kernel_name = 'hybrid_dilated_hgrn2_memory_step'


def _rmsnorm(x, g):
    x32 = x.astype(jnp.float32)
    y = x32 * lax.rsqrt(jnp.mean(x32 * x32, axis=-1, keepdims=True) + EPS)
    return (y * g.astype(jnp.float32)).astype(x.dtype)


def _rope(x, pos):
    dh = x.shape[-1]
    inv = ROPE_THETA ** (-jnp.arange(0, dh, 2, dtype=jnp.float32) / dh)
    ang = pos.astype(jnp.float32)[:, None] * inv[None, :]
    cos = jnp.cos(ang)[None, :, None, :]
    sin = jnp.sin(ang)[None, :, None, :]
    x32 = x.astype(jnp.float32)
    x1, x2 = x32[..., :dh // 2], x32[..., dh // 2:]
    return jnp.concatenate([x1 * cos - x2 * sin, x2 * cos + x1 * sin], axis=-1)


def _dilated_prompt(q, k, v, dil):
    B, T, H, Dh = q.shape
    L = T // dil
    blk = A_NKEY
    nb = -(-L // blk)
    Lp = nb * blk

    def residues(a):
        return a.reshape(B, L, dil, H, Dh).transpose(0, 2, 1, 3, 4)

    qb = jnp.pad(residues(q), ((0, 0), (0, 0), (0, Lp - L), (0, 0), (0, 0))).reshape(B, dil, nb, blk, H, Dh)

    def key_blocks(a):
        a = jnp.pad(residues(a), ((0, 0), (0, 0), (blk, Lp - L), (0, 0), (0, 0)))
        prev = a[:, :, :Lp].reshape(B, dil, nb, blk, H, Dh)
        cur = a[:, :, blk:].reshape(B, dil, nb, blk, H, Dh)
        return jnp.concatenate([prev, cur], axis=3)

    kb = key_blocks(k)
    vb = key_blocks(v)
    s = jnp.einsum('brnqhd,brnkhd->brnhqk', qb, kb) * (Dh ** -0.5)
    qi = jnp.arange(blk)[:, None]
    ki = jnp.arange(2 * blk)[None, :]
    dist = qi + blk - ki
    band = (dist >= 0) & (dist <= A_NKEY)
    real = (jnp.arange(nb)[:, None, None] > 0) | (ki[None] >= blk)
    mask = band[None] & real
    s = jnp.where(mask[None, None, :, None], s, -jnp.inf)
    m = jnp.max(s, axis=-1, keepdims=True)
    p = jnp.exp(s - m)
    l = jnp.sum(p, axis=-1, keepdims=True)
    o = jnp.einsum('brnhqk,brnkhd->brnqhd', p / l, vb)
    lse = (m + jnp.log(l))[..., 0].transpose(0, 1, 2, 4, 3)
    o = o.reshape(B, dil, Lp, H, Dh)[:, :, :L].transpose(0, 2, 1, 3, 4).reshape(B, T, H, Dh)
    lse = lse.reshape(B, dil, Lp, H)[:, :, :L].transpose(0, 2, 1, 3).reshape(B, T, H)
    return o, lse


def _dilated_sample(q, buf, k, v, dil):
    Wb = buf.shape[1]
    S = q.shape[1]
    Dh = q.shape[-1]
    kc = jnp.concatenate([buf[:, :, 0].astype(jnp.float32), k], axis=1)
    vc = jnp.concatenate([buf[:, :, 1].astype(jnp.float32), v], axis=1)
    idx = Wb + jnp.arange(S)[:, None] - dil * jnp.arange(A_NKEY + 1)[None, :]
    valid = idx >= 0
    idx = jnp.maximum(idx, 0)
    kg = kc[:, idx]
    vg = vc[:, idx]
    s = jnp.einsum('bshd,bsjhd->bshj', q, kg) * (Dh ** -0.5)
    s = jnp.where(valid[None, :, None, :], s, -jnp.inf)
    m = jnp.max(s, axis=-1, keepdims=True)
    p = jnp.exp(s - m)
    l = jnp.sum(p, axis=-1, keepdims=True)
    o = jnp.einsum('bshj,bsjhd->bshd', p / l, vg)
    lse = (m + jnp.log(l))[..., 0]
    return o, lse


def _hgrn2(q, k, v, logf, S0):
    B, T, H, K = q.shape
    C = min(HG_CHUNK, T)
    n = -(-T // C)
    Tp = n * C

    def chunks(a):
        a = jnp.pad(a, ((0, 0), (0, Tp - T), (0, 0), (0, 0)))
        return a.reshape(B, n, C, H, a.shape[-1]).transpose(1, 0, 2, 3, 4)

    tri = jnp.tril(jnp.ones((C, C), dtype=bool))

    def step(S, inp):
        qc, kc, vc, gc = inp
        b = jnp.cumsum(gc, axis=1)
        o_inter = jnp.einsum('bthk,bhkv->bthv', qc * jnp.exp(b), S)
        diff = b[:, :, None] - b[:, None, :]
        dec = jnp.exp(jnp.where(tri[None, :, :, None, None], diff, -jnp.inf))
        att = jnp.einsum('bthk,btshk,bshk->bhts', qc, dec, kc)
        o_intra = jnp.einsum('bhts,bshv->bthv', att, vc)
        b_last = b[:, -1]
        S_new = jnp.exp(b_last)[..., None] * S + jnp.einsum('bshk,bshv->bhkv', kc * jnp.exp(b_last[:, None] - b), vc)
        return S_new, o_inter + o_intra

    S_fin, o = lax.scan(step, S0, (chunks(q), chunks(k), chunks(v), chunks(logf)))
    o = o.transpose(1, 0, 2, 3, 4).reshape(B, Tp, H, v.shape[-1])[:, :T]
    return o, S_fin


def _layer(h, pos, attend, S0, mem_kv, lb, pre_g, post_g, w_in, hg_g, w_pa, w_pb, w_pc, w_out):
    B, T, _ = h.shape
    u = _rmsnorm(h, pre_g) @ w_in
    aq, ak, av, az, bq, bf, bi, bz, cq, cz, ga, gb, gc = jnp.split(u, IN_OFFSETS, axis=-1)

    ga_shape = (B, T, A_GROUPS, A_HEADS, A_HEAD_DIM)
    aq = _rope(aq.reshape(B, T, A_GROUPS * A_HEADS, A_HEAD_DIM), pos).reshape(ga_shape)
    ak = _rope(ak.reshape(B, T, A_GROUPS * A_HEADS, A_HEAD_DIM), pos).reshape(ga_shape)
    av = av.astype(jnp.float32).reshape(ga_shape)
    outs = []
    lses = []
    for g in range(A_GROUPS):
        o_g, lse_g = attend(g, aq[:, :, g], ak[:, :, g], av[:, :, g])
        outs.append(o_g)
        lses.append(lse_g)
    alpha = jax.nn.softmax(jnp.stack(lses, axis=0), axis=0)
    oa = jnp.sum(alpha[..., None] * jnp.stack(outs, axis=0), axis=0).reshape(B, T, A_OUT)
    ya = oa.astype(h.dtype) * jax.nn.silu(az)

    bf32 = bf.astype(jnp.float32)
    logf = jnp.log(lb + (1.0 - lb) * jax.nn.sigmoid(bf32))
    kin = (1.0 - lb) * jax.nn.sigmoid(-bf32)
    hs4 = (B, T, HG_HEADS, HG_K)
    ob, S_fin = _hgrn2(bq.astype(jnp.float32).reshape(hs4), kin.reshape(hs4),
                       bi.astype(jnp.float32).reshape(B, T, HG_HEADS, HG_V), logf.reshape(hs4), S0)
    ob = ob * lax.rsqrt(jnp.mean(ob * ob, axis=-1, keepdims=True) + EPS) * hg_g.astype(jnp.float32).reshape(HG_HEADS, HG_V)
    yb = ob.reshape(B, T, HG_OUT).astype(h.dtype) * jax.nn.silu(bz)

    mk = mem_kv[:, :, 0].astype(jnp.float32)
    mv = mem_kv[:, :, 1].astype(jnp.float32)
    cq32 = cq.astype(jnp.float32).reshape(B, T, XA_HEADS, XA_HEAD_DIM)
    sc = jnp.einsum('bthd,bmhd->bhtm', cq32, mk) * (XA_HEAD_DIM ** -0.5)
    pc = jax.nn.softmax(sc, axis=-1)
    oc = jnp.einsum('bhtm,bmhd->bthd', pc, mv).reshape(B, T, XA_W)
    yc = oc.astype(h.dtype) * jax.nn.silu(cz)

    merged = (jax.nn.sigmoid(ga) * (ya @ w_pa) + jax.nn.sigmoid(gb) * (yb @ w_pb)
              + jax.nn.sigmoid(gc) * (yc @ w_pc))
    out = h + _rmsnorm(merged @ w_out, post_g)
    return out, ak, av, S_fin


def setup_inputs(seed: int = 0) -> dict:
    key = jax.random.key(seed)
    ks = jax.random.split(key, 20)

    def nrm(k, shape, scale):
        return jax.random.normal(k, shape, jnp.float32) * scale

    wb = [min(w, PAST_LEN) for w in A_WINDOWS]
    return {
        'x_prompt': nrm(ks[0], (BATCH, SEQ, D_MODEL), 1.0),
        'x_sample': nrm(ks[1], (DEC_BATCH, DEC_SEQ, D_MODEL), 1.0),
        'mem_prompt': nrm(ks[2], (BATCH, MEM_LEN, D_MODEL), 1.0),
        'cache_win128_kv': nrm(ks[3], (DEPTH, DEC_BATCH, wb[0], 2, A_HEADS, A_HEAD_DIM), 1.0),
        'cache_win512_kv': nrm(ks[4], (DEPTH, DEC_BATCH, wb[1], 2, A_HEADS, A_HEAD_DIM), 1.0),
        'cache_win2048_kv': nrm(ks[5], (DEPTH, DEC_BATCH, wb[2], 2, A_HEADS, A_HEAD_DIM), 1.0),
        'state_hgrn': nrm(ks[6], (DEPTH, DEC_BATCH, HG_HEADS, HG_K, HG_V), 0.5),
        'cache_mem_kv': nrm(ks[7], (DEPTH, DEC_BATCH, MEM_LEN, 2, XA_HEADS, XA_HEAD_DIM), 1.0),
        'norm_pre': 1.0 + nrm(ks[8], (DEPTH, D_MODEL), 0.05),
        'norm_post': 1.0 + nrm(ks[9], (DEPTH, D_MODEL), 0.05),
        'w_in': nrm(ks[10], (DEPTH, D_MODEL, N_IN), D_MODEL ** -0.5),
        'hgrn_lb_logits': nrm(ks[11], (DEPTH + 1, HG_QK), 0.1),
        'hgrn_out_norm': 1.0 + nrm(ks[12], (DEPTH, HG_OUT), 0.05),
        'mem_norm': 1.0 + nrm(ks[13], (DEPTH, D_MODEL), 0.05),
        'w_mem_kv': nrm(ks[14], (DEPTH, D_MODEL, 2 * XA_W), D_MODEL ** -0.5),
        'w_branch_a': nrm(ks[15], (DEPTH, A_OUT, D_MODEL), A_OUT ** -0.5),
        'w_branch_b': nrm(ks[16], (DEPTH, HG_OUT, D_MODEL), HG_OUT ** -0.5),
        'w_branch_c': nrm(ks[17], (DEPTH, XA_W, D_MODEL), XA_W ** -0.5),
        'w_out': nrm(ks[18], (DEPTH, D_MODEL, D_MODEL), D_MODEL ** -0.5),
    }


def reference(x_prompt, x_sample, mem_prompt, cache_win128_kv, cache_win512_kv, cache_win2048_kv,
              state_hgrn, cache_mem_kv, norm_pre, norm_post, w_in, hgrn_lb_logits, hgrn_out_norm,
              mem_norm, w_mem_kv, w_branch_a, w_branch_b, w_branch_c, w_out):
    win_caches = (cache_win128_kv, cache_win512_kv, cache_win2048_kv)
    lb_all = jnp.cumsum(jax.nn.softmax(hgrn_lb_logits.astype(jnp.float32), axis=0), axis=0)
    Bp, T, _ = x_prompt.shape
    Bd, S, _ = x_sample.shape
    M = mem_prompt.shape[1]
    pos_p = jnp.arange(T, dtype=jnp.int32)
    pos_s = PAST_LEN + jnp.arange(S, dtype=jnp.int32)
    hp = x_prompt
    hs = x_sample
    p_win = [[] for _ in range(A_GROUPS)]
    s_win = [[] for _ in range(A_GROUPS)]
    p_hg, s_hg, p_mem = [], [], []

    def attend_p(g, q, k, v):
        return _dilated_prompt(q, k, v, A_DILATIONS[g])

    for l in range(DEPTH):
        shared = (lb_all[l], norm_pre[l], norm_post[l], w_in[l], hgrn_out_norm[l],
                  w_branch_a[l], w_branch_b[l], w_branch_c[l], w_out[l])
        mem_kv = (_rmsnorm(mem_prompt, mem_norm[l]) @ w_mem_kv[l]).reshape(Bp, M, 2, XA_HEADS, XA_HEAD_DIM)
        S0 = jnp.zeros((Bp, HG_HEADS, HG_K, HG_V), jnp.float32)
        hp, akp, avp, Sp = _layer(hp, pos_p, attend_p, S0, mem_kv, *shared)
        for g in range(A_GROUPS):
            wb = min(A_WINDOWS[g], T)
            p_win[g].append(jnp.stack([akp[:, T - wb:, g], avp[:, T - wb:, g]], axis=2).astype(x_prompt.dtype))
        p_hg.append(Sp.astype(x_prompt.dtype))
        p_mem.append(mem_kv)
        bufs = tuple(c[l] for c in win_caches)

        def attend_s(g, q, k, v, bufs=bufs):
            return _dilated_sample(q, bufs[g], k, v, A_DILATIONS[g])

        hs, aks, avs, Ss = _layer(hs, pos_s, attend_s, state_hgrn[l].astype(jnp.float32), cache_mem_kv[l], *shared)
        for g in range(A_GROUPS):
            s_win[g].append(jnp.stack([aks[:, :, g], avs[:, :, g]], axis=2).astype(x_sample.dtype))
        s_hg.append(Ss.astype(x_sample.dtype))

    new_win128_p = jnp.stack(p_win[0], axis=0)
    new_win512_p = jnp.stack(p_win[1], axis=0)
    new_win2048_p = jnp.stack(p_win[2], axis=0)
    new_hgrn_p = jnp.stack(p_hg, axis=0)
    new_mem_kv_p = jnp.stack(p_mem, axis=0)
    new_win128_s = jnp.stack(s_win[0], axis=0)
    new_win512_s = jnp.stack(s_win[1], axis=0)
    new_win2048_s = jnp.stack(s_win[2], axis=0)
    new_hgrn_s = jnp.stack(s_hg, axis=0)
    return (hp, hs, new_win128_p, new_win512_p, new_win2048_p, new_hgrn_p, new_mem_kv_p,
            new_win128_s, new_win512_s, new_win2048_s, new_hgrn_s)
```

```python
import functools

import jax
import jax.numpy as jnp
from jax import lax
from jax.experimental import pallas as pl
from jax.experimental.pallas import tpu as pltpu

D_MODEL = 1024
PAST_LEN = 8192
A_WINDOWS = (128, 512, 2048)
A_DILATIONS = (1, 4, 16)
A_GROUPS = 3
A_HEADS = 8
A_HEAD_DIM = 64
A_NKEY = 128
ROPE_THETA = 10000.0
HG_HEADS = 4
HG_K = 128
HG_V = 128
HG_CHUNK = 64
HG_BLOCK = 8
MEM_LEN = 256
XA_HEADS = 4
XA_HEAD_DIM = 128
EPS = 1e-6

A_W = A_HEADS * A_HEAD_DIM
HG_W = HG_HEADS * HG_K
XA_W = XA_HEADS * XA_HEAD_DIM
N_ATT = 3 * A_GROUPS * A_W
COL_AZ = N_ATT
COL_BQ = COL_AZ + A_W
COL_BF = COL_BQ + HG_W
COL_BI = COL_BF + HG_W
COL_BZ = COL_BI + HG_W
COL_CQ = COL_BZ + HG_W
COL_CZ = COL_CQ + XA_W
COL_GA = COL_CZ + XA_W
COL_GB = COL_GA + D_MODEL
COL_GC = COL_GB + D_MODEL
N_IN = COL_GC + D_MODEL

LANES = 128
VMEM_LIMIT_BYTES = 56 * 1024 * 1024

F32 = jnp.float32
BF16 = jnp.bfloat16
NEG_INF = float("-inf")


def _params(semantics, vmem=None):
    return pltpu.CompilerParams(dimension_semantics=semantics, vmem_limit_bytes=vmem)


def _resident(shape):
    return pl.BlockSpec(shape, lambda *_: (0,) * len(shape), pipeline_mode=pl.Buffered(1))


def _rmsnorm_rows(x, gain):
    ms = jnp.mean(x * x, axis=-1, keepdims=True)
    return x * lax.rsqrt(ms + EPS) * gain


def _sigmoid(x):
    return 1.0 / (1.0 + jnp.exp(-x))


def _silu(x):
    return x * _sigmoid(x)


def _inproj_attn_kernel(x_ref, g_ref, w_ref, invf_ref, *out_refs, tm, n_tiles, pos_base, pos_step, tails):
    q_refs = out_refs[0:3]
    k_refs = out_refs[3:6]
    v_refs = out_refs[6:9]
    t_refs = out_refs[9:12]
    i = pl.program_id(0)

    xn = _rmsnorm_rows(x_ref[...], g_ref[...]).astype(BF16)

    row = lax.broadcasted_iota(jnp.int32, (tm, LANES), 0)
    lane = lax.broadcasted_iota(jnp.int32, (tm, LANES), 1)
    pos = (pos_base + (i * tm + row) * pos_step).astype(F32)
    ang = pos * invf_ref[...]
    cos = jnp.cos(ang)
    sin = jnp.sin(ang)
    first_half = (lane % A_HEAD_DIM) < (A_HEAD_DIM // 2)
    sin_signed = jnp.where(first_half, -sin, sin)

    def rope_slab(xc):
        partner = jnp.where(first_half, pltpu.roll(xc, LANES - 32, 1), pltpu.roll(xc, 32, 1))
        return xc * cos + partner * sin_signed

    def write_tail(g, col0, val):
        first_tile, rows = tails[g]
        if rows >= tm:
            @pl.when(i >= first_tile)
            def _():
                t_refs[g][:, col0:col0 + A_W] = val
        else:
            @pl.when(i == n_tiles - 1)
            def _():
                t_refs[g][:, col0:col0 + A_W] = val[tm - rows:, :]

    for g in range(A_GROUPS):
        acc = jnp.dot(xn, w_ref[:, g * A_W:(g + 1) * A_W], preferred_element_type=F32)
        for c in range(A_W // LANES):
            sl = slice(c * LANES, (c + 1) * LANES)
            q_refs[g][:, sl] = (rope_slab(acc[:, sl]) * (A_HEAD_DIM ** -0.5)).astype(q_refs[g].dtype)
    for g in range(A_GROUPS):
        c0 = (A_GROUPS + g) * A_W
        acc = jnp.dot(xn, w_ref[:, c0:c0 + A_W], preferred_element_type=F32)
        kr = jnp.concatenate([rope_slab(acc[:, c * LANES:(c + 1) * LANES]) for c in range(A_W // LANES)], axis=-1)
        k_refs[g][...] = kr.astype(k_refs[g].dtype)
        write_tail(g, 0, kr)
    for g in range(A_GROUPS):
        c0 = (2 * A_GROUPS + g) * A_W
        acc = jnp.dot(xn, w_ref[:, c0:c0 + A_W], preferred_element_type=F32)
        v_refs[g][...] = acc.astype(v_refs[g].dtype)
        write_tail(g, A_W, acc)


def _inproj_attn(x, gain, w_att, invf, *, tm, pos_base, pos_step, tail_rows, out_dtype):
    n = x.shape[0]
    n_tiles = n // tm
    tails = []
    tail_specs = []
    tail_shapes = []
    for g in range(A_GROUPS):
        rows = tail_rows[g]
        first_tile = (n - rows) // tm
        tails.append((first_tile, rows))
        blk = min(tm, rows)
        tail_specs.append(pl.BlockSpec((blk, 2 * A_W), functools.partial(
            lambda i, ft: (jnp.maximum(i - ft, 0), 0), ft=first_tile)))
        tail_shapes.append(jax.ShapeDtypeStruct((rows, 2 * A_W), F32))
    row_spec = pl.BlockSpec((tm, A_W), lambda i: (i, 0))
    kernel = functools.partial(_inproj_attn_kernel, tm=tm, n_tiles=n_tiles, pos_base=pos_base,
                               pos_step=pos_step, tails=tuple(tails))
    outs = pl.pallas_call(
        kernel,
        out_shape=[jax.ShapeDtypeStruct((n, A_W), out_dtype)] * 9 + tail_shapes,
        grid=(n_tiles,),
        in_specs=[pl.BlockSpec((tm, D_MODEL), lambda i: (i, 0)),
                  _resident((1, D_MODEL)),
                  _resident((D_MODEL, N_ATT)),
                  _resident((1, LANES))],
        out_specs=[row_spec] * 9 + tail_specs,
        compiler_params=_params(("arbitrary",), VMEM_LIMIT_BYTES),
        name="inproj_attn",
    )(x, gain, w_att, invf)
    return outs[0:3], outs[3:6], outs[6:9], outs[9:12]


def _inproj_rest_kernel(x_ref, g_ref, w_ref, lbl_ref, az_ref, bq_ref, lf_ref, kin_ref, bi_ref, bz_ref,
                        cq_ref, cz_ref, ga_ref, gb_ref, gc_ref, *, layer):
    xn = _rmsnorm_rows(x_ref[...], g_ref[...]).astype(BF16)

    def proj(col, width):
        c0 = col - N_ATT
        return jnp.dot(xn, w_ref[:, c0:c0 + width], preferred_element_type=F32)

    logits = lbl_ref[...]
    e = jnp.exp(logits - jnp.max(logits, axis=0, keepdims=True))
    lb = jnp.sum(e[0:layer + 1, :], axis=0, keepdims=True) / jnp.sum(e, axis=0, keepdims=True)

    az_ref[...] = proj(COL_AZ, A_W).astype(az_ref.dtype)
    bq_ref[...] = proj(COL_BQ, HG_W).astype(bq_ref.dtype)
    bf = proj(COL_BF, HG_W)
    lf_ref[...] = jnp.log(lb + (1.0 - lb) * _sigmoid(bf))
    kin_ref[...] = ((1.0 - lb) * _sigmoid(-bf)).astype(kin_ref.dtype)
    bi_ref[...] = proj(COL_BI, HG_W).astype(bi_ref.dtype)
    bz_ref[...] = proj(COL_BZ, HG_W).astype(bz_ref.dtype)
    cq_ref[...] = proj(COL_CQ, XA_W).astype(cq_ref.dtype)
    cz_ref[...] = proj(COL_CZ, XA_W).astype(cz_ref.dtype)
    ga_ref[...] = proj(COL_GA, D_MODEL).astype(ga_ref.dtype)
    gb_ref[...] = proj(COL_GB, D_MODEL).astype(gb_ref.dtype)
    gc_ref[...] = proj(COL_GC, D_MODEL).astype(gc_ref.dtype)


def _inproj_rest(x, gain, w_rest, lb_logits, *, tm, layer, hg_dtype):
    n = x.shape[0]
    n_rest = N_IN - N_ATT
    s512 = pl.BlockSpec((tm, 512), lambda i: (i, 0))
    s1024 = pl.BlockSpec((tm, D_MODEL), lambda i: (i, 0))

    def sds(width, dt):
        return jax.ShapeDtypeStruct((n, width), dt)

    return pl.pallas_call(
        functools.partial(_inproj_rest_kernel, layer=layer),
        out_shape=[sds(A_W, BF16), sds(HG_W, hg_dtype), sds(HG_W, F32), sds(HG_W, hg_dtype), sds(HG_W, hg_dtype),
                   sds(HG_W, BF16), sds(XA_W, BF16), sds(XA_W, BF16),
                   sds(D_MODEL, BF16), sds(D_MODEL, BF16), sds(D_MODEL, BF16)],
        grid=(n // tm,),
        in_specs=[pl.BlockSpec((tm, D_MODEL), lambda i: (i, 0)),
                  _resident((1, D_MODEL)),
                  _resident((D_MODEL, n_rest)),
                  _resident(lb_logits.shape)],
        out_specs=[s512] * 8 + [s1024] * 3,
        compiler_params=_params(("parallel",), VMEM_LIMIT_BYTES),
        name="inproj_rest",
    )(x, gain, w_rest, lb_logits)


def _attn_prompt_kernel(q_ref, kc_ref, vc_ref, kp_ref, vp_ref, o_ref, st_ref, kbuf, vbuf, *, tq):
    n = pl.program_id(1)
    blk = A_NKEY
    kbuf[0:blk, :] = kp_ref[...]
    kbuf[blk:, :] = kc_ref[...]
    vbuf[0:blk, :] = vp_ref[...]
    vbuf[blk:, :] = vc_ref[...]

    qi = lax.broadcasted_iota(jnp.int32, (blk, 2 * blk), 0)
    ki = lax.broadcasted_iota(jnp.int32, (blk, 2 * blk), 1)
    dist = qi + blk - ki
    band = (dist >= 0) & (dist <= A_NKEY)
    bias = jnp.where(band, 0.0, NEG_INF).astype(F32)
    bias_first = jnp.where(band & (ki >= blk), 0.0, NEG_INF).astype(F32)
    lane = lax.broadcasted_iota(jnp.int32, (blk, LANES), 1)
    low_head = lane < A_HEAD_DIM
    head_mask = (jnp.where(low_head, 1.0, 0.0).astype(BF16), jnp.where(low_head, 0.0, 1.0).astype(BF16))

    def body(b, carry):
        r0 = pl.multiple_of(b * blk, blk)
        is_first = jnp.logical_and(n == 0, b == 0)
        bias_b = jnp.where(is_first, bias_first, bias)
        qb = q_ref[pl.ds(r0, blk), :]
        kb = kbuf[pl.ds(r0, 2 * blk), :]
        vb = vbuf[pl.ds(r0, 2 * blk), :]
        stats = jnp.zeros((blk, LANES), F32)
        for pair in range(A_HEADS // 2):
            sl = slice(pair * LANES, (pair + 1) * LANES)
            qp = qb[:, sl]
            kp = kb[:, sl]
            vp = vb[:, sl]
            outs = []
            for hh in range(2):
                qm = qp * head_mask[hh]
                s = pl.dot(qm, kp, trans_b=True) + bias_b
                m = jnp.max(s, axis=-1, keepdims=True)
                p = jnp.exp(s - m)
                l = jnp.sum(p, axis=-1, keepdims=True)
                pv = jnp.dot(p.astype(BF16), vp, preferred_element_type=F32)
                outs.append(pv * (1.0 / l))
                lse = m + jnp.log(l)
                stats = jnp.where(lane == 2 * pair + hh, lse, stats)
            o_ref[pl.ds(r0, blk), sl] = jnp.where(low_head, outs[0], outs[1]).astype(o_ref.dtype)
        st_ref[pl.ds(r0, blk), :] = stats
        return carry

    lax.fori_loop(0, tq // blk, body, 0)


def _attn_prompt(q, k, v, dil):
    t = q.shape[0]
    l = t // dil
    tq = min(1024, l)
    blk = A_NKEY
    qv, kv, vv = (a.reshape(l, dil * A_W) for a in (q, k, v))
    cur = pl.BlockSpec((tq, A_W), lambda r, n: (n, r))
    prev = pl.BlockSpec((blk, A_W), lambda r, n: (jnp.maximum(n * (tq // blk) - 1, 0), r))
    o, st = pl.pallas_call(
        functools.partial(_attn_prompt_kernel, tq=tq),
        out_shape=[jax.ShapeDtypeStruct((l, dil * A_W), BF16), jax.ShapeDtypeStruct((l, dil * LANES), F32)],
        grid=(dil, l // tq),
        in_specs=[cur, cur, cur, prev, prev],
        out_specs=[cur, pl.BlockSpec((tq, LANES), lambda r, n: (n, r))],
        scratch_shapes=[pltpu.VMEM((tq + blk, A_W), BF16), pltpu.VMEM((tq + blk, A_W), BF16)],
        compiler_params=_params(("parallel", "parallel")),
        name=f"attn_prompt_d{dil}",
    )(qv, kv, vv, kv, vv)
    return o.reshape(t, A_W), st.reshape(t, LANES)


def _shift_rows(x, s):
    return pltpu.roll(x, s, 0)


def _chunk_cumsum(g, row):
    b = g
    s = 1
    while s < g.shape[0]:
        b = b + jnp.where(row >= s, _shift_rows(b, s), 0.0)
        s *= 2
    return b


def _hgrn_prompt_kernel(q_ref, lf_ref, k_ref, v_ref, gain_ref, o_ref, s_out_ref, st_ref, *, th):
    i = pl.program_id(0)
    c_rows = HG_CHUNK
    nblk = c_rows // HG_BLOCK

    @pl.when(i == 0)
    def _():
        st_ref[...] = jnp.zeros_like(st_ref)

    row = lax.broadcasted_iota(jnp.int32, (c_rows, HG_K), 0)
    row_in_blk = row % HG_BLOCK
    a_row = lax.broadcasted_iota(jnp.int32, (c_rows, c_rows), 0) // HG_BLOCK
    a_col = lax.broadcasted_iota(jnp.int32, (c_rows, c_rows), 1) // HG_BLOCK

    def chunk(c, carry):
        r0 = pl.multiple_of(c * c_rows, c_rows)
        for h in range(HG_HEADS):
            sl = slice(h * HG_K, (h + 1) * HG_K)
            q = q_ref[pl.ds(r0, c_rows), sl].astype(F32)
            k = k_ref[pl.ds(r0, c_rows), sl].astype(F32)
            v = v_ref[pl.ds(r0, c_rows), sl].astype(F32)
            g = lf_ref[pl.ds(r0, c_rows), sl]
            b = _chunk_cumsum(g, row)
            b_last = b[c_rows - 1:c_rows, :]

            o = jnp.sum(q * k, axis=-1, keepdims=True) * v
            for d in range(1, HG_BLOCK):
                valid = row_in_blk >= d
                dec = jnp.exp(jnp.where(valid, b - _shift_rows(b, d), NEG_INF))
                a = jnp.sum(q * _shift_rows(k, d) * dec, axis=-1, keepdims=True)
                o = o + a * _shift_rows(v, d)

            b3 = b.reshape(nblk, HG_BLOCK, HG_K)
            b_end = jnp.broadcast_to(b3[:, HG_BLOCK - 1:HG_BLOCK, :], b3.shape).reshape(c_rows, HG_K)
            k_hat = (k * jnp.exp(b_end - b)).astype(BF16)
            q_ver = [(q * jnp.exp(jnp.minimum(b - b[HG_BLOCK * j + HG_BLOCK - 1:HG_BLOCK * (j + 1), :], 0.0))).astype(BF16)
                     for j in range(nblk - 1)]
            r = pl.dot(jnp.concatenate(q_ver, axis=0), k_hat, trans_b=True)
            att = jnp.zeros((c_rows, c_rows), F32)
            for j in range(nblk - 1):
                att = jnp.where(a_col == j, r[j * c_rows:(j + 1) * c_rows, :], att)
            att = jnp.where(a_row > a_col, att, 0.0)
            vb = v.astype(BF16)
            o = o + jnp.dot(att.astype(BF16), vb, preferred_element_type=F32)

            st = st_ref[h]
            o = o + pl.dot((q * jnp.exp(b)).astype(BF16), st.astype(BF16), trans_b=True)
            k_end = (k * jnp.exp(b_last - b)).astype(BF16)
            st_ref[h] = st * jnp.exp(b_last) + pl.dot(vb, k_end, trans_a=True)

            on = o * lax.rsqrt(jnp.mean(o * o, axis=-1, keepdims=True) + EPS) * gain_ref[:, sl]
            o_ref[pl.ds(r0, c_rows), sl] = on.astype(o_ref.dtype)
        return carry

    lax.fori_loop(0, th // c_rows, chunk, 0)

    @pl.when(i == pl.num_programs(0) - 1)
    def _():
        for h in range(HG_HEADS):
            s_out_ref[h] = st_ref[h].T


def _hgrn_prompt(q, lf, k, v, gain, *, th):
    t = q.shape[0]
    spec = pl.BlockSpec((th, HG_W), lambda i: (i, 0))
    return pl.pallas_call(
        functools.partial(_hgrn_prompt_kernel, th=th),
        out_shape=[jax.ShapeDtypeStruct((t, HG_W), BF16),
                   jax.ShapeDtypeStruct((HG_HEADS, HG_K, HG_V), F32)],
        grid=(t // th,),
        in_specs=[spec, spec, spec, spec, _resident((1, HG_W))],
        out_specs=[spec, pl.BlockSpec((HG_HEADS, HG_K, HG_V), lambda i: (0, 0, 0))],
        scratch_shapes=[pltpu.VMEM((HG_HEADS, HG_V, HG_K), F32)],
        compiler_params=_params(("arbitrary",)),
        name="hgrn_prompt",
    )(q, lf, k, v, gain)


def _mem_kv_kernel(m_ref, g_ref, w_ref, o_ref):
    xn = _rmsnorm_rows(m_ref[...], g_ref[...]).astype(BF16)
    o_ref[...] = jnp.dot(xn, w_ref[...], preferred_element_type=F32)


def _mem_kv(mem, gain, w):
    m = mem.shape[0]
    return pl.pallas_call(
        _mem_kv_kernel,
        out_shape=jax.ShapeDtypeStruct((m, 2 * XA_W), F32),
        name="mem_kv",
    )(mem, gain, w)


def _softmax_rows(s):
    m = jnp.max(s, axis=-1, keepdims=True)
    p = jnp.exp(s - m)
    return p, jnp.sum(p, axis=-1, keepdims=True)


def _final_kernel(*refs, tm, merge, xattn):
    refs = list(refs)
    if merge:
        o_refs = [refs.pop(0) for _ in range(A_GROUPS)]
        st_refs = [refs.pop(0) for _ in range(A_GROUPS)]
    else:
        oa_ref = refs.pop(0)
    if xattn:
        cq_ref, mk_ref, mv_ref = (refs.pop(0) for _ in range(3))
    else:
        oc_ref = refs.pop(0)
    (ob_ref, az_ref, bz_ref, cz_ref, ga_ref, gb_ref, gc_ref, x_ref, pg_ref,
     wa_ref, wb_ref, wc_ref, wo_ref, out_ref) = refs

    if merge:
        lse = [r[...] for r in st_refs]
        mx = jnp.maximum(jnp.maximum(lse[0], lse[1]), lse[2])
        ex = [jnp.exp(x - mx) for x in lse]
        inv = 1.0 / (ex[0] + ex[1] + ex[2])
        alpha = [e * inv for e in ex]
        lane_head = lax.broadcasted_iota(jnp.int32, (tm, A_W), 1) // A_HEAD_DIM
        oa = jnp.zeros((tm, A_W), F32)
        for g in range(A_GROUPS):
            w = jnp.zeros((tm, A_W), F32)
            for h in range(A_HEADS):
                w = jnp.where(lane_head == h, alpha[g][:, h:h + 1], w)
            oa = oa + w * o_refs[g][...].astype(F32)
    else:
        oa = oa_ref[...].astype(F32)
    ya = (oa * _silu(az_ref[...].astype(F32))).astype(BF16)

    if xattn:
        cq = cq_ref[...]
        parts = []
        for h in range(XA_HEADS):
            sl = slice(h * XA_HEAD_DIM, (h + 1) * XA_HEAD_DIM)
            s = pl.dot(cq[:, sl], mk_ref[:, sl], trans_b=True) * (XA_HEAD_DIM ** -0.5)
            p, l = _softmax_rows(s)
            parts.append(jnp.dot(p.astype(BF16), mv_ref[:, sl], preferred_element_type=F32) * (1.0 / l))
        oc = jnp.concatenate(parts, axis=-1)
    else:
        oc = oc_ref[...].astype(F32)
    yc = (oc * _silu(cz_ref[...].astype(F32))).astype(BF16)

    yb = (ob_ref[...].astype(F32) * _silu(bz_ref[...].astype(F32))).astype(BF16)

    merged = (_sigmoid(ga_ref[...].astype(F32)) * jnp.dot(ya, wa_ref[...], preferred_element_type=F32)
              + _sigmoid(gb_ref[...].astype(F32)) * jnp.dot(yb, wb_ref[...], preferred_element_type=F32)
              + _sigmoid(gc_ref[...].astype(F32)) * jnp.dot(yc, wc_ref[...], preferred_element_type=F32))
    z = jnp.dot(merged.astype(BF16), wo_ref[...], preferred_element_type=F32)
    out_ref[...] = x_ref[...] + _rmsnorm_rows(z, pg_ref[...])


def _final(att_in, xa_in, ob, az, bz, cz, ga, gb, gc, x, post_g, w_pa, w_pb, w_pc, w_out, *, tm, merge, xattn):
    n = x.shape[0]

    def rows(width):
        return pl.BlockSpec((tm, width), lambda i: (i, 0))

    if merge:
        att_specs = [rows(A_W)] * A_GROUPS + [rows(LANES)] * A_GROUPS
    else:
        att_specs = [rows(A_W)]
    if xattn:
        xa_specs = [rows(XA_W), _resident((MEM_LEN, XA_W)), _resident((MEM_LEN, XA_W))]
    else:
        xa_specs = [rows(XA_W)]
    in_specs = (att_specs + xa_specs + [rows(HG_W), rows(A_W), rows(HG_W), rows(XA_W)]
                + [rows(D_MODEL)] * 4 + [_resident((1, D_MODEL))]
                + [_resident((A_W, D_MODEL)), _resident((HG_W, D_MODEL)), _resident((XA_W, D_MODEL)),
                   _resident((D_MODEL, D_MODEL))])
    return pl.pallas_call(
        functools.partial(_final_kernel, tm=tm, merge=merge, xattn=xattn),
        out_shape=jax.ShapeDtypeStruct((n, D_MODEL), F32),
        grid=(n // tm,),
        in_specs=in_specs,
        out_specs=rows(D_MODEL),
        compiler_params=_params(("parallel",), VMEM_LIMIT_BYTES),
        name="final_merge" if merge else "final",
    )(*att_in, *xa_in, ob, az, bz, cz, ga, gb, gc, x, post_g, w_pa, w_pb, w_pc, w_out)


def _head_rows(q_row, n_heads, head_dim):
    w = q_row.shape[-1]
    sub = lax.broadcasted_iota(jnp.int32, (8, w), 0)
    lane_head = lax.broadcasted_iota(jnp.int32, (8, w), 1) // head_dim
    del n_heads
    keep = sub == lane_head
    return jnp.where(keep, jnp.broadcast_to(q_row, (8, w)), 0.0), keep


def _attn_sample_kernel(q0, q1, q2, k0, k1, k2, v0, v1, v2, c0, c1, c2, o_ref, *, sb):
    q_refs = (q0, q1, q2)
    k_refs = (k0, k1, k2)
    v_refs = (v0, v1, v2)
    c_refs = (c0, c1, c2)
    for b in range(sb):
        outs = []
        lses = []
        keep = None
        for g in range(A_GROUPS):
            q_row = q_refs[g][b:b + 1, :].astype(F32)
            k_new = k_refs[g][b:b + 1, :].astype(F32)
            v_new = v_refs[g][b:b + 1, :].astype(F32)
            qm, keep = _head_rows(q_row, A_HEADS, A_HEAD_DIM)
            kc = c_refs[g][b, :, 0:A_W].astype(BF16)
            vc = c_refs[g][b, :, A_W:2 * A_W].astype(BF16)
            s = pl.dot(qm.astype(BF16), kc, trans_b=True)
            s_new = jnp.sum(qm * k_new, axis=-1, keepdims=True)
            m = jnp.maximum(jnp.max(s, axis=-1, keepdims=True), s_new)
            p = jnp.exp(s - m)
            p_new = jnp.exp(s_new - m)
            l = jnp.sum(p, axis=-1, keepdims=True) + p_new
            o = (jnp.dot(p.astype(BF16), vc, preferred_element_type=F32) + p_new * v_new) * (1.0 / l)
            outs.append(o)
            lses.append(m + jnp.log(l))
        mx = jnp.maximum(jnp.maximum(lses[0], lses[1]), lses[2])
        ex = [jnp.exp(x - mx) for x in lses]
        inv = 1.0 / (ex[0] + ex[1] + ex[2])
        oa = (ex[0] * outs[0] + ex[1] * outs[1] + ex[2] * outs[2]) * inv
        o_ref[b:b + 1, :] = jnp.sum(jnp.where(keep, oa, 0.0), axis=0, keepdims=True)


def _attn_sample(qs, ks, vs, caches, *, sb):
    bd = qs[0].shape[0]
    views = []
    c_specs = []
    for g in range(A_GROUPS):
        dil = A_DILATIONS[g]
        c = caches[g]
        wb = c.shape[1]
        assert wb == A_NKEY * dil, "window buffer must hold exactly 128 dilated keys"
        views.append(c.reshape(bd, wb // dil, dil * 2 * A_W))
        c_specs.append(pl.BlockSpec((sb, A_NKEY, 2 * A_W), lambda i: (i, 0, 0)))
    row = pl.BlockSpec((sb, A_W), lambda i: (i, 0))
    return pl.pallas_call(
        functools.partial(_attn_sample_kernel, sb=sb),
        out_shape=jax.ShapeDtypeStruct((bd, A_W), F32),
        grid=(bd // sb,),
        in_specs=[row] * 9 + c_specs,
        out_specs=row,
        compiler_params=_params(("parallel",), VMEM_LIMIT_BYTES),
        name="attn_sample",
    )(*qs, *ks, *vs, *views)


def _xattn_sample_kernel(q_ref, c_ref, o_ref, *, sb):
    for b in range(sb):
        qm, keep = _head_rows(q_ref[b:b + 1, :].astype(F32), XA_HEADS, XA_HEAD_DIM)
        kc = c_ref[b, :, 0:XA_W].astype(BF16)
        vc = c_ref[b, :, XA_W:2 * XA_W].astype(BF16)
        s = pl.dot(qm.astype(BF16), kc, trans_b=True) * (XA_HEAD_DIM ** -0.5)
        p, l = _softmax_rows(s)
        o = jnp.dot(p.astype(BF16), vc, preferred_element_type=F32) * (1.0 / l)
        o_ref[b:b + 1, :] = jnp.sum(jnp.where(keep, o, 0.0), axis=0, keepdims=True)


def _xattn_sample(cq, mem_cache, *, sb):
    bd = cq.shape[0]
    view = mem_cache.reshape(bd, MEM_LEN, 2 * XA_W)
    return pl.pallas_call(
        functools.partial(_xattn_sample_kernel, sb=sb),
        out_shape=jax.ShapeDtypeStruct((bd, XA_W), F32),
        grid=(bd // sb,),
        in_specs=[pl.BlockSpec((sb, XA_W), lambda i: (i, 0)),
                  pl.BlockSpec((sb, MEM_LEN, 2 * XA_W), lambda i: (i, 0, 0))],
        out_specs=pl.BlockSpec((sb, XA_W), lambda i: (i, 0)),
        compiler_params=_params(("parallel",)),
        name="xattn_sample",
    )(cq, view)


def _hgrn_sample_kernel(qt_ref, lft_ref, kt_ref, v_ref, s0_ref, gain_ref, o_ref, s_ref, *, sb):
    for b in range(sb):
        for h in range(HG_HEADS):
            sl = slice(h * HG_K, (h + 1) * HG_K)
            d_col = jnp.exp(lft_ref[0, sl, b:b + 1])
            k_col = kt_ref[0, sl, b:b + 1]
            q_col = qt_ref[0, sl, b:b + 1]
            v_row = v_ref[b:b + 1, sl]
            s_new = d_col * s0_ref[b, h] + k_col * v_row
            s_ref[b, h] = s_new
            o = jnp.sum(q_col * s_new, axis=0, keepdims=True)
            on = o * lax.rsqrt(jnp.mean(o * o, axis=-1, keepdims=True) + EPS) * gain_ref[:, sl]
            o_ref[b:b + 1, sl] = on


def _hgrn_sample(q, lf, kin, v, s0, gain, *, sb):
    bd = q.shape[0]
    steps = bd // sb

    def cols(a):
        return a.reshape(steps, sb, HG_W).transpose(0, 2, 1)

    col_spec = pl.BlockSpec((1, HG_W, sb), lambda i: (i, 0, 0))
    st_spec = pl.BlockSpec((sb, HG_HEADS, HG_K, HG_V), lambda i: (i, 0, 0, 0))
    row = pl.BlockSpec((sb, HG_W), lambda i: (i, 0))
    return pl.pallas_call(
        functools.partial(_hgrn_sample_kernel, sb=sb),
        out_shape=[jax.ShapeDtypeStruct((bd, HG_W), F32), jax.ShapeDtypeStruct(s0.shape, F32)],
        grid=(steps,),
        in_specs=[col_spec, col_spec, col_spec, row, st_spec, _resident((1, HG_W))],
        out_specs=[row, st_spec],
        compiler_params=_params(("parallel",)),
        name="hgrn_sample",
    )(cols(q), cols(lf), cols(kin), v, s0, gain)


def kernel(x_prompt, x_sample, mem_prompt, cache_win128_kv, cache_win512_kv, cache_win2048_kv, state_hgrn, cache_mem_kv, norm_pre, norm_post, w_in, hgrn_lb_logits, hgrn_out_norm, mem_norm, w_mem_kv, w_branch_a, w_branch_b, w_branch_c, w_out):
    depth = w_in.shape[0]
    assert depth == 1, "single-layer trunk"
    layer = 0
    bp, t, _ = x_prompt.shape
    bd, s_len, _ = x_sample.shape
    assert bp == 1 and s_len == 1
    caches = (cache_win128_kv[layer], cache_win512_kv[layer], cache_win2048_kv[layer])

    w_all = w_in[layer].astype(BF16)
    w_att, w_rest = w_all[:, :N_ATT], w_all[:, N_ATT:]
    pre_g = norm_pre[layer].reshape(1, D_MODEL)
    post_g = norm_post[layer].reshape(1, D_MODEL)
    hg_g = hgrn_out_norm[layer].reshape(1, HG_W)
    w_pa, w_pb, w_pc, w_o = (w[layer].astype(BF16) for w in (w_branch_a, w_branch_b, w_branch_c, w_out))
    half = A_HEAD_DIM // 2
    inv = ROPE_THETA ** (-jnp.arange(0, A_HEAD_DIM, 2, dtype=F32) / A_HEAD_DIM)
    invf = jnp.tile(inv, LANES // half).reshape(1, LANES)

    xp = x_prompt.reshape(t, D_MODEL)
    tail_rows = tuple(min(w, t) for w in A_WINDOWS)
    q_p, k_p, v_p, tails_p = _inproj_attn(xp, pre_g, w_att, invf, tm=512, pos_base=0, pos_step=1,
                                          tail_rows=tail_rows, out_dtype=BF16)
    (az, bq, lf, kin, bi, bz, cq, cz, ga, gb, gc) = _inproj_rest(
        xp, pre_g, w_rest, hgrn_lb_logits, tm=512, layer=layer, hg_dtype=BF16)
    att = [_attn_prompt(q_p[g], k_p[g], v_p[g], A_DILATIONS[g]) for g in range(A_GROUPS)]
    ob_p, s_p = _hgrn_prompt(bq, lf, kin, bi, hg_g, th=512)
    mem_kv = _mem_kv(mem_prompt.reshape(MEM_LEN, D_MODEL), mem_norm[layer].reshape(1, D_MODEL),
                     w_mem_kv[layer].astype(BF16))
    mk = mem_kv[:, :XA_W].astype(BF16)
    mv = mem_kv[:, XA_W:].astype(BF16)
    y_p = _final([a[0] for a in att] + [a[1] for a in att], [cq, mk, mv], ob_p, az, bz, cz, ga, gb, gc,
                 xp, post_g, w_pa, w_pb, w_pc, w_o, tm=256, merge=True, xattn=True)

    xs = x_sample.reshape(bd, D_MODEL)
    q_s, k_s, v_s, tails_s = _inproj_attn(xs, pre_g, w_att, invf, tm=bd, pos_base=PAST_LEN, pos_step=0,
                                          tail_rows=(bd,) * A_GROUPS, out_dtype=F32)
    (az_s, bq_s, lf_s, kin_s, bi_s, bz_s, cq_s, cz_s, ga_s, gb_s, gc_s) = _inproj_rest(
        xs, pre_g, w_rest, hgrn_lb_logits, tm=bd, layer=layer, hg_dtype=F32)
    oa_s = _attn_sample(q_s, k_s, v_s, caches, sb=8)
    oc_s = _xattn_sample(cq_s, cache_mem_kv[layer], sb=8)
    ob_s, s_s = _hgrn_sample(bq_s, lf_s, kin_s, bi_s, state_hgrn[layer], hg_g, sb=8)
    y_s = _final([oa_s], [oc_s], ob_s, az_s, bz_s, cz_s, ga_s, gb_s, gc_s,
                 xs, post_g, w_pa, w_pb, w_pc, w_o, tm=bd, merge=False, xattn=False)

    new_win_p = [tails_p[g].reshape(1, 1, tail_rows[g], 2, A_HEADS, A_HEAD_DIM) for g in range(A_GROUPS)]
    new_win_s = [tails_s[g].reshape(1, bd, 1, 2, A_HEADS, A_HEAD_DIM) for g in range(A_GROUPS)]
    return (y_p.reshape(bp, t, D_MODEL), y_s.reshape(bd, 1, D_MODEL),
            new_win_p[0], new_win_p[1], new_win_p[2],
            s_p.reshape(1, 1, HG_HEADS, HG_K, HG_V),
            mem_kv.reshape(1, 1, MEM_LEN, 2, XA_HEADS, XA_HEAD_DIM),
            new_win_s[0], new_win_s[1], new_win_s[2],
            s_s.reshape(1, bd, HG_HEADS, HG_K, HG_V))
```

```python
import functools

import jax
import jax.numpy as jnp
from jax import lax
from jax.experimental import pallas as pl
from jax.experimental.pallas import tpu as pltpu

D_MODEL = 1024
PAST_LEN = 8192
A_WINDOWS = (128, 512, 2048)
A_DILATIONS = (1, 4, 16)
A_GROUPS = 3
A_HEADS = 8
A_HEAD_DIM = 64
A_NKEY = 128
ROPE_THETA = 10000.0
HG_HEADS = 4
HG_K = 128
HG_V = 128
HG_CHUNK = 64
HG_BLOCK = 8
MEM_LEN = 256
XA_HEADS = 4
XA_HEAD_DIM = 128
EPS = 1e-6

A_W = A_HEADS * A_HEAD_DIM
HG_W = HG_HEADS * HG_K
XA_W = XA_HEADS * XA_HEAD_DIM
N_ATT = 3 * A_GROUPS * A_W
COL_AZ = N_ATT
COL_BQ = COL_AZ + A_W
COL_BF = COL_BQ + HG_W
COL_BI = COL_BF + HG_W
COL_BZ = COL_BI + HG_W
COL_CQ = COL_BZ + HG_W
COL_CZ = COL_CQ + XA_W
COL_GA = COL_CZ + XA_W
COL_GB = COL_GA + D_MODEL
COL_GC = COL_GB + D_MODEL
N_IN = COL_GC + D_MODEL

LANES = 128
VMEM_LIMIT_BYTES = 56 * 1024 * 1024

F32 = jnp.float32
BF16 = jnp.bfloat16
NEG_INF = float("-inf")


def _params(semantics, vmem=None):
    return pltpu.CompilerParams(dimension_semantics=semantics, vmem_limit_bytes=vmem)


def _resident(shape):
    return pl.BlockSpec(shape, lambda *_: (0,) * len(shape), pipeline_mode=pl.Buffered(1))


def _rmsnorm_rows(x, gain):
    ms = jnp.mean(x * x, axis=-1, keepdims=True)
    return x * lax.rsqrt(ms + EPS) * gain


def _sigmoid(x):
    return 1.0 / (1.0 + jnp.exp(-x))


def _silu(x):
    return x * _sigmoid(x)


def _inproj_attn_kernel(x_ref, g_ref, w_ref, invf_ref, *refs, tm, n_tiles, pos_base, pos_step, tails, row_dil):
    q_refs = refs[0:3]
    k_refs = refs[3:6]
    v_refs = refs[6:9]
    t_refs = refs[9:12]
    perm_ref = refs[12]
    i = pl.program_id(0)
    n_slab = A_W // LANES

    xn = _rmsnorm_rows(x_ref[...], g_ref[...]).astype(BF16)

    row = lax.broadcasted_iota(jnp.int32, (tm, LANES), 0)
    lane = lax.broadcasted_iota(jnp.int32, (tm, LANES), 1)
    pos = (pos_base + (i * tm + row) * pos_step).astype(F32)
    ang = pos * invf_ref[...]
    cos = jnp.cos(ang)
    sin = jnp.sin(ang)
    first_half = (lane % A_HEAD_DIM) < (A_HEAD_DIM // 2)
    sin_signed = jnp.where(first_half, -sin, sin)

    def rope(acc):
        outs = []
        for c in range(n_slab):
            xc = acc[:, c * LANES:(c + 1) * LANES]
            partner = jnp.where(first_half, pltpu.roll(xc, LANES - 32, 1), pltpu.roll(xc, 32, 1))
            outs.append(xc * cos + partner * sin_signed)
        return jnp.concatenate(outs, axis=-1)

    def store_rows(ref, g, val):
        d = row_dil[g]
        if d == 1:
            ref[...] = val.astype(ref.dtype)
            return
        for c in range(n_slab):
            perm_ref[c] = val[:, c * LANES:(c + 1) * LANES]
        for r in range(d):
            for c in range(n_slab):
                c0 = r * A_W + c * LANES
                ref[:, c0:c0 + LANES] = perm_ref[c, pl.ds(r, tm // d, stride=d), :].astype(ref.dtype)

    def write_tail(g, col0, val):
        first_tile, rows = tails[g]
        if rows >= tm:
            @pl.when(i >= first_tile)
            def _():
                t_refs[g][:, col0:col0 + A_W] = val
        else:
            @pl.when(i == n_tiles - 1)
            def _():
                t_refs[g][:, col0:col0 + A_W] = val[tm - rows:, :]

    for g in range(A_GROUPS):
        acc = jnp.dot(xn, w_ref[:, g * A_W:(g + 1) * A_W], preferred_element_type=F32)
        store_rows(q_refs[g], g, rope(acc) * (A_HEAD_DIM ** -0.5))
    for g in range(A_GROUPS):
        c0 = (A_GROUPS + g) * A_W
        kr = rope(jnp.dot(xn, w_ref[:, c0:c0 + A_W], preferred_element_type=F32))
        store_rows(k_refs[g], g, kr)
        write_tail(g, 0, kr)
    for g in range(A_GROUPS):
        c0 = (2 * A_GROUPS + g) * A_W
        acc = jnp.dot(xn, w_ref[:, c0:c0 + A_W], preferred_element_type=F32)
        store_rows(v_refs[g], g, acc)
        write_tail(g, A_W, acc)


def _inproj_attn(x, gain, w_att, invf, *, tm, pos_base, pos_step, tail_rows, out_dtype, row_dil):
    n = x.shape[0]
    n_tiles = n // tm
    tails = []
    tail_specs = []
    tail_shapes = []
    for g in range(A_GROUPS):
        rows = tail_rows[g]
        first_tile = (n - rows) // tm
        tails.append((first_tile, rows))
        blk = min(tm, rows)
        tail_specs.append(pl.BlockSpec((blk, 2 * A_W), functools.partial(
            lambda i, ft: (jnp.maximum(i - ft, 0), 0), ft=first_tile)))
        tail_shapes.append(jax.ShapeDtypeStruct((rows, 2 * A_W), F32))
    row_specs = [pl.BlockSpec((tm // d, d * A_W), lambda i: (i, 0)) for d in row_dil] * 3
    row_shapes = [jax.ShapeDtypeStruct((n // d, d * A_W), out_dtype) for d in row_dil] * 3
    kernel = functools.partial(_inproj_attn_kernel, tm=tm, n_tiles=n_tiles, pos_base=pos_base,
                               pos_step=pos_step, tails=tuple(tails), row_dil=tuple(row_dil))
    outs = pl.pallas_call(
        kernel,
        out_shape=row_shapes + tail_shapes,
        grid=(n_tiles,),
        in_specs=[pl.BlockSpec((tm, D_MODEL), lambda i: (i, 0)),
                  _resident((1, D_MODEL)),
                  _resident((D_MODEL, N_ATT)),
                  _resident((1, LANES))],
        out_specs=row_specs + tail_specs,
        scratch_shapes=[pltpu.VMEM((A_W // LANES, tm, LANES), F32)],
        compiler_params=_params(("arbitrary",), VMEM_LIMIT_BYTES),
        name="inproj_attn",
    )(x, gain, w_att, invf)
    return outs[0:3], outs[3:6], outs[6:9], outs[9:12]


def _inproj_rest_kernel(x_ref, g_ref, w_ref, lbl_ref, az_ref, bq_ref, lf_ref, kin_ref, bi_ref, bz_ref,
                        cq_ref, cz_ref, ga_ref, gb_ref, gc_ref, *, layer):
    xn = _rmsnorm_rows(x_ref[...], g_ref[...]).astype(BF16)

    def proj(col, width):
        c0 = col - N_ATT
        return jnp.dot(xn, w_ref[:, c0:c0 + width], preferred_element_type=F32)

    logits = lbl_ref[...]
    e = jnp.exp(logits - jnp.max(logits, axis=0, keepdims=True))
    lb = jnp.sum(e[0:layer + 1, :], axis=0, keepdims=True) / jnp.sum(e, axis=0, keepdims=True)

    az_ref[...] = proj(COL_AZ, A_W).astype(az_ref.dtype)
    bq_ref[...] = proj(COL_BQ, HG_W).astype(bq_ref.dtype)
    bf = proj(COL_BF, HG_W)
    lf_ref[...] = jnp.log(lb + (1.0 - lb) * _sigmoid(bf))
    kin_ref[...] = ((1.0 - lb) * _sigmoid(-bf)).astype(kin_ref.dtype)
    bi_ref[...] = proj(COL_BI, HG_W).astype(bi_ref.dtype)
    bz_ref[...] = proj(COL_BZ, HG_W).astype(bz_ref.dtype)
    cq_ref[...] = proj(COL_CQ, XA_W).astype(cq_ref.dtype)
    cz_ref[...] = proj(COL_CZ, XA_W).astype(cz_ref.dtype)
    ga_ref[...] = proj(COL_GA, D_MODEL).astype(ga_ref.dtype)
    gb_ref[...] = proj(COL_GB, D_MODEL).astype(gb_ref.dtype)
    gc_ref[...] = proj(COL_GC, D_MODEL).astype(gc_ref.dtype)


def _inproj_rest(x, gain, w_rest, lb_logits, *, tm, layer, hg_dtype):
    n = x.shape[0]
    n_rest = N_IN - N_ATT
    s512 = pl.BlockSpec((tm, 512), lambda i: (i, 0))
    s1024 = pl.BlockSpec((tm, D_MODEL), lambda i: (i, 0))

    def sds(width, dt):
        return jax.ShapeDtypeStruct((n, width), dt)

    return pl.pallas_call(
        functools.partial(_inproj_rest_kernel, layer=layer),
        out_shape=[sds(A_W, BF16), sds(HG_W, hg_dtype), sds(HG_W, F32), sds(HG_W, hg_dtype), sds(HG_W, hg_dtype),
                   sds(HG_W, BF16), sds(XA_W, BF16), sds(XA_W, BF16),
                   sds(D_MODEL, BF16), sds(D_MODEL, BF16), sds(D_MODEL, BF16)],
        grid=(n // tm,),
        in_specs=[pl.BlockSpec((tm, D_MODEL), lambda i: (i, 0)),
                  _resident((1, D_MODEL)),
                  _resident((D_MODEL, n_rest)),
                  _resident(lb_logits.shape)],
        out_specs=[s512] * 8 + [s1024] * 3,
        compiler_params=_params(("parallel",), VMEM_LIMIT_BYTES),
        name="inproj_rest",
    )(x, gain, w_rest, lb_logits)


def _attn_prompt_kernel(q_ref, kc_ref, vc_ref, kp_ref, vp_ref, o_ref, st_ref, kbuf, vbuf, o_acc, st_acc,
                        *, tmv, dil, rps):
    n = pl.program_id(0)
    rstep = pl.program_id(1)
    blk = A_NKEY

    qi = lax.broadcasted_iota(jnp.int32, (blk, 2 * blk), 0)
    ki = lax.broadcasted_iota(jnp.int32, (blk, 2 * blk), 1)
    dist = qi + blk - ki
    band = (dist >= 0) & (dist <= A_NKEY)
    bias = jnp.where(band, 0.0, NEG_INF).astype(F32)
    bias_first = jnp.where(band & (ki >= blk), 0.0, NEG_INF).astype(F32)
    lane = lax.broadcasted_iota(jnp.int32, (blk, LANES), 1)
    low_head = lane < A_HEAD_DIM
    head_mask = (jnp.where(low_head, 1.0, 0.0).astype(BF16), jnp.where(low_head, 0.0, 1.0).astype(BF16))

    for rr in range(rps):
        cs = slice(rr * A_W, (rr + 1) * A_W)
        kbuf[0:blk, :] = kp_ref[:, cs]
        kbuf[blk:, :] = kc_ref[:, cs]
        vbuf[0:blk, :] = vp_ref[:, cs]
        vbuf[blk:, :] = vc_ref[:, cs]
        res = rstep * rps + rr

        def body(b, carry, cs=cs, res=res):
            r0 = pl.multiple_of(b * blk, blk)
            is_first = jnp.logical_and(n == 0, b == 0)
            bias_b = jnp.where(is_first, bias_first, bias)
            qb = q_ref[pl.ds(r0, blk), cs]
            kb = kbuf[pl.ds(r0, 2 * blk), :]
            vb = vbuf[pl.ds(r0, 2 * blk), :]
            out_rows = pl.ds(r0 * dil + res, blk, stride=dil)
            stats = jnp.zeros((blk, LANES), F32)
            for pair in range(A_HEADS // 2):
                sl = slice(pair * LANES, (pair + 1) * LANES)
                qp = qb[:, sl]
                kp = kb[:, sl]
                vp = vb[:, sl]
                outs = []
                for hh in range(2):
                    qm = qp * head_mask[hh]
                    s = pl.dot(qm, kp, trans_b=True) + bias_b
                    m = jnp.max(s, axis=-1, keepdims=True)
                    p = jnp.exp(s - m)
                    l = jnp.sum(p, axis=-1, keepdims=True)
                    pv = jnp.dot(p.astype(BF16), vp, preferred_element_type=F32)
                    outs.append(pv * (1.0 / l))
                    lse = m + jnp.log(l)
                    stats = jnp.where(lane == 2 * pair + hh, lse, stats)
                o_acc[pair, out_rows, :] = jnp.where(low_head, outs[0], outs[1])
            st_acc[out_rows, :] = stats
            return carry

        lax.fori_loop(0, tmv // blk, body, 0)

    @pl.when(rstep == pl.num_programs(1) - 1)
    def _():
        for pair in range(A_HEADS // 2):
            o_ref[:, pair * LANES:(pair + 1) * LANES] = o_acc[pair].astype(o_ref.dtype)
        st_ref[...] = st_acc[...]


def _attn_prompt(qv, kv, vv, dil, *, rows_per_step):
    l = qv.shape[0]
    t = l * dil
    blk = A_NKEY
    tmv = rows_per_step // dil
    rps = max(1, blk // tmv)
    tmv = max(tmv, blk)
    assert t % (tmv * dil) == 0 and dil % rps == 0
    cur = pl.BlockSpec((tmv, rps * A_W), lambda n, r: (n, r))
    prev = pl.BlockSpec((blk, rps * A_W), lambda n, r: (jnp.maximum(n * (tmv // blk) - 1, 0), r))
    rows = tmv * dil
    return pl.pallas_call(
        functools.partial(_attn_prompt_kernel, tmv=tmv, dil=dil, rps=rps),
        out_shape=[jax.ShapeDtypeStruct((t, A_W), BF16), jax.ShapeDtypeStruct((t, LANES), F32)],
        grid=(t // rows, dil // rps),
        in_specs=[cur, cur, cur, prev, prev],
        out_specs=[pl.BlockSpec((rows, A_W), lambda n, r: (n, 0)), pl.BlockSpec((rows, LANES), lambda n, r: (n, 0))],
        scratch_shapes=[pltpu.VMEM((tmv + blk, A_W), BF16), pltpu.VMEM((tmv + blk, A_W), BF16),
                        pltpu.VMEM((A_HEADS // 2, rows, LANES), F32), pltpu.VMEM((rows, LANES), F32)],
        compiler_params=_params(("parallel", "arbitrary"), VMEM_LIMIT_BYTES),
        name=f"attn_prompt_d{dil}",
    )(qv, kv, vv, kv, vv)


def _shift_rows(x, s):
    return pltpu.roll(x, s, 0)


def _chunk_cumsum(g, row):
    b = g
    s = 1
    while s < g.shape[0]:
        b = b + jnp.where(row >= s, _shift_rows(b, s), 0.0)
        s *= 2
    return b


def _hgrn_prompt_kernel(q_ref, lf_ref, k_ref, v_ref, gain_ref, o_ref, s_out_ref, st_ref, *, th):
    i = pl.program_id(0)
    c_rows = HG_CHUNK
    nblk = c_rows // HG_BLOCK

    @pl.when(i == 0)
    def _():
        st_ref[...] = jnp.zeros_like(st_ref)

    row = lax.broadcasted_iota(jnp.int32, (c_rows, HG_K), 0)
    row_in_blk = row % HG_BLOCK
    a_row = lax.broadcasted_iota(jnp.int32, (c_rows, c_rows), 0) // HG_BLOCK
    a_col = lax.broadcasted_iota(jnp.int32, (c_rows, c_rows), 1) // HG_BLOCK

    def chunk(c, carry):
        r0 = pl.multiple_of(c * c_rows, c_rows)
        for h in range(HG_HEADS):
            sl = slice(h * HG_K, (h + 1) * HG_K)
            q = q_ref[pl.ds(r0, c_rows), sl].astype(F32)
            k = k_ref[pl.ds(r0, c_rows), sl].astype(F32)
            v = v_ref[pl.ds(r0, c_rows), sl].astype(F32)
            g = lf_ref[pl.ds(r0, c_rows), sl]
            b = _chunk_cumsum(g, row)
            b_last = b[c_rows - 1:c_rows, :]

            o = jnp.sum(q * k, axis=-1, keepdims=True) * v
            for d in range(1, HG_BLOCK):
                valid = row_in_blk >= d
                dec = jnp.exp(jnp.where(valid, b - _shift_rows(b, d), NEG_INF))
                a = jnp.sum(q * _shift_rows(k, d) * dec, axis=-1, keepdims=True)
                o = o + a * _shift_rows(v, d)

            b3 = b.reshape(nblk, HG_BLOCK, HG_K)
            b_end = jnp.broadcast_to(b3[:, HG_BLOCK - 1:HG_BLOCK, :], b3.shape).reshape(c_rows, HG_K)
            k_hat = (k * jnp.exp(b_end - b)).astype(BF16)
            q_ver = [(q * jnp.exp(jnp.minimum(b - b[HG_BLOCK * j + HG_BLOCK - 1:HG_BLOCK * (j + 1), :], 0.0))).astype(BF16)
                     for j in range(nblk - 1)]
            r = pl.dot(jnp.concatenate(q_ver, axis=0), k_hat, trans_b=True)
            att = jnp.zeros((c_rows, c_rows), F32)
            for j in range(nblk - 1):
                att = jnp.where(a_col == j, r[j * c_rows:(j + 1) * c_rows, :], att)
            att = jnp.where(a_row > a_col, att, 0.0)
            vb = v.astype(BF16)
            o = o + jnp.dot(att.astype(BF16), vb, preferred_element_type=F32)

            st = st_ref[h]
            o = o + pl.dot((q * jnp.exp(b)).astype(BF16), st.astype(BF16), trans_b=True)
            k_end = (k * jnp.exp(b_last - b)).astype(BF16)
            st_ref[h] = st * jnp.exp(b_last) + pl.dot(vb, k_end, trans_a=True)

            on = o * lax.rsqrt(jnp.mean(o * o, axis=-1, keepdims=True) + EPS) * gain_ref[:, sl]
            o_ref[pl.ds(r0, c_rows), sl] = on.astype(o_ref.dtype)
        return carry

    lax.fori_loop(0, th // c_rows, chunk, 0)

    @pl.when(i == pl.num_programs(0) - 1)
    def _():
        for h in range(HG_HEADS):
            s_out_ref[h] = st_ref[h].T


def _hgrn_prompt(q, lf, k, v, gain, *, th):
    t = q.shape[0]
    spec = pl.BlockSpec((th, HG_W), lambda i: (i, 0))
    return pl.pallas_call(
        functools.partial(_hgrn_prompt_kernel, th=th),
        out_shape=[jax.ShapeDtypeStruct((t, HG_W), BF16),
                   jax.ShapeDtypeStruct((HG_HEADS, HG_K, HG_V), F32)],
        grid=(t // th,),
        in_specs=[spec, spec, spec, spec, _resident((1, HG_W))],
        out_specs=[spec, pl.BlockSpec((HG_HEADS, HG_K, HG_V), lambda i: (0, 0, 0))],
        scratch_shapes=[pltpu.VMEM((HG_HEADS, HG_V, HG_K), F32)],
        compiler_params=_params(("arbitrary",)),
        name="hgrn_prompt",
    )(q, lf, k, v, gain)


def _mem_kv_kernel(m_ref, g_ref, w_ref, o_ref):
    xn = _rmsnorm_rows(m_ref[...], g_ref[...]).astype(BF16)
    o_ref[...] = jnp.dot(xn, w_ref[...], preferred_element_type=F32)


def _mem_kv(mem, gain, w):
    m = mem.shape[0]
    return pl.pallas_call(
        _mem_kv_kernel,
        out_shape=jax.ShapeDtypeStruct((m, 2 * XA_W), F32),
        name="mem_kv",
    )(mem, gain, w)


def _softmax_rows(s):
    m = jnp.max(s, axis=-1, keepdims=True)
    p = jnp.exp(s - m)
    return p, jnp.sum(p, axis=-1, keepdims=True)


def _final_kernel(*refs, tm, merge, xattn):
    refs = list(refs)
    if merge:
        o_refs = [refs.pop(0) for _ in range(A_GROUPS)]
        st_refs = [refs.pop(0) for _ in range(A_GROUPS)]
    else:
        oa_ref = refs.pop(0)
    if xattn:
        cq_ref, mk_ref, mv_ref = (refs.pop(0) for _ in range(3))
    else:
        oc_ref = refs.pop(0)
    (ob_ref, az_ref, bz_ref, cz_ref, ga_ref, gb_ref, gc_ref, x_ref, pg_ref,
     wa_ref, wb_ref, wc_ref, wo_ref, out_ref) = refs

    if merge:
        lse = [r[...] for r in st_refs]
        mx = jnp.maximum(jnp.maximum(lse[0], lse[1]), lse[2])
        ex = [jnp.exp(x - mx) for x in lse]
        inv = 1.0 / (ex[0] + ex[1] + ex[2])
        alpha = [e * inv for e in ex]
        lane_head = lax.broadcasted_iota(jnp.int32, (tm, A_W), 1) // A_HEAD_DIM
        oa = jnp.zeros((tm, A_W), F32)
        for g in range(A_GROUPS):
            w = jnp.zeros((tm, A_W), F32)
            for h in range(A_HEADS):
                w = jnp.where(lane_head == h, alpha[g][:, h:h + 1], w)
            oa = oa + w * o_refs[g][...].astype(F32)
    else:
        oa = oa_ref[...].astype(F32)
    ya = (oa * _silu(az_ref[...].astype(F32))).astype(BF16)

    if xattn:
        cq = cq_ref[...]
        parts = []
        for h in range(XA_HEADS):
            sl = slice(h * XA_HEAD_DIM, (h + 1) * XA_HEAD_DIM)
            s = pl.dot(cq[:, sl], mk_ref[:, sl], trans_b=True) * (XA_HEAD_DIM ** -0.5)
            p, l = _softmax_rows(s)
            parts.append(jnp.dot(p.astype(BF16), mv_ref[:, sl], preferred_element_type=F32) * (1.0 / l))
        oc = jnp.concatenate(parts, axis=-1)
    else:
        oc = oc_ref[...].astype(F32)
    yc = (oc * _silu(cz_ref[...].astype(F32))).astype(BF16)

    yb = (ob_ref[...].astype(F32) * _silu(bz_ref[...].astype(F32))).astype(BF16)

    merged = (_sigmoid(ga_ref[...].astype(F32)) * jnp.dot(ya, wa_ref[...], preferred_element_type=F32)
              + _sigmoid(gb_ref[...].astype(F32)) * jnp.dot(yb, wb_ref[...], preferred_element_type=F32)
              + _sigmoid(gc_ref[...].astype(F32)) * jnp.dot(yc, wc_ref[...], preferred_element_type=F32))
    z = jnp.dot(merged.astype(BF16), wo_ref[...], preferred_element_type=F32)
    out_ref[...] = x_ref[...] + _rmsnorm_rows(z, pg_ref[...])


def _final(att_in, xa_in, ob, az, bz, cz, ga, gb, gc, x, post_g, w_pa, w_pb, w_pc, w_out, *, tm, merge, xattn):
    n = x.shape[0]

    def rows(width):
        return pl.BlockSpec((tm, width), lambda i: (i, 0))

    if merge:
        att_specs = [rows(A_W)] * A_GROUPS + [rows(LANES)] * A_GROUPS
    else:
        att_specs = [rows(A_W)]
    if xattn:
        xa_specs = [rows(XA_W), _resident((MEM_LEN, XA_W)), _resident((MEM_LEN, XA_W))]
    else:
        xa_specs = [rows(XA_W)]
    in_specs = (att_specs + xa_specs + [rows(HG_W), rows(A_W), rows(HG_W), rows(XA_W)]
                + [rows(D_MODEL)] * 4 + [_resident((1, D_MODEL))]
                + [_resident((A_W, D_MODEL)), _resident((HG_W, D_MODEL)), _resident((XA_W, D_MODEL)),
                   _resident((D_MODEL, D_MODEL))])
    return pl.pallas_call(
        functools.partial(_final_kernel, tm=tm, merge=merge, xattn=xattn),
        out_shape=jax.ShapeDtypeStruct((n, D_MODEL), F32),
        grid=(n // tm,),
        in_specs=in_specs,
        out_specs=rows(D_MODEL),
        compiler_params=_params(("parallel",), VMEM_LIMIT_BYTES),
        name="final_merge" if merge else "final",
    )(*att_in, *xa_in, ob, az, bz, cz, ga, gb, gc, x, post_g, w_pa, w_pb, w_pc, w_out)


def _head_rows(q_row, head_dim):
    w = q_row.shape[-1]
    sub = lax.broadcasted_iota(jnp.int32, (8, w), 0)
    lane_head = lax.broadcasted_iota(jnp.int32, (8, w), 1) // head_dim
    keep = sub == lane_head
    return jnp.where(keep, jnp.broadcast_to(q_row, (8, w)), 0.0), keep


def _attn_sample_kernel(q0, q1, q2, k0, k1, k2, v0, v1, v2, c0, c1, c2, o_ref):
    q_refs = (q0, q1, q2)
    k_refs = (k0, k1, k2)
    v_refs = (v0, v1, v2)
    c_refs = (c0, c1, c2)
    b = pl.program_id(0)
    outs = []
    lses = []
    keep = None
    for g in range(A_GROUPS):
        dil = A_DILATIONS[g]
        wb = c_refs[g].shape[-1]
        q_row = q_refs[g][pl.ds(b, 1), :]
        k_new = k_refs[g][pl.ds(b, 1), :]
        v_new = v_refs[g][pl.ds(b, 1), :]
        qm, keep = _head_rows(q_row, A_HEAD_DIM)
        kt = c_refs[g][0].astype(BF16)
        vt = c_refs[g][1].astype(BF16)
        s = jnp.dot(qm.astype(BF16), kt, preferred_element_type=F32)
        w_pos = lax.broadcasted_iota(jnp.int32, (8, wb), 1)
        s = jnp.where(w_pos % dil == 0, s, NEG_INF)
        s_new = jnp.sum(qm * k_new, axis=-1, keepdims=True)
        m = jnp.maximum(jnp.max(s, axis=-1, keepdims=True), s_new)
        p = jnp.exp(s - m)
        p_new = jnp.exp(s_new - m)
        l = jnp.sum(p, axis=-1, keepdims=True) + p_new
        o = (pl.dot(p.astype(BF16), vt, trans_b=True) + p_new * v_new) * (1.0 / l)
        outs.append(o)
        lses.append(m + jnp.log(l))
    mx = jnp.maximum(jnp.maximum(lses[0], lses[1]), lses[2])
    ex = [jnp.exp(x - mx) for x in lses]
    inv = 1.0 / (ex[0] + ex[1] + ex[2])
    oa = (ex[0] * outs[0] + ex[1] * outs[1] + ex[2] * outs[2]) * inv
    o_ref[pl.ds(b, 1), :] = jnp.sum(jnp.where(keep, oa, 0.0), axis=0, keepdims=True)


def _attn_sample(qs, ks, vs, caches):
    bd = qs[0].shape[0]
    views = []
    c_specs = []
    for g in range(A_GROUPS):
        c = caches[g]
        wb = c.shape[1]
        assert wb == A_NKEY * A_DILATIONS[g], "window buffer must hold exactly 128 dilated keys"
        views.append(jnp.transpose(c, (0, 2, 3, 4, 1)).reshape(bd, 2, A_W, wb))
        c_specs.append(pl.BlockSpec((None, 2, A_W, wb), lambda b: (b, 0, 0, 0)))
    full = pl.BlockSpec((bd, A_W), lambda b: (0, 0))
    return pl.pallas_call(
        _attn_sample_kernel,
        out_shape=jax.ShapeDtypeStruct((bd, A_W), F32),
        grid=(bd,),
        in_specs=[full] * 9 + c_specs,
        out_specs=full,
        compiler_params=_params(("arbitrary",), VMEM_LIMIT_BYTES),
        name="attn_sample",
    )(*qs, *ks, *vs, *views)


def _xattn_sample_kernel(q_ref, c_ref, o_ref, *, sb):
    stride = 2 * XA_HEADS
    for b in range(sb):
        for h in range(XA_HEADS):
            sl = slice(h * XA_HEAD_DIM, (h + 1) * XA_HEAD_DIM)
            kh = c_ref[b, pl.ds(h, MEM_LEN, stride=stride), :]
            vh = c_ref[b, pl.ds(XA_HEADS + h, MEM_LEN, stride=stride), :]
            q_h = q_ref[b:b + 1, sl].astype(F32)
            s = jnp.sum(kh * q_h, axis=-1, keepdims=True) * (XA_HEAD_DIM ** -0.5)
            m = jnp.max(s, axis=0, keepdims=True)
            p = jnp.exp(s - m)
            l = jnp.sum(p, axis=0, keepdims=True)
            o_ref[b:b + 1, sl] = jnp.sum(p * vh, axis=0, keepdims=True) * (1.0 / l)


def _xattn_sample(cq, mem_cache, *, sb):
    bd = cq.shape[0]
    rows = MEM_LEN * 2 * XA_HEADS
    view = mem_cache.reshape(bd, rows, XA_HEAD_DIM)
    return pl.pallas_call(
        functools.partial(_xattn_sample_kernel, sb=sb),
        out_shape=jax.ShapeDtypeStruct((bd, XA_W), F32),
        grid=(bd // sb,),
        in_specs=[pl.BlockSpec((sb, XA_W), lambda i: (i, 0)),
                  pl.BlockSpec((sb, rows, XA_HEAD_DIM), lambda i: (i, 0, 0))],
        out_specs=pl.BlockSpec((sb, XA_W), lambda i: (i, 0)),
        compiler_params=_params(("parallel",), VMEM_LIMIT_BYTES),
        name="xattn_sample",
    )(cq, view)


def _hgrn_sample_kernel(qt_ref, lft_ref, kt_ref, v_ref, s0_ref, gain_ref, o_ref, s_ref, *, sb):
    for b in range(sb):
        for h in range(HG_HEADS):
            sl = slice(h * HG_K, (h + 1) * HG_K)
            d_col = jnp.exp(lft_ref[0, sl, b:b + 1])
            k_col = kt_ref[0, sl, b:b + 1]
            q_col = qt_ref[0, sl, b:b + 1]
            v_row = v_ref[b:b + 1, sl]
            s_new = d_col * s0_ref[b, h] + k_col * v_row
            s_ref[b, h] = s_new
            o = jnp.sum(q_col * s_new, axis=0, keepdims=True)
            on = o * lax.rsqrt(jnp.mean(o * o, axis=-1, keepdims=True) + EPS) * gain_ref[:, sl]
            o_ref[b:b + 1, sl] = on


def _hgrn_sample(q, lf, kin, v, s0, gain, *, sb):
    bd = q.shape[0]
    steps = bd // sb

    def cols(a):
        return a.reshape(steps, sb, HG_W).transpose(0, 2, 1)

    col_spec = pl.BlockSpec((1, HG_W, sb), lambda i: (i, 0, 0))
    st_spec = pl.BlockSpec((sb, HG_HEADS, HG_K, HG_V), lambda i: (i, 0, 0, 0))
    row = pl.BlockSpec((sb, HG_W), lambda i: (i, 0))
    return pl.pallas_call(
        functools.partial(_hgrn_sample_kernel, sb=sb),
        out_shape=[jax.ShapeDtypeStruct((bd, HG_W), F32), jax.ShapeDtypeStruct(s0.shape, F32)],
        grid=(steps,),
        in_specs=[col_spec, col_spec, col_spec, row, st_spec, _resident((1, HG_W))],
        out_specs=[row, st_spec],
        compiler_params=_params(("parallel",)),
        name="hgrn_sample",
    )(cols(q), cols(lf), cols(kin), v, s0, gain)


def kernel(x_prompt, x_sample, mem_prompt, cache_win128_kv, cache_win512_kv, cache_win2048_kv, state_hgrn, cache_mem_kv, norm_pre, norm_post, w_in, hgrn_lb_logits, hgrn_out_norm, mem_norm, w_mem_kv, w_branch_a, w_branch_b, w_branch_c, w_out):
    depth = w_in.shape[0]
    assert depth == 1, "single-layer trunk"
    layer = 0
    bp, t, _ = x_prompt.shape
    bd, s_len, _ = x_sample.shape
    assert bp == 1 and s_len == 1
    caches = (cache_win128_kv[layer], cache_win512_kv[layer], cache_win2048_kv[layer])

    w_att = w_in[layer][:, :N_ATT].astype(BF16)
    w_rest = w_in[layer][:, N_ATT:].astype(BF16)
    pre_g = norm_pre[layer].reshape(1, D_MODEL)
    post_g = norm_post[layer].reshape(1, D_MODEL)
    hg_g = hgrn_out_norm[layer].reshape(1, HG_W)
    w_pa, w_pb, w_pc, w_o = (w[layer].astype(BF16) for w in (w_branch_a, w_branch_b, w_branch_c, w_out))
    half = A_HEAD_DIM // 2
    inv = ROPE_THETA ** (-jnp.arange(0, A_HEAD_DIM, 2, dtype=F32) / A_HEAD_DIM)
    invf = jnp.tile(inv, LANES // half).reshape(1, LANES)

    xp = x_prompt.reshape(t, D_MODEL)
    tail_rows = tuple(min(w, t) for w in A_WINDOWS)
    q_p, k_p, v_p, tails_p = _inproj_attn(xp, pre_g, w_att, invf, tm=512, pos_base=0, pos_step=1,
                                          tail_rows=tail_rows, out_dtype=BF16, row_dil=A_DILATIONS)
    (az, bq, lf, kin, bi, bz, cq, cz, ga, gb, gc) = _inproj_rest(
        xp, pre_g, w_rest, hgrn_lb_logits, tm=512, layer=layer, hg_dtype=BF16)
    att = [_attn_prompt(q_p[g], k_p[g], v_p[g], A_DILATIONS[g], rows_per_step=2048) for g in range(A_GROUPS)]
    ob_p, s_p = _hgrn_prompt(bq, lf, kin, bi, hg_g, th=512)
    mem_kv = _mem_kv(mem_prompt.reshape(MEM_LEN, D_MODEL), mem_norm[layer].reshape(1, D_MODEL),
                     w_mem_kv[layer].astype(BF16))
    mk = mem_kv[:, :XA_W].astype(BF16)
    mv = mem_kv[:, XA_W:].astype(BF16)
    y_p = _final([a[0] for a in att] + [a[1] for a in att], [cq, mk, mv], ob_p, az, bz, cz, ga, gb, gc,
                 xp, post_g, w_pa, w_pb, w_pc, w_o, tm=256, merge=True, xattn=True)

    xs = x_sample.reshape(bd, D_MODEL)
    q_s, k_s, v_s, tails_s = _inproj_attn(xs, pre_g, w_att, invf, tm=bd, pos_base=PAST_LEN, pos_step=0,
                                          tail_rows=(bd,) * A_GROUPS, out_dtype=F32, row_dil=(1,) * A_GROUPS)
    (az_s, bq_s, lf_s, kin_s, bi_s, bz_s, cq_s, cz_s, ga_s, gb_s, gc_s) = _inproj_rest(
        xs, pre_g, w_rest, hgrn_lb_logits, tm=bd, layer=layer, hg_dtype=F32)
    oa_s = _attn_sample(q_s, k_s, v_s, caches)
    oc_s = _xattn_sample(cq_s, cache_mem_kv[layer], sb=8)
    ob_s, s_s = _hgrn_sample(bq_s, lf_s, kin_s, bi_s, state_hgrn[layer], hg_g, sb=8)
    y_s = _final([oa_s], [oc_s], ob_s, az_s, bz_s, cz_s, ga_s, gb_s, gc_s,
                 xs, post_g, w_pa, w_pb, w_pc, w_o, tm=bd, merge=False, xattn=False)

    new_win_p = [tails_p[g].reshape(1, 1, tail_rows[g], 2, A_HEADS, A_HEAD_DIM) for g in range(A_GROUPS)]
    new_win_s = [tails_s[g].reshape(1, bd, 1, 2, A_HEADS, A_HEAD_DIM) for g in range(A_GROUPS)]
    return (y_p.reshape(bp, t, D_MODEL), y_s.reshape(bd, 1, D_MODEL),
            new_win_p[0], new_win_p[1], new_win_p[2],
            s_p.reshape(1, 1, HG_HEADS, HG_K, HG_V),
            mem_kv.reshape(1, 1, MEM_LEN, 2, XA_HEADS, XA_HEAD_DIM),
            new_win_s[0], new_win_s[1], new_win_s[2],
            s_s.reshape(1, bd, HG_HEADS, HG_K, HG_V))
```

```python
import functools

import jax
import jax.numpy as jnp
from jax import lax
from jax.experimental import pallas as pl
from jax.experimental.pallas import tpu as pltpu

D_MODEL = 1024
PAST_LEN = 8192
A_WINDOWS = (128, 512, 2048)
A_DILATIONS = (1, 4, 16)
A_GROUPS = 3
A_HEADS = 8
A_HEAD_DIM = 64
A_NKEY = 128
ROPE_THETA = 10000.0
HG_HEADS = 4
HG_K = 128
HG_V = 128
HG_CHUNK = 64
HG_BLOCK = 8
MEM_LEN = 256
XA_HEADS = 4
XA_HEAD_DIM = 128
EPS = 1e-6

A_W = A_HEADS * A_HEAD_DIM
HG_W = HG_HEADS * HG_K
XA_W = XA_HEADS * XA_HEAD_DIM
N_ATT = 3 * A_GROUPS * A_W
COL_AZ = N_ATT
COL_BQ = COL_AZ + A_W
COL_BF = COL_BQ + HG_W
COL_BI = COL_BF + HG_W
COL_BZ = COL_BI + HG_W
COL_CQ = COL_BZ + HG_W
COL_CZ = COL_CQ + XA_W
COL_GA = COL_CZ + XA_W
COL_GB = COL_GA + D_MODEL
COL_GC = COL_GB + D_MODEL
N_IN = COL_GC + D_MODEL

LANES = 128
VMEM_LIMIT_BYTES = 56 * 1024 * 1024

F32 = jnp.float32
BF16 = jnp.bfloat16
NEG_INF = float("-inf")


def _params(semantics, vmem=None):
    return pltpu.CompilerParams(dimension_semantics=semantics, vmem_limit_bytes=vmem)


def _resident(shape):
    return pl.BlockSpec(shape, lambda *_: (0,) * len(shape), pipeline_mode=pl.Buffered(1))


def _rmsnorm_rows(x, gain):
    ms = jnp.mean(x * x, axis=-1, keepdims=True)
    return x * lax.rsqrt(ms + EPS) * gain


def _sigmoid(x):
    return 1.0 / (1.0 + jnp.exp(-x))


def _silu(x):
    return x * _sigmoid(x)


def _tile_row_order(tm, d):
    j = lax.broadcasted_iota(jnp.int32, (tm, LANES), 0)
    per = tm // d
    return (j % per) * d + j // per


def _inproj_attn_kernel(x_ref, g_ref, w_ref, invf_ref, *refs, tm, n_tiles, pos_base, pos_step, tails, row_dil):
    q_refs = refs[0:3]
    k_refs = refs[3:6]
    v_refs = refs[6:9]
    t_refs = refs[9:12]
    cos_tab, sin_tab, xa_ref, xb_ref, xp_ref = refs[12:17]
    i = pl.program_id(0)
    n_slab = A_W // LANES
    n_xslab = D_MODEL // LANES
    dils = sorted(set(row_dil))

    @pl.when(i == 0)
    def _():
        for t_idx, d in enumerate(dils):
            ang0 = (_tile_row_order(tm, d) * pos_step).astype(F32) * invf_ref[...]
            cos_tab[t_idx] = jnp.cos(ang0)
            sin_tab[t_idx] = jnp.sin(ang0)

    lane = lax.broadcasted_iota(jnp.int32, (tm, LANES), 1)
    first_half = (lane % A_HEAD_DIM) < (A_HEAD_DIM // 2)
    base = (pos_base + i * tm * pos_step).astype(F32) * invf_ref[...]
    cos_b = jnp.cos(base)
    sin_b = jnp.sin(base)

    def trig(d):
        t_idx = dils.index(d)
        c0, s0 = cos_tab[t_idx], sin_tab[t_idx]
        cos = c0 * cos_b - s0 * sin_b
        sin = s0 * cos_b + c0 * sin_b
        return cos, jnp.where(first_half, -sin, sin)

    def rope(acc, cos, sin_signed):
        outs = []
        for c in range(n_slab):
            xc = acc[:, c * LANES:(c + 1) * LANES]
            partner = jnp.where(first_half, pltpu.roll(xc, LANES - 32, 1), pltpu.roll(xc, 32, 1))
            outs.append(xc * cos + partner * sin_signed)
        return jnp.concatenate(outs, axis=-1)

    xn32 = _rmsnorm_rows(x_ref[...], g_ref[...])
    xn = xn32.astype(BF16)

    lhs = {1: xn}
    if max(row_dil) > 1:
        assert set(row_dil) <= {1, 4, 16}
        per4 = tm // 4
        for c in range(n_xslab):
            xa_ref[c] = xn32[:, c * LANES:(c + 1) * LANES]
        for c in range(n_xslab):
            for r in range(4):
                blk4 = xa_ref[c, pl.ds(r, per4, stride=4), :]
                xb_ref[c, r * per4:(r + 1) * per4, :] = blk4
                xp_ref[0, r * per4:(r + 1) * per4, c * LANES:(c + 1) * LANES] = blk4.astype(BF16)
        lhs[4] = xp_ref[0]
        if 16 in row_dil:
            per16 = tm // 16
            for c in range(n_xslab):
                for r1 in range(4):
                    for r2 in range(4):
                        res = 4 * r2 + r1
                        xp_ref[1, res * per16:(res + 1) * per16, c * LANES:(c + 1) * LANES] = (
                            xb_ref[c, pl.ds(r1 * per4 + r2, per16, stride=4), :].astype(BF16))
            lhs[16] = xp_ref[1]

    def store_rows(ref, d, val):
        per = tm // d
        for r in range(d):
            ref[:, r * A_W:(r + 1) * A_W] = val[r * per:(r + 1) * per, :].astype(ref.dtype)

    def tail_block(g, col0, natural_val):
        first_tile, rows = tails[g]
        cond = (i >= first_tile) if rows >= tm else (i == n_tiles - 1)

        @pl.when(cond)
        def _():
            val = natural_val()
            t_refs[g][:, col0:col0 + A_W] = val if rows >= tm else val[tm - rows:, :]

    cos_n, sin_n = trig(1)
    for g in range(A_GROUPS):
        d = row_dil[g]
        cos, sin_s = (cos_n, sin_n) if d == 1 else trig(d)
        wq = w_ref[:, g * A_W:(g + 1) * A_W]
        wk = w_ref[:, (A_GROUPS + g) * A_W:(A_GROUPS + g + 1) * A_W]
        wv = w_ref[:, (2 * A_GROUPS + g) * A_W:(2 * A_GROUPS + g + 1) * A_W]
        q = rope(jnp.dot(lhs[d], wq, preferred_element_type=F32), cos, sin_s) * (A_HEAD_DIM ** -0.5)
        store_rows(q_refs[g], d, q)
        kr = rope(jnp.dot(lhs[d], wk, preferred_element_type=F32), cos, sin_s)
        store_rows(k_refs[g], d, kr)
        v = jnp.dot(lhs[d], wv, preferred_element_type=F32)
        store_rows(v_refs[g], d, v)
        if d == 1:
            tail_block(g, 0, lambda kr=kr: kr)
            tail_block(g, A_W, lambda v=v: v)
        else:
            tail_block(g, 0, lambda wk=wk: rope(jnp.dot(xn, wk, preferred_element_type=F32), cos_n, sin_n))
            tail_block(g, A_W, lambda wv=wv: jnp.dot(xn, wv, preferred_element_type=F32))


def _inproj_attn(x, gain, w_att, invf, *, tm, pos_base, pos_step, tail_rows, out_dtype, row_dil):
    n = x.shape[0]
    n_tiles = n // tm
    assert 1 in row_dil
    n_orders = len(set(row_dil))
    tails = []
    tail_specs = []
    tail_shapes = []
    for g in range(A_GROUPS):
        rows = tail_rows[g]
        first_tile = (n - rows) // tm
        tails.append((first_tile, rows))
        blk = min(tm, rows)
        tail_specs.append(pl.BlockSpec((blk, 2 * A_W), functools.partial(
            lambda i, ft: (jnp.maximum(i - ft, 0), 0), ft=first_tile)))
        tail_shapes.append(jax.ShapeDtypeStruct((rows, 2 * A_W), F32))
    row_specs = [pl.BlockSpec((tm // d, d * A_W), lambda i: (i, 0)) for d in row_dil] * 3
    row_shapes = [jax.ShapeDtypeStruct((n // d, d * A_W), out_dtype) for d in row_dil] * 3
    kernel = functools.partial(_inproj_attn_kernel, tm=tm, n_tiles=n_tiles, pos_base=pos_base,
                               pos_step=pos_step, tails=tuple(tails), row_dil=tuple(row_dil))
    outs = pl.pallas_call(
        kernel,
        out_shape=row_shapes + tail_shapes,
        grid=(n_tiles,),
        in_specs=[pl.BlockSpec((tm, D_MODEL), lambda i: (i, 0)),
                  _resident((1, D_MODEL)),
                  _resident((D_MODEL, N_ATT)),
                  _resident((1, LANES))],
        out_specs=row_specs + tail_specs,
        scratch_shapes=[pltpu.VMEM((n_orders, tm, LANES), F32), pltpu.VMEM((n_orders, tm, LANES), F32),
                        pltpu.VMEM((D_MODEL // LANES, tm, LANES), F32), pltpu.VMEM((D_MODEL // LANES, tm, LANES), F32),
                        pltpu.VMEM((2, tm, D_MODEL), BF16)],
        compiler_params=_params(("arbitrary",), VMEM_LIMIT_BYTES),
        name="inproj_attn",
    )(x, gain, w_att, invf)
    return outs[0:3], outs[3:6], outs[6:9], outs[9:12]


def _inproj_rest_kernel(x_ref, g_ref, w_ref, lbl_ref, az_ref, bq_ref, lf_ref, kin_ref, bi_ref, bz_ref,
                        cq_ref, cz_ref, ga_ref, gb_ref, gc_ref, *, layer):
    xn = _rmsnorm_rows(x_ref[...], g_ref[...]).astype(BF16)

    def proj(col, width):
        c0 = col - N_ATT
        return jnp.dot(xn, w_ref[:, c0:c0 + width], preferred_element_type=F32)

    logits = lbl_ref[...]
    e = jnp.exp(logits - jnp.max(logits, axis=0, keepdims=True))
    lb = jnp.sum(e[0:layer + 1, :], axis=0, keepdims=True) / jnp.sum(e, axis=0, keepdims=True)

    az_ref[...] = _silu(proj(COL_AZ, A_W)).astype(az_ref.dtype)
    bq_ref[...] = proj(COL_BQ, HG_W).astype(bq_ref.dtype)
    bf = proj(COL_BF, HG_W)
    lf_ref[...] = jnp.log(lb + (1.0 - lb) * _sigmoid(bf))
    kin_ref[...] = ((1.0 - lb) * _sigmoid(-bf)).astype(kin_ref.dtype)
    bi_ref[...] = proj(COL_BI, HG_W).astype(bi_ref.dtype)
    bz_ref[...] = _silu(proj(COL_BZ, HG_W)).astype(bz_ref.dtype)
    cq_ref[...] = proj(COL_CQ, XA_W).astype(cq_ref.dtype)
    cz_ref[...] = _silu(proj(COL_CZ, XA_W)).astype(cz_ref.dtype)
    ga_ref[...] = _sigmoid(proj(COL_GA, D_MODEL)).astype(ga_ref.dtype)
    gb_ref[...] = _sigmoid(proj(COL_GB, D_MODEL)).astype(gb_ref.dtype)
    gc_ref[...] = _sigmoid(proj(COL_GC, D_MODEL)).astype(gc_ref.dtype)


def _inproj_rest(x, gain, w_rest, lb_logits, *, tm, layer, hg_dtype):
    n = x.shape[0]
    n_rest = N_IN - N_ATT
    s512 = pl.BlockSpec((tm, 512), lambda i: (i, 0))
    s1024 = pl.BlockSpec((tm, D_MODEL), lambda i: (i, 0))

    def sds(width, dt):
        return jax.ShapeDtypeStruct((n, width), dt)

    return pl.pallas_call(
        functools.partial(_inproj_rest_kernel, layer=layer),
        out_shape=[sds(A_W, BF16), sds(HG_W, hg_dtype), sds(HG_W, F32), sds(HG_W, hg_dtype), sds(HG_W, hg_dtype),
                   sds(HG_W, BF16), sds(XA_W, BF16), sds(XA_W, BF16),
                   sds(D_MODEL, BF16), sds(D_MODEL, BF16), sds(D_MODEL, BF16)],
        grid=(n // tm,),
        in_specs=[pl.BlockSpec((tm, D_MODEL), lambda i: (i, 0)),
                  _resident((1, D_MODEL)),
                  _resident((D_MODEL, n_rest)),
                  _resident(lb_logits.shape)],
        out_specs=[s512] * 8 + [s1024] * 3,
        compiler_params=_params(("parallel",), VMEM_LIMIT_BYTES),
        name="inproj_rest",
    )(x, gain, w_rest, lb_logits)


def _attn_prompt_kernel(q_ref, kc_ref, vc_ref, kp_ref, vp_ref, o_ref, st_ref, kbuf, vbuf, o_acc, st_acc,
                        *, tmv, dil, rps):
    n = pl.program_id(0)
    rstep = pl.program_id(1)
    blk = A_NKEY

    qi = lax.broadcasted_iota(jnp.int32, (blk, 2 * blk), 0)
    ki = lax.broadcasted_iota(jnp.int32, (blk, 2 * blk), 1)
    dist = qi + blk - ki
    band = (dist >= 0) & (dist <= A_NKEY)
    bias = jnp.where(band, 0.0, NEG_INF).astype(F32)
    bias_first = jnp.where(band & (ki >= blk), 0.0, NEG_INF).astype(F32)
    lane = lax.broadcasted_iota(jnp.int32, (blk, LANES), 1)
    low_head = lane < A_HEAD_DIM
    head_mask = (jnp.where(low_head, 1.0, 0.0).astype(BF16), jnp.where(low_head, 0.0, 1.0).astype(BF16))

    for rr in range(rps):
        cs = slice(rr * A_W, (rr + 1) * A_W)
        kbuf[0:blk, :] = kp_ref[:, cs]
        kbuf[blk:, :] = kc_ref[:, cs]
        vbuf[0:blk, :] = vp_ref[:, cs]
        vbuf[blk:, :] = vc_ref[:, cs]
        res = rstep * rps + rr

        def body(b, carry, cs=cs, res=res):
            r0 = pl.multiple_of(b * blk, blk)
            is_first = jnp.logical_and(n == 0, b == 0)
            bias_b = jnp.where(is_first, bias_first, bias)
            qb = q_ref[pl.ds(r0, blk), cs]
            kb = kbuf[pl.ds(r0, 2 * blk), :]
            vb = vbuf[pl.ds(r0, 2 * blk), :]
            out_rows = pl.ds(r0 * dil + res, blk, stride=dil)
            stats = jnp.zeros((blk, LANES), F32)
            for pair in range(A_HEADS // 2):
                sl = slice(pair * LANES, (pair + 1) * LANES)
                qp = qb[:, sl]
                kp = kb[:, sl]
                vp = vb[:, sl]
                outs = []
                for hh in range(2):
                    qm = qp * head_mask[hh]
                    s = pl.dot(qm, kp, trans_b=True) + bias_b
                    m = jnp.max(s, axis=-1, keepdims=True)
                    p = jnp.exp(s - m)
                    l = jnp.sum(p, axis=-1, keepdims=True)
                    pv = jnp.dot(p.astype(BF16), vp, preferred_element_type=F32)
                    outs.append(pv * (1.0 / l))
                    lse = m + jnp.log(l)
                    stats = jnp.where(lane == 2 * pair + hh, lse, stats)
                o_acc[pair, out_rows, :] = jnp.where(low_head, outs[0], outs[1])
            st_acc[out_rows, :] = stats
            return carry

        lax.fori_loop(0, tmv // blk, body, 0)

    @pl.when(rstep == pl.num_programs(1) - 1)
    def _():
        for pair in range(A_HEADS // 2):
            o_ref[:, pair * LANES:(pair + 1) * LANES] = o_acc[pair].astype(o_ref.dtype)
        st_ref[...] = st_acc[...]


def _attn_prompt(qv, kv, vv, dil, *, rows_per_step):
    l = qv.shape[0]
    t = l * dil
    blk = A_NKEY
    tmv = rows_per_step // dil
    rps = max(1, blk // tmv)
    tmv = max(tmv, blk)
    assert t % (tmv * dil) == 0 and dil % rps == 0
    cur = pl.BlockSpec((tmv, rps * A_W), lambda n, r: (n, r))
    prev = pl.BlockSpec((blk, rps * A_W), lambda n, r: (jnp.maximum(n * (tmv // blk) - 1, 0), r))
    rows = tmv * dil
    return pl.pallas_call(
        functools.partial(_attn_prompt_kernel, tmv=tmv, dil=dil, rps=rps),
        out_shape=[jax.ShapeDtypeStruct((t, A_W), BF16), jax.ShapeDtypeStruct((t, LANES), F32)],
        grid=(t // rows, dil // rps),
        in_specs=[cur, cur, cur, prev, prev],
        out_specs=[pl.BlockSpec((rows, A_W), lambda n, r: (n, 0)), pl.BlockSpec((rows, LANES), lambda n, r: (n, 0))],
        scratch_shapes=[pltpu.VMEM((tmv + blk, A_W), BF16), pltpu.VMEM((tmv + blk, A_W), BF16),
                        pltpu.VMEM((A_HEADS // 2, rows, LANES), F32), pltpu.VMEM((rows, LANES), F32)],
        compiler_params=_params(("parallel", "arbitrary"), VMEM_LIMIT_BYTES),
        name=f"attn_prompt_d{dil}",
    )(qv, kv, vv, kv, vv)


def _hgrn_prompt_kernel(q_ref, lf_ref, k_ref, v_ref, gain_ref, o_ref, s_out_ref, st_ref, *, th):
    i = pl.program_id(0)
    c_rows = HG_CHUNK
    nblk = c_rows // HG_BLOCK
    shape3 = (nblk, HG_BLOCK, HG_K)

    @pl.when(i == 0)
    def _():
        st_ref[...] = jnp.zeros_like(st_ref)

    sub = lax.broadcasted_iota(jnp.int32, shape3, 1)
    a_row = lax.broadcasted_iota(jnp.int32, (c_rows, c_rows), 0) // HG_BLOCK
    a_col = lax.broadcasted_iota(jnp.int32, (c_rows, c_rows), 1) // HG_BLOCK

    def shift(x3, s):
        return pltpu.roll(x3, s, 1)

    def chunk(c, carry):
        r0 = pl.multiple_of(c * c_rows, c_rows)
        for h in range(HG_HEADS):
            sl = slice(h * HG_K, (h + 1) * HG_K)
            q = q_ref[pl.ds(r0, c_rows), sl].astype(F32)
            k = k_ref[pl.ds(r0, c_rows), sl].astype(F32)
            v = v_ref[pl.ds(r0, c_rows), sl].astype(F32)
            q3, k3, v3 = (a.reshape(shape3) for a in (q, k, v))
            cum = lf_ref[pl.ds(r0, c_rows), sl].reshape(shape3)
            s = 1
            while s < HG_BLOCK:
                cum = cum + jnp.where(sub >= s, shift(cum, s), 0.0)
                s *= 2
            anchors = [jnp.zeros((1, 1, HG_K), F32)]
            for j in range(nblk):
                anchors.append(anchors[j] + cum[j:j + 1, HG_BLOCK - 1:HG_BLOCK, :])
            b3 = cum + jnp.concatenate(anchors[:nblk], axis=0)
            b = b3.reshape(c_rows, HG_K)
            b_last = anchors[nblk].reshape(1, HG_K)

            o3 = jnp.sum(q3 * k3, axis=-1, keepdims=True) * v3
            for d in range(1, HG_BLOCK):
                dec = jnp.exp(jnp.where(sub >= d, cum - shift(cum, d), NEG_INF))
                a = jnp.sum(q3 * shift(k3, d) * dec, axis=-1, keepdims=True)
                o3 = o3 + a * shift(v3, d)
            o = o3.reshape(c_rows, HG_K)

            b_end = jnp.concatenate(anchors[1:], axis=0)
            k_hat = (k3 * jnp.exp(b_end - b3)).reshape(c_rows, HG_K).astype(BF16)
            q_ver = [(q3 * jnp.exp(jnp.minimum(b3 - anchors[j + 1], 0.0))).reshape(c_rows, HG_K).astype(BF16)
                     for j in range(nblk - 1)]
            r = pl.dot(jnp.concatenate(q_ver, axis=0), k_hat, trans_b=True)
            att = jnp.zeros((c_rows, c_rows), F32)
            for j in range(nblk - 1):
                att = jnp.where(a_col == j, r[j * c_rows:(j + 1) * c_rows, :], att)
            att = jnp.where(a_row > a_col, att, 0.0)
            vb = v.astype(BF16)
            o = o + jnp.dot(att.astype(BF16), vb, preferred_element_type=F32)

            st = st_ref[h]
            o = o + pl.dot((q * jnp.exp(b)).astype(BF16), st.astype(BF16), trans_b=True)
            k_end = (k * jnp.exp(b_last - b)).astype(BF16)
            st_ref[h] = st * jnp.exp(b_last) + pl.dot(vb, k_end, trans_a=True)

            on = o * lax.rsqrt(jnp.mean(o * o, axis=-1, keepdims=True) + EPS) * gain_ref[:, sl]
            o_ref[pl.ds(r0, c_rows), sl] = on.astype(o_ref.dtype)
        return carry

    lax.fori_loop(0, th // c_rows, chunk, 0)

    @pl.when(i == pl.num_programs(0) - 1)
    def _():
        for h in range(HG_HEADS):
            s_out_ref[h] = st_ref[h].T


def _hgrn_prompt(q, lf, k, v, gain, *, th):
    t = q.shape[0]
    spec = pl.BlockSpec((th, HG_W), lambda i: (i, 0))
    return pl.pallas_call(
        functools.partial(_hgrn_prompt_kernel, th=th),
        out_shape=[jax.ShapeDtypeStruct((t, HG_W), BF16),
                   jax.ShapeDtypeStruct((HG_HEADS, HG_K, HG_V), F32)],
        grid=(t // th,),
        in_specs=[spec, spec, spec, spec, _resident((1, HG_W))],
        out_specs=[spec, pl.BlockSpec((HG_HEADS, HG_K, HG_V), lambda i: (0, 0, 0))],
        scratch_shapes=[pltpu.VMEM((HG_HEADS, HG_V, HG_K), F32)],
        compiler_params=_params(("arbitrary",)),
        name="hgrn_prompt",
    )(q, lf, k, v, gain)


def _mem_kv_kernel(m_ref, g_ref, w_ref, o_ref):
    xn = _rmsnorm_rows(m_ref[...], g_ref[...]).astype(BF16)
    o_ref[...] = jnp.dot(xn, w_ref[...], preferred_element_type=F32)


def _mem_kv(mem, gain, w):
    m = mem.shape[0]
    return pl.pallas_call(
        _mem_kv_kernel,
        out_shape=jax.ShapeDtypeStruct((m, 2 * XA_W), F32),
        name="mem_kv",
    )(mem, gain, w)


def _softmax_rows(s):
    m = jnp.max(s, axis=-1, keepdims=True)
    p = jnp.exp(s - m)
    return p, jnp.sum(p, axis=-1, keepdims=True)


def _final_kernel(*refs, tm, merge, xattn):
    refs = list(refs)
    if merge:
        o_refs = [refs.pop(0) for _ in range(A_GROUPS)]
        st_refs = [refs.pop(0) for _ in range(A_GROUPS)]
    else:
        oa_ref = refs.pop(0)
    if xattn:
        cq_ref, mk_ref, mv_ref = (refs.pop(0) for _ in range(3))
    else:
        oc_ref = refs.pop(0)
    (ob_ref, az_ref, bz_ref, cz_ref, ga_ref, gb_ref, gc_ref, x_ref, pg_ref,
     wa_ref, wb_ref, wc_ref, wo_ref, out_ref) = refs

    if merge:
        lse = [r[...] for r in st_refs]
        mx = jnp.maximum(jnp.maximum(lse[0], lse[1]), lse[2])
        ex = [jnp.exp(x - mx) for x in lse]
        inv = 1.0 / (ex[0] + ex[1] + ex[2])
        alpha = [e * inv for e in ex]
        low_head = lax.broadcasted_iota(jnp.int32, (tm, LANES), 1) < A_HEAD_DIM
        slabs = []
        for c in range(A_W // LANES):
            sl = slice(c * LANES, (c + 1) * LANES)
            acc = jnp.zeros((tm, LANES), F32)
            for g in range(A_GROUPS):
                w = jnp.where(low_head, alpha[g][:, 2 * c:2 * c + 1], alpha[g][:, 2 * c + 1:2 * c + 2])
                acc = acc + w * o_refs[g][:, sl].astype(F32)
            slabs.append(acc)
        oa = jnp.concatenate(slabs, axis=-1)
    else:
        oa = oa_ref[...].astype(F32)
    ya = (oa * az_ref[...].astype(F32)).astype(BF16)

    if xattn:
        cq = cq_ref[...]
        parts = []
        for h in range(XA_HEADS):
            sl = slice(h * XA_HEAD_DIM, (h + 1) * XA_HEAD_DIM)
            s = pl.dot(cq[:, sl], mk_ref[:, sl], trans_b=True) * (XA_HEAD_DIM ** -0.5)
            p, l = _softmax_rows(s)
            parts.append(jnp.dot(p.astype(BF16), mv_ref[:, sl], preferred_element_type=F32) * (1.0 / l))
        oc = jnp.concatenate(parts, axis=-1)
    else:
        oc = oc_ref[...].astype(F32)
    yc = (oc * cz_ref[...].astype(F32)).astype(BF16)

    yb = (ob_ref[...].astype(F32) * bz_ref[...].astype(F32)).astype(BF16)

    merged = (ga_ref[...].astype(F32) * jnp.dot(ya, wa_ref[...], preferred_element_type=F32)
              + gb_ref[...].astype(F32) * jnp.dot(yb, wb_ref[...], preferred_element_type=F32)
              + gc_ref[...].astype(F32) * jnp.dot(yc, wc_ref[...], preferred_element_type=F32))
    z = jnp.dot(merged.astype(BF16), wo_ref[...], preferred_element_type=F32)
    out_ref[...] = x_ref[...] + _rmsnorm_rows(z, pg_ref[...])


def _final(att_in, xa_in, ob, az, bz, cz, ga, gb, gc, x, post_g, w_pa, w_pb, w_pc, w_out, *, tm, merge, xattn):
    n = x.shape[0]

    def rows(width):
        return pl.BlockSpec((tm, width), lambda i: (i, 0))

    if merge:
        att_specs = [rows(A_W)] * A_GROUPS + [rows(LANES)] * A_GROUPS
    else:
        att_specs = [rows(A_W)]
    if xattn:
        xa_specs = [rows(XA_W), _resident((MEM_LEN, XA_W)), _resident((MEM_LEN, XA_W))]
    else:
        xa_specs = [rows(XA_W)]
    in_specs = (att_specs + xa_specs + [rows(HG_W), rows(A_W), rows(HG_W), rows(XA_W)]
                + [rows(D_MODEL)] * 4 + [_resident((1, D_MODEL))]
                + [_resident((A_W, D_MODEL)), _resident((HG_W, D_MODEL)), _resident((XA_W, D_MODEL)),
                   _resident((D_MODEL, D_MODEL))])
    return pl.pallas_call(
        functools.partial(_final_kernel, tm=tm, merge=merge, xattn=xattn),
        out_shape=jax.ShapeDtypeStruct((n, D_MODEL), F32),
        grid=(n // tm,),
        in_specs=in_specs,
        out_specs=rows(D_MODEL),
        compiler_params=_params(("parallel",), VMEM_LIMIT_BYTES),
        name="final_merge" if merge else "final",
    )(*att_in, *xa_in, ob, az, bz, cz, ga, gb, gc, x, post_g, w_pa, w_pb, w_pc, w_out)


def _head_rows(q_row, head_dim):
    w = q_row.shape[-1]
    sub = lax.broadcasted_iota(jnp.int32, (8, w), 0)
    lane_head = lax.broadcasted_iota(jnp.int32, (8, w), 1) // head_dim
    keep = sub == lane_head
    return jnp.where(keep, jnp.broadcast_to(q_row, (8, w)), 0.0), keep


def _attn_sample_kernel(q0, q1, q2, k0, k1, k2, v0, v1, v2, c0, c1, c2, o_ref):
    q_refs = (q0, q1, q2)
    k_refs = (k0, k1, k2)
    v_refs = (v0, v1, v2)
    c_refs = (c0, c1, c2)
    b = pl.program_id(0)
    outs = []
    lses = []
    keep = None
    for g in range(A_GROUPS):
        dil = A_DILATIONS[g]
        wb = c_refs[g].shape[-1]
        q_row = q_refs[g][pl.ds(b, 1), :]
        k_new = k_refs[g][pl.ds(b, 1), :]
        v_new = v_refs[g][pl.ds(b, 1), :]
        qm, keep = _head_rows(q_row, A_HEAD_DIM)
        kt = c_refs[g][0].astype(BF16)
        vt = c_refs[g][1].astype(BF16)
        s = jnp.dot(qm.astype(BF16), kt, preferred_element_type=F32)
        w_pos = lax.broadcasted_iota(jnp.int32, (8, wb), 1)
        s = jnp.where(w_pos % dil == 0, s, NEG_INF)
        s_new = jnp.sum(qm * k_new, axis=-1, keepdims=True)
        m = jnp.maximum(jnp.max(s, axis=-1, keepdims=True), s_new)
        p = jnp.exp(s - m)
        p_new = jnp.exp(s_new - m)
        l = jnp.sum(p, axis=-1, keepdims=True) + p_new
        o = (pl.dot(p.astype(BF16), vt, trans_b=True) + p_new * v_new) * (1.0 / l)
        outs.append(o)
        lses.append(m + jnp.log(l))
    mx = jnp.maximum(jnp.maximum(lses[0], lses[1]), lses[2])
    ex = [jnp.exp(x - mx) for x in lses]
    inv = 1.0 / (ex[0] + ex[1] + ex[2])
    oa = (ex[0] * outs[0] + ex[1] * outs[1] + ex[2] * outs[2]) * inv
    o_ref[pl.ds(b, 1), :] = jnp.sum(jnp.where(keep, oa, 0.0), axis=0, keepdims=True)


def _attn_sample(qs, ks, vs, caches):
    bd = qs[0].shape[0]
    views = []
    c_specs = []
    for g in range(A_GROUPS):
        c = caches[g]
        wb = c.shape[1]
        assert wb == A_NKEY * A_DILATIONS[g], "window buffer must hold exactly 128 dilated keys"
        views.append(jnp.transpose(c, (0, 2, 3, 4, 1)).reshape(bd, 2, A_W, wb))
        c_specs.append(pl.BlockSpec((None, 2, A_W, wb), lambda b: (b, 0, 0, 0)))
    full = pl.BlockSpec((bd, A_W), lambda b: (0, 0))
    return pl.pallas_call(
        _attn_sample_kernel,
        out_shape=jax.ShapeDtypeStruct((bd, A_W), F32),
        grid=(bd,),
        in_specs=[full] * 9 + c_specs,
        out_specs=full,
        compiler_params=_params(("arbitrary",), VMEM_LIMIT_BYTES),
        name="attn_sample",
    )(*qs, *ks, *vs, *views)


def _xattn_sample_kernel(q_ref, c_ref, o_ref, *, sb):
    stride = 2 * XA_HEADS
    for b in range(sb):
        for h in range(XA_HEADS):
            sl = slice(h * XA_HEAD_DIM, (h + 1) * XA_HEAD_DIM)
            kh = c_ref[b, pl.ds(h, MEM_LEN, stride=stride), :]
            vh = c_ref[b, pl.ds(XA_HEADS + h, MEM_LEN, stride=stride), :]
            q_h = q_ref[b:b + 1, sl].astype(F32)
            s = jnp.sum(kh * q_h, axis=-1, keepdims=True) * (XA_HEAD_DIM ** -0.5)
            m = jnp.max(s, axis=0, keepdims=True)
            p = jnp.exp(s - m)
            l = jnp.sum(p, axis=0, keepdims=True)
            o_ref[b:b + 1, sl] = jnp.sum(p * vh, axis=0, keepdims=True) * (1.0 / l)


def _xattn_sample(cq, mem_cache, *, sb):
    bd = cq.shape[0]
    rows = MEM_LEN * 2 * XA_HEADS
    view = mem_cache.reshape(bd, rows, XA_HEAD_DIM)
    return pl.pallas_call(
        functools.partial(_xattn_sample_kernel, sb=sb),
        out_shape=jax.ShapeDtypeStruct((bd, XA_W), F32),
        grid=(bd // sb,),
        in_specs=[pl.BlockSpec((sb, XA_W), lambda i: (i, 0)),
                  pl.BlockSpec((sb, rows, XA_HEAD_DIM), lambda i: (i, 0, 0))],
        out_specs=pl.BlockSpec((sb, XA_W), lambda i: (i, 0)),
        compiler_params=_params(("parallel",), VMEM_LIMIT_BYTES),
        name="xattn_sample",
    )(cq, view)


def _hgrn_sample_kernel(qt_ref, lft_ref, kt_ref, v_ref, s0_ref, gain_ref, o_ref, s_ref, *, sb):
    for b in range(sb):
        for h in range(HG_HEADS):
            sl = slice(h * HG_K, (h + 1) * HG_K)
            d_col = jnp.exp(lft_ref[0, sl, b:b + 1])
            k_col = kt_ref[0, sl, b:b + 1]
            q_col = qt_ref[0, sl, b:b + 1]
            v_row = v_ref[b:b + 1, sl]
            s_new = d_col * s0_ref[b, h] + k_col * v_row
            s_ref[b, h] = s_new
            o = jnp.sum(q_col * s_new, axis=0, keepdims=True)
            on = o * lax.rsqrt(jnp.mean(o * o, axis=-1, keepdims=True) + EPS) * gain_ref[:, sl]
            o_ref[b:b + 1, sl] = on


def _hgrn_sample(q, lf, kin, v, s0, gain, *, sb):
    bd = q.shape[0]
    steps = bd // sb

    def cols(a):
        return a.reshape(steps, sb, HG_W).transpose(0, 2, 1)

    col_spec = pl.BlockSpec((1, HG_W, sb), lambda i: (i, 0, 0))
    st_spec = pl.BlockSpec((sb, HG_HEADS, HG_K, HG_V), lambda i: (i, 0, 0, 0))
    row = pl.BlockSpec((sb, HG_W), lambda i: (i, 0))
    return pl.pallas_call(
        functools.partial(_hgrn_sample_kernel, sb=sb),
        out_shape=[jax.ShapeDtypeStruct((bd, HG_W), F32), jax.ShapeDtypeStruct(s0.shape, F32)],
        grid=(steps,),
        in_specs=[col_spec, col_spec, col_spec, row, st_spec, _resident((1, HG_W))],
        out_specs=[row, st_spec],
        compiler_params=_params(("parallel",)),
        name="hgrn_sample",
    )(cols(q), cols(lf), cols(kin), v, s0, gain)


def kernel(x_prompt, x_sample, mem_prompt, cache_win128_kv, cache_win512_kv, cache_win2048_kv, state_hgrn, cache_mem_kv, norm_pre, norm_post, w_in, hgrn_lb_logits, hgrn_out_norm, mem_norm, w_mem_kv, w_branch_a, w_branch_b, w_branch_c, w_out):
    depth = w_in.shape[0]
    assert depth == 1, "single-layer trunk"
    layer = 0
    bp, t, _ = x_prompt.shape
    bd, s_len, _ = x_sample.shape
    assert bp == 1 and s_len == 1
    caches = (cache_win128_kv[layer], cache_win512_kv[layer], cache_win2048_kv[layer])

    w_att = w_in[layer][:, :N_ATT].astype(BF16)
    w_rest = w_in[layer][:, N_ATT:].astype(BF16)
    pre_g = norm_pre[layer].reshape(1, D_MODEL)
    post_g = norm_post[layer].reshape(1, D_MODEL)
    hg_g = hgrn_out_norm[layer].reshape(1, HG_W)
    w_pa, w_pb, w_pc, w_o = (w[layer].astype(BF16) for w in (w_branch_a, w_branch_b, w_branch_c, w_out))
    half = A_HEAD_DIM // 2
    inv = ROPE_THETA ** (-jnp.arange(0, A_HEAD_DIM, 2, dtype=F32) / A_HEAD_DIM)
    invf = jnp.tile(inv, LANES // half).reshape(1, LANES)

    xp = x_prompt.reshape(t, D_MODEL)
    tail_rows = tuple(min(w, t) for w in A_WINDOWS)
    q_p, k_p, v_p, tails_p = _inproj_attn(xp, pre_g, w_att, invf, tm=512, pos_base=0, pos_step=1,
                                          tail_rows=tail_rows, out_dtype=BF16, row_dil=A_DILATIONS)
    (az, bq, lf, kin, bi, bz, cq, cz, ga, gb, gc) = _inproj_rest(
        xp, pre_g, w_rest, hgrn_lb_logits, tm=512, layer=layer, hg_dtype=BF16)
    att = [_attn_prompt(q_p[g], k_p[g], v_p[g], A_DILATIONS[g], rows_per_step=2048) for g in range(A_GROUPS)]
    ob_p, s_p = _hgrn_prompt(bq, lf, kin, bi, hg_g, th=512)
    mem_kv = _mem_kv(mem_prompt.reshape(MEM_LEN, D_MODEL), mem_norm[layer].reshape(1, D_MODEL),
                     w_mem_kv[layer].astype(BF16))
    mk = mem_kv[:, :XA_W].astype(BF16)
    mv = mem_kv[:, XA_W:].astype(BF16)
    y_p = _final([a[0] for a in att] + [a[1] for a in att], [cq, mk, mv], ob_p, az, bz, cz, ga, gb, gc,
                 xp, post_g, w_pa, w_pb, w_pc, w_o, tm=512, merge=True, xattn=True)

    xs = x_sample.reshape(bd, D_MODEL)
    q_s, k_s, v_s, tails_s = _inproj_attn(xs, pre_g, w_att, invf, tm=bd, pos_base=PAST_LEN, pos_step=0,
                                          tail_rows=(bd,) * A_GROUPS, out_dtype=F32, row_dil=(1,) * A_GROUPS)
    (az_s, bq_s, lf_s, kin_s, bi_s, bz_s, cq_s, cz_s, ga_s, gb_s, gc_s) = _inproj_rest(
        xs, pre_g, w_rest, hgrn_lb_logits, tm=bd, layer=layer, hg_dtype=F32)
    oa_s = _attn_sample(q_s, k_s, v_s, caches)
    oc_s = _xattn_sample(cq_s, cache_mem_kv[layer], sb=8)
    ob_s, s_s = _hgrn_sample(bq_s, lf_s, kin_s, bi_s, state_hgrn[layer], hg_g, sb=8)
    y_s = _final([oa_s], [oc_s], ob_s, az_s, bz_s, cz_s, ga_s, gb_s, gc_s,
                 xs, post_g, w_pa, w_pb, w_pc, w_o, tm=bd, merge=False, xattn=False)

    new_win_p = [tails_p[g].reshape(1, 1, tail_rows[g], 2, A_HEADS, A_HEAD_DIM) for g in range(A_GROUPS)]
    new_win_s = [tails_s[g].reshape(1, bd, 1, 2, A_HEADS, A_HEAD_DIM) for g in range(A_GROUPS)]
    return (y_p.reshape(bp, t, D_MODEL), y_s.reshape(bd, 1, D_MODEL),
            new_win_p[0], new_win_p[1], new_win_p[2],
            s_p.reshape(1, 1, HG_HEADS, HG_K, HG_V),
            mem_kv.reshape(1, 1, MEM_LEN, 2, XA_HEADS, XA_HEAD_DIM),
            new_win_s[0], new_win_s[1], new_win_s[2],
            s_s.reshape(1, bd, HG_HEADS, HG_K, HG_V))
```

```python
import functools

import jax
import jax.numpy as jnp
from jax import lax
from jax.experimental import pallas as pl
from jax.experimental.pallas import tpu as pltpu

D_MODEL = 1024
PAST_LEN = 8192
A_WINDOWS = (128, 512, 2048)
A_DILATIONS = (1, 4, 16)
A_GROUPS = 3
A_HEADS = 8
A_HEAD_DIM = 64
A_NKEY = 128
ROPE_THETA = 10000.0
HG_HEADS = 4
HG_K = 128
HG_V = 128
HG_CHUNK = 64
HG_BLOCK = 8
ATTN_BLOCKS_IN_FLIGHT = 4
MEM_LEN = 256
XA_HEADS = 4
XA_HEAD_DIM = 128
EPS = 1e-6

A_W = A_HEADS * A_HEAD_DIM
HG_W = HG_HEADS * HG_K
XA_W = XA_HEADS * XA_HEAD_DIM
N_ATT = 3 * A_GROUPS * A_W
COL_AZ = N_ATT
COL_BQ = COL_AZ + A_W
COL_BF = COL_BQ + HG_W
COL_BI = COL_BF + HG_W
COL_BZ = COL_BI + HG_W
COL_CQ = COL_BZ + HG_W
COL_CZ = COL_CQ + XA_W
COL_GA = COL_CZ + XA_W
COL_GB = COL_GA + D_MODEL
COL_GC = COL_GB + D_MODEL
N_IN = COL_GC + D_MODEL

LANES = 128
VMEM_LIMIT_BYTES = 56 * 1024 * 1024

F32 = jnp.float32
BF16 = jnp.bfloat16
NEG_INF = float("-inf")


def _params(semantics, vmem=None):
    return pltpu.CompilerParams(dimension_semantics=semantics, vmem_limit_bytes=vmem)


def _resident(shape):
    return pl.BlockSpec(shape, lambda *_: (0,) * len(shape), pipeline_mode=pl.Buffered(1))


def _rmsnorm_rows(x, gain):
    ms = jnp.mean(x * x, axis=-1, keepdims=True)
    return x * lax.rsqrt(ms + EPS) * gain


def _sigmoid(x):
    return 1.0 / (1.0 + jnp.exp(-x))


def _silu(x):
    return x * _sigmoid(x)


def _tile_row_order(tm, d):
    j = lax.broadcasted_iota(jnp.int32, (tm, LANES), 0)
    per = tm // d
    return (j % per) * d + j // per


def _inproj_attn_kernel(x_ref, g_ref, w_ref, invf_ref, *refs, tm, n_tiles, pos_base, pos_step, tails, row_dil):
    q_refs = refs[0:3]
    k_refs = refs[3:6]
    v_refs = refs[6:9]
    t_refs = refs[9:12]
    cos_tab, sin_tab, xa_ref, xb_ref, xp_ref = refs[12:17]
    i = pl.program_id(0)
    n_slab = A_W // LANES
    n_xslab = D_MODEL // LANES
    dils = sorted(set(row_dil))

    @pl.when(i == 0)
    def _():
        for t_idx, d in enumerate(dils):
            ang0 = (_tile_row_order(tm, d) * pos_step).astype(F32) * invf_ref[...]
            cos_tab[t_idx] = jnp.cos(ang0)
            sin_tab[t_idx] = jnp.sin(ang0)

    lane = lax.broadcasted_iota(jnp.int32, (tm, LANES), 1)
    first_half = (lane % A_HEAD_DIM) < (A_HEAD_DIM // 2)
    base = (pos_base + i * tm * pos_step).astype(F32) * invf_ref[...]
    cos_b = jnp.cos(base)
    sin_b = jnp.sin(base)

    def trig(d):
        t_idx = dils.index(d)
        c0, s0 = cos_tab[t_idx], sin_tab[t_idx]
        cos = c0 * cos_b - s0 * sin_b
        sin = s0 * cos_b + c0 * sin_b
        return cos, jnp.where(first_half, -sin, sin)

    def rope(acc, cos, sin_signed):
        outs = []
        for c in range(n_slab):
            xc = acc[:, c * LANES:(c + 1) * LANES]
            partner = jnp.where(first_half, pltpu.roll(xc, LANES - 32, 1), pltpu.roll(xc, 32, 1))
            outs.append(xc * cos + partner * sin_signed)
        return jnp.concatenate(outs, axis=-1)

    xn32 = _rmsnorm_rows(x_ref[...], g_ref[...])
    xn = xn32.astype(BF16)

    lhs = {1: xn}
    if max(row_dil) > 1:
        assert set(row_dil) <= {1, 4, 16}
        per4 = tm // 4
        for c in range(n_xslab):
            xa_ref[c] = xn32[:, c * LANES:(c + 1) * LANES]
        for c in range(n_xslab):
            for r in range(4):
                blk4 = xa_ref[c, pl.ds(r, per4, stride=4), :]
                xb_ref[c, r * per4:(r + 1) * per4, :] = blk4
                xp_ref[0, r * per4:(r + 1) * per4, c * LANES:(c + 1) * LANES] = blk4.astype(BF16)
        lhs[4] = xp_ref[0]
        if 16 in row_dil:
            per16 = tm // 16
            for c in range(n_xslab):
                for r1 in range(4):
                    for r2 in range(4):
                        res = 4 * r2 + r1
                        xp_ref[1, res * per16:(res + 1) * per16, c * LANES:(c + 1) * LANES] = (
                            xb_ref[c, pl.ds(r1 * per4 + r2, per16, stride=4), :].astype(BF16))
            lhs[16] = xp_ref[1]

    def store_rows(ref, d, val):
        per = tm // d
        for r in range(d):
            ref[:, r * A_W:(r + 1) * A_W] = val[r * per:(r + 1) * per, :].astype(ref.dtype)

    def tail_block(g, col0, natural_val):
        first_tile, rows = tails[g]
        cond = (i >= first_tile) if rows >= tm else (i == n_tiles - 1)

        @pl.when(cond)
        def _():
            val = natural_val()
            t_refs[g][:, col0:col0 + A_W] = val if rows >= tm else val[tm - rows:, :]

    cos_n, sin_n = trig(1)
    for g in range(A_GROUPS):
        d = row_dil[g]
        cos, sin_s = (cos_n, sin_n) if d == 1 else trig(d)
        wq = w_ref[:, g * A_W:(g + 1) * A_W]
        wk = w_ref[:, (A_GROUPS + g) * A_W:(A_GROUPS + g + 1) * A_W]
        wv = w_ref[:, (2 * A_GROUPS + g) * A_W:(2 * A_GROUPS + g + 1) * A_W]
        q = rope(jnp.dot(lhs[d], wq, preferred_element_type=F32), cos, sin_s) * (A_HEAD_DIM ** -0.5)
        store_rows(q_refs[g], d, q)
        kr = rope(jnp.dot(lhs[d], wk, preferred_element_type=F32), cos, sin_s)
        store_rows(k_refs[g], d, kr)
        v = jnp.dot(lhs[d], wv, preferred_element_type=F32)
        store_rows(v_refs[g], d, v)
        if d == 1:
            tail_block(g, 0, lambda kr=kr: kr)
            tail_block(g, A_W, lambda v=v: v)
        else:
            tail_block(g, 0, lambda wk=wk: rope(jnp.dot(xn, wk, preferred_element_type=F32), cos_n, sin_n))
            tail_block(g, A_W, lambda wv=wv: jnp.dot(xn, wv, preferred_element_type=F32))


def _inproj_attn(x, gain, w_att, invf, *, tm, pos_base, pos_step, tail_rows, out_dtype, row_dil):
    n = x.shape[0]
    n_tiles = n // tm
    assert 1 in row_dil
    n_orders = len(set(row_dil))
    tails = []
    tail_specs = []
    tail_shapes = []
    for g in range(A_GROUPS):
        rows = tail_rows[g]
        first_tile = (n - rows) // tm
        tails.append((first_tile, rows))
        blk = min(tm, rows)
        tail_specs.append(pl.BlockSpec((blk, 2 * A_W), functools.partial(
            lambda i, ft: (jnp.maximum(i - ft, 0), 0), ft=first_tile)))
        tail_shapes.append(jax.ShapeDtypeStruct((rows, 2 * A_W), F32))
    row_specs = [pl.BlockSpec((tm // d, d * A_W), lambda i: (i, 0)) for d in row_dil] * 3
    row_shapes = [jax.ShapeDtypeStruct((n // d, d * A_W), out_dtype) for d in row_dil] * 3
    kernel = functools.partial(_inproj_attn_kernel, tm=tm, n_tiles=n_tiles, pos_base=pos_base,
                               pos_step=pos_step, tails=tuple(tails), row_dil=tuple(row_dil))
    outs = pl.pallas_call(
        kernel,
        out_shape=row_shapes + tail_shapes,
        grid=(n_tiles,),
        in_specs=[pl.BlockSpec((tm, D_MODEL), lambda i: (i, 0)),
                  _resident((1, D_MODEL)),
                  _resident((D_MODEL, N_ATT)),
                  _resident((1, LANES))],
        out_specs=row_specs + tail_specs,
        scratch_shapes=[pltpu.VMEM((n_orders, tm, LANES), F32), pltpu.VMEM((n_orders, tm, LANES), F32),
                        pltpu.VMEM((D_MODEL // LANES, tm, LANES), F32), pltpu.VMEM((D_MODEL // LANES, tm, LANES), F32),
                        pltpu.VMEM((2, tm, D_MODEL), BF16)],
        compiler_params=_params(("arbitrary",), VMEM_LIMIT_BYTES),
        name="inproj_attn",
    )(x, gain, w_att, invf)
    return outs[0:3], outs[3:6], outs[6:9], outs[9:12]


def _inproj_rest_kernel(x_ref, g_ref, w_ref, lbl_ref, az_ref, bq_ref, lf_ref, kin_ref, bi_ref, bz_ref,
                        cq_ref, cz_ref, ga_ref, gb_ref, gc_ref, *, layer):
    xn = _rmsnorm_rows(x_ref[...], g_ref[...]).astype(BF16)

    def proj(col, width):
        c0 = col - N_ATT
        return jnp.dot(xn, w_ref[:, c0:c0 + width], preferred_element_type=F32)

    logits = lbl_ref[...]
    e = jnp.exp(logits - jnp.max(logits, axis=0, keepdims=True))
    lb = jnp.sum(e[0:layer + 1, :], axis=0, keepdims=True) / jnp.sum(e, axis=0, keepdims=True)

    az_ref[...] = _silu(proj(COL_AZ, A_W)).astype(az_ref.dtype)
    bq_ref[...] = proj(COL_BQ, HG_W).astype(bq_ref.dtype)
    bf = proj(COL_BF, HG_W)
    lf_ref[...] = jnp.log(lb + (1.0 - lb) * _sigmoid(bf))
    kin_ref[...] = ((1.0 - lb) * _sigmoid(-bf)).astype(kin_ref.dtype)
    bi_ref[...] = proj(COL_BI, HG_W).astype(bi_ref.dtype)
    bz_ref[...] = _silu(proj(COL_BZ, HG_W)).astype(bz_ref.dtype)
    cq_ref[...] = proj(COL_CQ, XA_W).astype(cq_ref.dtype)
    cz_ref[...] = _silu(proj(COL_CZ, XA_W)).astype(cz_ref.dtype)
    ga_ref[...] = _sigmoid(proj(COL_GA, D_MODEL)).astype(ga_ref.dtype)
    gb_ref[...] = _sigmoid(proj(COL_GB, D_MODEL)).astype(gb_ref.dtype)
    gc_ref[...] = _sigmoid(proj(COL_GC, D_MODEL)).astype(gc_ref.dtype)


def _inproj_rest(x, gain, w_rest, lb_logits, *, tm, layer, hg_dtype):
    n = x.shape[0]
    n_rest = N_IN - N_ATT
    s512 = pl.BlockSpec((tm, 512), lambda i: (i, 0))
    s1024 = pl.BlockSpec((tm, D_MODEL), lambda i: (i, 0))

    def sds(width, dt):
        return jax.ShapeDtypeStruct((n, width), dt)

    return pl.pallas_call(
        functools.partial(_inproj_rest_kernel, layer=layer),
        out_shape=[sds(A_W, BF16), sds(HG_W, hg_dtype), sds(HG_W, F32), sds(HG_W, hg_dtype), sds(HG_W, hg_dtype),
                   sds(HG_W, BF16), sds(XA_W, BF16), sds(XA_W, BF16),
                   sds(D_MODEL, BF16), sds(D_MODEL, BF16), sds(D_MODEL, BF16)],
        grid=(n // tm,),
        in_specs=[pl.BlockSpec((tm, D_MODEL), lambda i: (i, 0)),
                  _resident((1, D_MODEL)),
                  _resident((D_MODEL, n_rest)),
                  _resident(lb_logits.shape)],
        out_specs=[s512] * 8 + [s1024] * 3,
        compiler_params=_params(("parallel",), VMEM_LIMIT_BYTES),
        name="inproj_rest",
    )(x, gain, w_rest, lb_logits)


def _attn_prompt_kernel(q_ref, kc_ref, vc_ref, kp_ref, vp_ref, o_ref, st_ref, kbuf, vbuf, o_acc, st_acc,
                        *, tmv, dil, rps):
    n = pl.program_id(0)
    rstep = pl.program_id(1)
    blk = A_NKEY

    qi = lax.broadcasted_iota(jnp.int32, (blk, 2 * blk), 0)
    ki = lax.broadcasted_iota(jnp.int32, (blk, 2 * blk), 1)
    dist = qi + blk - ki
    band = (dist >= 0) & (dist <= A_NKEY)
    bias = jnp.where(band, 0.0, NEG_INF).astype(F32)
    bias_first = jnp.where(band & (ki >= blk), 0.0, NEG_INF).astype(F32)
    lane = lax.broadcasted_iota(jnp.int32, (blk, LANES), 1)
    low_head = lane < A_HEAD_DIM
    head_mask = (jnp.where(low_head, 1.0, 0.0).astype(BF16), jnp.where(low_head, 0.0, 1.0).astype(BF16))

    for rr in range(rps):
        cs = slice(rr * A_W, (rr + 1) * A_W)
        kbuf[rr, 0:blk, :] = kp_ref[:, cs]
        kbuf[rr, blk:, :] = kc_ref[:, cs]
        vbuf[rr, 0:blk, :] = vp_ref[:, cs]
        vbuf[rr, blk:, :] = vc_ref[:, cs]
        res = rstep * rps + rr

        def body(b, carry, cs=cs, res=res, rr=rr):
            r0 = pl.multiple_of(b * blk, blk)
            is_first = jnp.logical_and(n == 0, b == 0)
            bias_b = jnp.where(is_first, bias_first, bias)
            qb = q_ref[pl.ds(r0, blk), cs]
            kb = kbuf[rr, pl.ds(r0, 2 * blk), :]
            vb = vbuf[rr, pl.ds(r0, 2 * blk), :]
            out_rows = pl.ds(r0 * dil + res, blk, stride=dil)
            stats = jnp.zeros((blk, LANES), F32)
            for pair in range(A_HEADS // 2):
                sl = slice(pair * LANES, (pair + 1) * LANES)
                qp = qb[:, sl]
                kp = kb[:, sl]
                vp = vb[:, sl]
                outs = []
                for hh in range(2):
                    qm = qp * head_mask[hh]
                    s = pl.dot(qm, kp, trans_b=True) + bias_b
                    m = jnp.max(s, axis=-1, keepdims=True)
                    p = jnp.exp(s - m)
                    l = jnp.sum(p, axis=-1, keepdims=True)
                    pv = jnp.dot(p.astype(BF16), vp, preferred_element_type=F32)
                    outs.append(pv * (1.0 / l))
                    lse = m + jnp.log(l)
                    stats = jnp.where(lane == 2 * pair + hh, lse, stats)
                o_acc[pair, out_rows, :] = jnp.where(low_head, outs[0], outs[1])
            st_acc[out_rows, :] = stats
            return carry

        lax.fori_loop(0, tmv // blk, body, 0, unroll=min(ATTN_BLOCKS_IN_FLIGHT, tmv // blk))

    @pl.when(rstep == pl.num_programs(1) - 1)
    def _():
        for pair in range(A_HEADS // 2):
            o_ref[:, pair * LANES:(pair + 1) * LANES] = o_acc[pair].astype(o_ref.dtype)
        st_ref[...] = st_acc[...]


def _attn_prompt(qv, kv, vv, dil, *, rows_per_step):
    l = qv.shape[0]
    t = l * dil
    blk = A_NKEY
    tmv = max(rows_per_step // dil, blk)
    rps = min(dil, max(1, ATTN_BLOCKS_IN_FLIGHT * blk // tmv))
    assert t % (tmv * dil) == 0 and dil % rps == 0
    cur = pl.BlockSpec((tmv, rps * A_W), lambda n, r: (n, r))
    prev = pl.BlockSpec((blk, rps * A_W), lambda n, r: (jnp.maximum(n * (tmv // blk) - 1, 0), r))
    rows = tmv * dil
    return pl.pallas_call(
        functools.partial(_attn_prompt_kernel, tmv=tmv, dil=dil, rps=rps),
        out_shape=[jax.ShapeDtypeStruct((t, A_W), BF16), jax.ShapeDtypeStruct((t, LANES), F32)],
        grid=(t // rows, dil // rps),
        in_specs=[cur, cur, cur, prev, prev],
        out_specs=[pl.BlockSpec((rows, A_W), lambda n, r: (n, 0)), pl.BlockSpec((rows, LANES), lambda n, r: (n, 0))],
        scratch_shapes=[pltpu.VMEM((rps, tmv + blk, A_W), BF16), pltpu.VMEM((rps, tmv + blk, A_W), BF16),
                        pltpu.VMEM((A_HEADS // 2, rows, LANES), F32), pltpu.VMEM((rows, LANES), F32)],
        compiler_params=_params(("parallel", "arbitrary"), VMEM_LIMIT_BYTES),
        name=f"attn_prompt_d{dil}",
    )(qv, kv, vv, kv, vv)


def _hgrn_prompt_kernel(q_ref, lf_ref, k_ref, v_ref, gain_ref, o_ref, s_out_ref, st_ref, *, th):
    i = pl.program_id(0)
    c_rows = HG_CHUNK
    nblk = c_rows // HG_BLOCK
    shape3 = (nblk, HG_BLOCK, HG_K)

    @pl.when(i == 0)
    def _():
        st_ref[...] = jnp.zeros_like(st_ref)

    sub = lax.broadcasted_iota(jnp.int32, shape3, 1)
    a_row = lax.broadcasted_iota(jnp.int32, (c_rows, c_rows), 0) // HG_BLOCK
    a_col = lax.broadcasted_iota(jnp.int32, (c_rows, c_rows), 1) // HG_BLOCK

    def shift(x3, s):
        return pltpu.roll(x3, s, 1)

    def chunk(c, carry):
        r0 = pl.multiple_of(c * c_rows, c_rows)
        for h in range(HG_HEADS):
            sl = slice(h * HG_K, (h + 1) * HG_K)
            q = q_ref[pl.ds(r0, c_rows), sl].astype(F32)
            k = k_ref[pl.ds(r0, c_rows), sl].astype(F32)
            v = v_ref[pl.ds(r0, c_rows), sl].astype(F32)
            q3, k3, v3 = (a.reshape(shape3) for a in (q, k, v))
            cum = lf_ref[pl.ds(r0, c_rows), sl].reshape(shape3)
            s = 1
            while s < HG_BLOCK:
                cum = cum + jnp.where(sub >= s, shift(cum, s), 0.0)
                s *= 2
            anchors = [jnp.zeros((1, 1, HG_K), F32)]
            for j in range(nblk):
                anchors.append(anchors[j] + cum[j:j + 1, HG_BLOCK - 1:HG_BLOCK, :])
            b3 = cum + jnp.concatenate(anchors[:nblk], axis=0)
            b = b3.reshape(c_rows, HG_K)
            b_last = anchors[nblk].reshape(1, HG_K)

            o3 = jnp.sum(q3 * k3, axis=-1, keepdims=True) * v3
            for d in range(1, HG_BLOCK):
                dec = jnp.exp(jnp.where(sub >= d, cum - shift(cum, d), NEG_INF))
                a = jnp.sum(q3 * shift(k3, d) * dec, axis=-1, keepdims=True)
                o3 = o3 + a * shift(v3, d)
            o = o3.reshape(c_rows, HG_K)

            b_end = jnp.concatenate(anchors[1:], axis=0)
            k_hat = (k3 * jnp.exp(b_end - b3)).reshape(c_rows, HG_K).astype(BF16)
            q_ver = [(q3 * jnp.exp(jnp.minimum(b3 - anchors[j + 1], 0.0))).reshape(c_rows, HG_K).astype(BF16)
                     for j in range(nblk - 1)]
            r = pl.dot(jnp.concatenate(q_ver, axis=0), k_hat, trans_b=True)
            att = jnp.zeros((c_rows, c_rows), F32)
            for j in range(nblk - 1):
                att = jnp.where(a_col == j, r[j * c_rows:(j + 1) * c_rows, :], att)
            att = jnp.where(a_row > a_col, att, 0.0)
            vb = v.astype(BF16)
            o = o + jnp.dot(att.astype(BF16), vb, preferred_element_type=F32)

            st = st_ref[h]
            o = o + pl.dot((q * jnp.exp(b)).astype(BF16), st.astype(BF16), trans_b=True)
            k_end = (k * jnp.exp(b_last - b)).astype(BF16)
            st_ref[h] = st * jnp.exp(b_last) + pl.dot(vb, k_end, trans_a=True)

            on = o * lax.rsqrt(jnp.mean(o * o, axis=-1, keepdims=True) + EPS) * gain_ref[:, sl]
            o_ref[pl.ds(r0, c_rows), sl] = on.astype(o_ref.dtype)
        return carry

    n_chunks = th // c_rows
    lax.fori_loop(0, n_chunks, chunk, 0, unroll=n_chunks <= 2)

    @pl.when(i == pl.num_programs(0) - 1)
    def _():
        for h in range(HG_HEADS):
            s_out_ref[h] = st_ref[h].T


def _mem_kv_kernel(m_ref, g_ref, w_ref, o_ref):
    xn = _rmsnorm_rows(m_ref[...], g_ref[...]).astype(BF16)
    o_ref[...] = jnp.dot(xn, w_ref[...], preferred_element_type=F32)


def _mem_kv(mem, gain, w):
    m = mem.shape[0]
    return pl.pallas_call(
        _mem_kv_kernel,
        out_shape=jax.ShapeDtypeStruct((m, 2 * XA_W), F32),
        name="mem_kv",
    )(mem, gain, w)


def _softmax_rows(s):
    m = jnp.max(s, axis=-1, keepdims=True)
    p = jnp.exp(s - m)
    return p, jnp.sum(p, axis=-1, keepdims=True)


def _final_kernel(*refs, tm, merge, xattn):
    refs = list(refs)
    if merge:
        o_refs = [refs.pop(0) for _ in range(A_GROUPS)]
        st_refs = [refs.pop(0) for _ in range(A_GROUPS)]
    else:
        oa_ref = refs.pop(0)
    if xattn:
        cq_ref, mk_ref, mv_ref = (refs.pop(0) for _ in range(3))
    else:
        oc_ref = refs.pop(0)
    (ob_ref, az_ref, bz_ref, cz_ref, ga_ref, gb_ref, gc_ref, x_ref, pg_ref,
     wa_ref, wb_ref, wc_ref, wo_ref, out_ref) = refs

    if merge:
        lse = [r[...] for r in st_refs]
        mx = jnp.maximum(jnp.maximum(lse[0], lse[1]), lse[2])
        ex = [jnp.exp(x - mx) for x in lse]
        inv = 1.0 / (ex[0] + ex[1] + ex[2])
        alpha = [e * inv for e in ex]
        low_head = lax.broadcasted_iota(jnp.int32, (tm, LANES), 1) < A_HEAD_DIM
        slabs = []
        for c in range(A_W // LANES):
            sl = slice(c * LANES, (c + 1) * LANES)
            acc = jnp.zeros((tm, LANES), F32)
            for g in range(A_GROUPS):
                w = jnp.where(low_head, alpha[g][:, 2 * c:2 * c + 1], alpha[g][:, 2 * c + 1:2 * c + 2])
                acc = acc + w * o_refs[g][:, sl].astype(F32)
            slabs.append(acc)
        oa = jnp.concatenate(slabs, axis=-1)
    else:
        oa = oa_ref[...].astype(F32)
    ya = (oa * az_ref[...].astype(F32)).astype(BF16)

    if xattn:
        cq = cq_ref[...]
        parts = []
        for h in range(XA_HEADS):
            sl = slice(h * XA_HEAD_DIM, (h + 1) * XA_HEAD_DIM)
            s = pl.dot(cq[:, sl], mk_ref[:, sl], trans_b=True) * (XA_HEAD_DIM ** -0.5)
            p, l = _softmax_rows(s)
            parts.append(jnp.dot(p.astype(BF16), mv_ref[:, sl], preferred_element_type=F32) * (1.0 / l))
        oc = jnp.concatenate(parts, axis=-1)
    else:
        oc = oc_ref[...].astype(F32)
    yc = (oc * cz_ref[...].astype(F32)).astype(BF16)

    yb = (ob_ref[...].astype(F32) * bz_ref[...].astype(F32)).astype(BF16)

    merged = (ga_ref[...].astype(F32) * jnp.dot(ya, wa_ref[...], preferred_element_type=F32)
              + gb_ref[...].astype(F32) * jnp.dot(yb, wb_ref[...], preferred_element_type=F32)
              + gc_ref[...].astype(F32) * jnp.dot(yc, wc_ref[...], preferred_element_type=F32))
    z = jnp.dot(merged.astype(BF16), wo_ref[...], preferred_element_type=F32)
    out_ref[...] = x_ref[...] + _rmsnorm_rows(z, pg_ref[...])


def _final(att_in, xa_in, ob, az, bz, cz, ga, gb, gc, x, post_g, w_pa, w_pb, w_pc, w_out, *, tm, merge, xattn):
    n = x.shape[0]

    def rows(width):
        return pl.BlockSpec((tm, width), lambda i: (i, 0))

    if merge:
        att_specs = [rows(A_W)] * A_GROUPS + [rows(LANES)] * A_GROUPS
    else:
        att_specs = [rows(A_W)]
    if xattn:
        xa_specs = [rows(XA_W), _resident((MEM_LEN, XA_W)), _resident((MEM_LEN, XA_W))]
    else:
        xa_specs = [rows(XA_W)]
    in_specs = (att_specs + xa_specs + [rows(HG_W), rows(A_W), rows(HG_W), rows(XA_W)]
                + [rows(D_MODEL)] * 4 + [_resident((1, D_MODEL))]
                + [_resident((A_W, D_MODEL)), _resident((HG_W, D_MODEL)), _resident((XA_W, D_MODEL)),
                   _resident((D_MODEL, D_MODEL))])
    return pl.pallas_call(
        functools.partial(_final_kernel, tm=tm, merge=merge, xattn=xattn),
        out_shape=jax.ShapeDtypeStruct((n, D_MODEL), F32),
        grid=(n // tm,),
        in_specs=in_specs,
        out_specs=rows(D_MODEL),
        compiler_params=_params(("parallel",), VMEM_LIMIT_BYTES),
        name="final_merge" if merge else "final",
    )(*att_in, *xa_in, ob, az, bz, cz, ga, gb, gc, x, post_g, w_pa, w_pb, w_pc, w_out)


def _head_rows(q_row, head_dim):
    w = q_row.shape[-1]
    sub = lax.broadcasted_iota(jnp.int32, (8, w), 0)
    lane_head = lax.broadcasted_iota(jnp.int32, (8, w), 1) // head_dim
    keep = sub == lane_head
    return jnp.where(keep, jnp.broadcast_to(q_row, (8, w)), 0.0), keep


def _attn_sample_kernel(q0, q1, q2, k0, k1, k2, v0, v1, v2, c0, c1, c2, o_ref):
    q_refs = (q0, q1, q2)
    k_refs = (k0, k1, k2)
    v_refs = (v0, v1, v2)
    c_refs = (c0, c1, c2)
    b = pl.program_id(0)
    outs = []
    lses = []
    keep = None
    for g in range(A_GROUPS):
        dil = A_DILATIONS[g]
        wb = c_refs[g].shape[-1]
        q_row = q_refs[g][pl.ds(b, 1), :]
        k_new = k_refs[g][pl.ds(b, 1), :]
        v_new = v_refs[g][pl.ds(b, 1), :]
        qm, keep = _head_rows(q_row, A_HEAD_DIM)
        kt = c_refs[g][0].astype(BF16)
        vt = c_refs[g][1].astype(BF16)
        s = jnp.dot(qm.astype(BF16), kt, preferred_element_type=F32)
        w_pos = lax.broadcasted_iota(jnp.int32, (8, wb), 1)
        s = jnp.where(w_pos % dil == 0, s, NEG_INF)
        s_new = jnp.sum(qm * k_new, axis=-1, keepdims=True)
        m = jnp.maximum(jnp.max(s, axis=-1, keepdims=True), s_new)
        p = jnp.exp(s - m)
        p_new = jnp.exp(s_new - m)
        l = jnp.sum(p, axis=-1, keepdims=True) + p_new
        o = (pl.dot(p.astype(BF16), vt, trans_b=True) + p_new * v_new) * (1.0 / l)
        outs.append(o)
        lses.append(m + jnp.log(l))
    mx = jnp.maximum(jnp.maximum(lses[0], lses[1]), lses[2])
    ex = [jnp.exp(x - mx) for x in lses]
    inv = 1.0 / (ex[0] + ex[1] + ex[2])
    oa = (ex[0] * outs[0] + ex[1] * outs[1] + ex[2] * outs[2]) * inv
    o_ref[pl.ds(b, 1), :] = jnp.sum(jnp.where(keep, oa, 0.0), axis=0, keepdims=True)


def _hgrn_and_attn_sample_kernel(*refs, th):
    hg_in, at_in = refs[0:5], refs[5:17]
    hg_out, at_out, hg_state = refs[17:19], refs[19], refs[20]
    _hgrn_prompt_kernel(*hg_in, *hg_out, hg_state, th=th)
    _attn_sample_kernel(*at_in, at_out)


def _hgrn_and_attn_sample(q, lf, k, v, gain, qs, ks, vs, caches):
    t = q.shape[0]
    bd = qs[0].shape[0]
    assert t % bd == 0 and (t // bd) % HG_CHUNK == 0
    th = t // bd
    views = []
    c_specs = []
    for g in range(A_GROUPS):
        c = caches[g]
        wb = c.shape[1]
        assert wb == A_NKEY * A_DILATIONS[g], "window buffer must hold exactly 128 dilated keys"
        views.append(jnp.transpose(c, (0, 2, 3, 4, 1)).reshape(bd, 2, A_W, wb))
        c_specs.append(pl.BlockSpec((None, 2, A_W, wb), lambda b: (b, 0, 0, 0)))
    rows = pl.BlockSpec((th, HG_W), lambda i: (i, 0))
    full = pl.BlockSpec((bd, A_W), lambda b: (0, 0))
    state = pl.BlockSpec((HG_HEADS, HG_K, HG_V), lambda i: (0, 0, 0))
    return pl.pallas_call(
        functools.partial(_hgrn_and_attn_sample_kernel, th=th),
        out_shape=[jax.ShapeDtypeStruct((t, HG_W), BF16), jax.ShapeDtypeStruct((HG_HEADS, HG_K, HG_V), F32),
                   jax.ShapeDtypeStruct((bd, A_W), F32)],
        grid=(bd,),
        in_specs=[rows] * 4 + [_resident((1, HG_W))] + [full] * 9 + c_specs,
        out_specs=[rows, state, full],
        scratch_shapes=[pltpu.VMEM((HG_HEADS, HG_V, HG_K), F32)],
        compiler_params=_params(("arbitrary",), VMEM_LIMIT_BYTES),
        name="hgrn_prompt_attn_sample",
    )(q, lf, k, v, gain, *qs, *ks, *vs, *views)


def _xattn_sample_kernel(q_ref, c_ref, o_ref, *, sb):
    stride = 2 * XA_HEADS
    for b in range(sb):
        for h in range(XA_HEADS):
            sl = slice(h * XA_HEAD_DIM, (h + 1) * XA_HEAD_DIM)
            kh = c_ref[b, pl.ds(h, MEM_LEN, stride=stride), :]
            vh = c_ref[b, pl.ds(XA_HEADS + h, MEM_LEN, stride=stride), :]
            q_h = q_ref[b:b + 1, sl].astype(F32)
            s = jnp.sum(kh * q_h, axis=-1, keepdims=True) * (XA_HEAD_DIM ** -0.5)
            m = jnp.max(s, axis=0, keepdims=True)
            p = jnp.exp(s - m)
            l = jnp.sum(p, axis=0, keepdims=True)
            o_ref[b:b + 1, sl] = jnp.sum(p * vh, axis=0, keepdims=True) * (1.0 / l)


def _xattn_sample(cq, mem_cache, *, sb):
    bd = cq.shape[0]
    rows = MEM_LEN * 2 * XA_HEADS
    view = mem_cache.reshape(bd, rows, XA_HEAD_DIM)
    return pl.pallas_call(
        functools.partial(_xattn_sample_kernel, sb=sb),
        out_shape=jax.ShapeDtypeStruct((bd, XA_W), F32),
        grid=(bd // sb,),
        in_specs=[pl.BlockSpec((sb, XA_W), lambda i: (i, 0)),
                  pl.BlockSpec((sb, rows, XA_HEAD_DIM), lambda i: (i, 0, 0))],
        out_specs=pl.BlockSpec((sb, XA_W), lambda i: (i, 0)),
        compiler_params=_params(("parallel",), VMEM_LIMIT_BYTES),
        name="xattn_sample",
    )(cq, view)


def _hgrn_sample_kernel(qt_ref, lft_ref, kt_ref, v_ref, s0_ref, gain_ref, o_ref, s_ref, *, sb):
    for b in range(sb):
        for h in range(HG_HEADS):
            sl = slice(h * HG_K, (h + 1) * HG_K)
            d_col = jnp.exp(lft_ref[0, sl, b:b + 1])
            k_col = kt_ref[0, sl, b:b + 1]
            q_col = qt_ref[0, sl, b:b + 1]
            v_row = v_ref[b:b + 1, sl]
            s_new = d_col * s0_ref[b, h] + k_col * v_row
            s_ref[b, h] = s_new
            o = jnp.sum(q_col * s_new, axis=0, keepdims=True)
            on = o * lax.rsqrt(jnp.mean(o * o, axis=-1, keepdims=True) + EPS) * gain_ref[:, sl]
            o_ref[b:b + 1, sl] = on


def _hgrn_sample(q, lf, kin, v, s0, gain, *, sb):
    bd = q.shape[0]
    steps = bd // sb

    def cols(a):
        return a.reshape(steps, sb, HG_W).transpose(0, 2, 1)

    col_spec = pl.BlockSpec((1, HG_W, sb), lambda i: (i, 0, 0))
    st_spec = pl.BlockSpec((sb, HG_HEADS, HG_K, HG_V), lambda i: (i, 0, 0, 0))
    row = pl.BlockSpec((sb, HG_W), lambda i: (i, 0))
    return pl.pallas_call(
        functools.partial(_hgrn_sample_kernel, sb=sb),
        out_shape=[jax.ShapeDtypeStruct((bd, HG_W), F32), jax.ShapeDtypeStruct(s0.shape, F32)],
        grid=(steps,),
        in_specs=[col_spec, col_spec, col_spec, row, st_spec, _resident((1, HG_W))],
        out_specs=[row, st_spec],
        compiler_params=_params(("parallel",)),
        name="hgrn_sample",
    )(cols(q), cols(lf), cols(kin), v, s0, gain)


def kernel(x_prompt, x_sample, mem_prompt, cache_win128_kv, cache_win512_kv, cache_win2048_kv, state_hgrn, cache_mem_kv, norm_pre, norm_post, w_in, hgrn_lb_logits, hgrn_out_norm, mem_norm, w_mem_kv, w_branch_a, w_branch_b, w_branch_c, w_out):
    depth = w_in.shape[0]
    assert depth == 1, "single-layer trunk"
    layer = 0
    bp, t, _ = x_prompt.shape
    bd, s_len, _ = x_sample.shape
    assert bp == 1 and s_len == 1
    caches = (cache_win128_kv[layer], cache_win512_kv[layer], cache_win2048_kv[layer])

    w_att = w_in[layer][:, :N_ATT].astype(BF16)
    w_rest = w_in[layer][:, N_ATT:].astype(BF16)
    pre_g = norm_pre[layer].reshape(1, D_MODEL)
    post_g = norm_post[layer].reshape(1, D_MODEL)
    hg_g = hgrn_out_norm[layer].reshape(1, HG_W)
    w_pa, w_pb, w_pc, w_o = (w[layer].astype(BF16) for w in (w_branch_a, w_branch_b, w_branch_c, w_out))
    half = A_HEAD_DIM // 2
    inv = ROPE_THETA ** (-jnp.arange(0, A_HEAD_DIM, 2, dtype=F32) / A_HEAD_DIM)
    invf = jnp.tile(inv, LANES // half).reshape(1, LANES)

    xp = x_prompt.reshape(t, D_MODEL)
    tail_rows = tuple(min(w, t) for w in A_WINDOWS)
    q_p, k_p, v_p, tails_p = _inproj_attn(xp, pre_g, w_att, invf, tm=512, pos_base=0, pos_step=1,
                                          tail_rows=tail_rows, out_dtype=BF16, row_dil=A_DILATIONS)
    (az, bq, lf, kin, bi, bz, cq, cz, ga, gb, gc) = _inproj_rest(
        xp, pre_g, w_rest, hgrn_lb_logits, tm=512, layer=layer, hg_dtype=BF16)
    att = [_attn_prompt(q_p[g], k_p[g], v_p[g], A_DILATIONS[g], rows_per_step=2048) for g in range(A_GROUPS)]

    xs = x_sample.reshape(bd, D_MODEL)
    q_s, k_s, v_s, tails_s = _inproj_attn(xs, pre_g, w_att, invf, tm=bd, pos_base=PAST_LEN, pos_step=0,
                                          tail_rows=(bd,) * A_GROUPS, out_dtype=F32, row_dil=(1,) * A_GROUPS)
    ob_p, s_p, oa_s = _hgrn_and_attn_sample(bq, lf, kin, bi, hg_g, q_s, k_s, v_s, caches)

    mem_kv = _mem_kv(mem_prompt.reshape(MEM_LEN, D_MODEL), mem_norm[layer].reshape(1, D_MODEL),
                     w_mem_kv[layer].astype(BF16))
    mk = mem_kv[:, :XA_W].astype(BF16)
    mv = mem_kv[:, XA_W:].astype(BF16)
    y_p = _final([a[0] for a in att] + [a[1] for a in att], [cq, mk, mv], ob_p, az, bz, cz, ga, gb, gc,
                 xp, post_g, w_pa, w_pb, w_pc, w_o, tm=512, merge=True, xattn=True)

    (az_s, bq_s, lf_s, kin_s, bi_s, bz_s, cq_s, cz_s, ga_s, gb_s, gc_s) = _inproj_rest(
        xs, pre_g, w_rest, hgrn_lb_logits, tm=bd, layer=layer, hg_dtype=F32)
    oc_s = _xattn_sample(cq_s, cache_mem_kv[layer], sb=8)
    ob_s, s_s = _hgrn_sample(bq_s, lf_s, kin_s, bi_s, state_hgrn[layer], hg_g, sb=8)
    y_s = _final([oa_s], [oc_s], ob_s, az_s, bz_s, cz_s, ga_s, gb_s, gc_s,
                 xs, post_g, w_pa, w_pb, w_pc, w_o, tm=bd, merge=False, xattn=False)

    new_win_p = [tails_p[g].reshape(1, 1, tail_rows[g], 2, A_HEADS, A_HEAD_DIM) for g in range(A_GROUPS)]
    new_win_s = [tails_s[g].reshape(1, bd, 1, 2, A_HEADS, A_HEAD_DIM) for g in range(A_GROUPS)]
    return (y_p.reshape(bp, t, D_MODEL), y_s.reshape(bd, 1, D_MODEL),
            new_win_p[0], new_win_p[1], new_win_p[2],
            s_p.reshape(1, 1, HG_HEADS, HG_K, HG_V),
            mem_kv.reshape(1, 1, MEM_LEN, 2, XA_HEADS, XA_HEAD_DIM),
            new_win_s[0], new_win_s[1], new_win_s[2],
            s_s.reshape(1, bd, HG_HEADS, HG_K, HG_V))
```

```python
import functools
from typing import Callable, NamedTuple

import jax
import jax.numpy as jnp
from jax import lax
from jax.experimental import pallas as pl
from jax.experimental.pallas import tpu as pltpu

D_MODEL = 1024
PAST_LEN = 8192
A_WINDOWS = (128, 512, 2048)
A_DILATIONS = (1, 4, 16)
A_GROUPS = 3
A_HEADS = 8
A_HEAD_DIM = 64
A_NKEY = 128
ROPE_THETA = 10000.0
HG_HEADS = 4
HG_K = 128
HG_V = 128
HG_CHUNK = 64
HG_BLOCK = 8
ATTN_BLOCKS_IN_FLIGHT = 4
MEM_LEN = 256
XA_HEADS = 4
XA_HEAD_DIM = 128
EPS = 1e-6

A_W = A_HEADS * A_HEAD_DIM
HG_W = HG_HEADS * HG_K
XA_W = XA_HEADS * XA_HEAD_DIM
N_ATT = 3 * A_GROUPS * A_W
COL_AZ = N_ATT
COL_BQ = COL_AZ + A_W
COL_BF = COL_BQ + HG_W
COL_BI = COL_BF + HG_W
COL_BZ = COL_BI + HG_W
COL_CQ = COL_BZ + HG_W
COL_CZ = COL_CQ + XA_W
COL_GA = COL_CZ + XA_W
COL_GB = COL_GA + D_MODEL
COL_GC = COL_GB + D_MODEL
N_IN = COL_GC + D_MODEL

LANES = 128
VMEM_LIMIT_BYTES = 56 * 1024 * 1024

F32 = jnp.float32
BF16 = jnp.bfloat16
NEG_INF = float("-inf")


def _params(semantics, vmem=None):
    return pltpu.CompilerParams(dimension_semantics=semantics, vmem_limit_bytes=vmem)


def _resident(shape):
    return pl.BlockSpec(shape, lambda *_: (0,) * len(shape), pipeline_mode=pl.Buffered(1))


def _resident_columns(rows, col0, width):
    return pl.BlockSpec((pl.Element(rows), pl.Element(width)), lambda *_: (0, col0), pipeline_mode=pl.Buffered(1))


class _Rider(NamedTuple):
    body: Callable
    inputs: tuple = ()
    in_specs: tuple = ()
    out_shapes: tuple = ()
    out_specs: tuple = ()


_NO_RIDER = _Rider(body=lambda step: None)


def _join_riders(a, b):
    na_in, nb_in, na_out = len(a.inputs), len(b.inputs), len(a.out_shapes)

    def body(step, *refs):
        ins, outs = refs[:na_in + nb_in], refs[na_in + nb_in:]
        a.body(step, *ins[:na_in], *outs[:na_out])
        b.body(step, *ins[na_in:], *outs[na_out:])

    return _Rider(body, a.inputs + b.inputs, a.in_specs + b.in_specs, a.out_shapes + b.out_shapes,
                  a.out_specs + b.out_specs)


def _rmsnorm_rows(x, gain):
    ms = jnp.mean(x * x, axis=-1, keepdims=True)
    return x * lax.rsqrt(ms + EPS) * gain


def _sigmoid(x):
    return 1.0 / (1.0 + jnp.exp(-x))


def _silu(x):
    return x * _sigmoid(x)


def _tile_row_order(tm, d):
    j = lax.broadcasted_iota(jnp.int32, (tm, LANES), 0)
    per = tm // d
    return (j % per) * d + j // per


def _inproj_attn_kernel(x_ref, g_ref, w_ref, invf_ref, *refs, tm, n_tiles, pos_base, pos_step, tails, row_dil,
                        rider_body, n_rider_in, n_rider_out):
    rider_in, refs = refs[:n_rider_in], refs[n_rider_in:]
    q_refs = refs[0:3]
    k_refs = refs[3:6]
    v_refs = refs[6:9]
    t_refs = refs[9:12]
    rider_out = refs[12:12 + n_rider_out]
    cos_tab, sin_tab, xa_ref, xb_ref, xp_ref = refs[12 + n_rider_out:]
    i = pl.program_id(0)
    n_slab = A_W // LANES
    n_xslab = D_MODEL // LANES
    dils = sorted(set(row_dil))

    @pl.when(i == 0)
    def _():
        for t_idx, d in enumerate(dils):
            ang0 = (_tile_row_order(tm, d) * pos_step).astype(F32) * invf_ref[...]
            cos_tab[t_idx] = jnp.cos(ang0)
            sin_tab[t_idx] = jnp.sin(ang0)

    rider_body(i, *rider_in, *rider_out)

    lane = lax.broadcasted_iota(jnp.int32, (tm, LANES), 1)
    first_half = (lane % A_HEAD_DIM) < (A_HEAD_DIM // 2)
    base =(pos_base + i * tm * pos_step).astype(F32) * invf_ref[...]
    cos_b = jnp.cos(base)
    sin_b = jnp.sin(base)

    def trig(d):
        t_idx = dils.index(d)
        c0, s0 = cos_tab[t_idx], sin_tab[t_idx]
        cos = c0 * cos_b - s0 * sin_b
        sin = s0 * cos_b + c0 * sin_b
        return cos, jnp.where(first_half, -sin, sin)

    def rope(acc, cos, sin_signed):
        outs = []
        for c in range(n_slab):
            xc = acc[:, c * LANES:(c + 1) * LANES]
            partner = jnp.where(first_half, pltpu.roll(xc, LANES - 32, 1), pltpu.roll(xc, 32, 1))
            outs.append(xc * cos + partner * sin_signed)
        return jnp.concatenate(outs, axis=-1)

    xn32 = _rmsnorm_rows(x_ref[...], g_ref[...])
    xn = xn32.astype(BF16)

    lhs = {1: xn}
    if max(row_dil) > 1:
        assert set(row_dil) <= {1, 4, 16}
        per4 = tm // 4
        for c in range(n_xslab):
            xa_ref[c] = xn32[:, c * LANES:(c + 1) * LANES]
        for c in range(n_xslab):
            for r in range(4):
                blk4 = xa_ref[c, pl.ds(r, per4, stride=4), :]
                xb_ref[c, r * per4:(r + 1) * per4, :] = blk4
                xp_ref[0, r * per4:(r + 1) * per4, c * LANES:(c + 1) * LANES] = blk4.astype(BF16)
        lhs[4] = xp_ref[0]
        if 16 in row_dil:
            per16 = tm // 16
            for c in range(n_xslab):
                for r1 in range(4):
                    for r2 in range(4):
                        res = 4 * r2 + r1
                        xp_ref[1, res * per16:(res + 1) * per16, c * LANES:(c + 1) * LANES] = (
                            xb_ref[c, pl.ds(r1 * per4 + r2, per16, stride=4), :].astype(BF16))
            lhs[16] = xp_ref[1]

    def store_rows(ref, d, val):
        per = tm // d
        for r in range(d):
            ref[:, r * A_W:(r + 1) * A_W] = val[r * per:(r + 1) * per, :].astype(ref.dtype)

    def tail_block(g, col0, natural_val):
        first_tile, rows = tails[g]
        cond = (i >= first_tile) if rows >= tm else (i == n_tiles - 1)

        @pl.when(cond)
        def _():
            val = natural_val()
            t_refs[g][:, col0:col0 + A_W] = val if rows >= tm else val[tm - rows:, :]

    cos_n, sin_n = trig(1)
    for g in range(A_GROUPS):
        d = row_dil[g]
        cos, sin_s = (cos_n, sin_n) if d == 1 else trig(d)
        wq = w_ref[:, g * A_W:(g + 1) * A_W]
        wk = w_ref[:, (A_GROUPS + g) * A_W:(A_GROUPS + g + 1) * A_W]
        wv = w_ref[:, (2 * A_GROUPS + g) * A_W:(2 * A_GROUPS + g + 1) * A_W]
        q = rope(jnp.dot(lhs[d], wq, preferred_element_type=F32), cos, sin_s) * (A_HEAD_DIM ** -0.5)
        store_rows(q_refs[g], d, q)
        kr = rope(jnp.dot(lhs[d], wk, preferred_element_type=F32), cos, sin_s)
        store_rows(k_refs[g], d, kr)
        v = jnp.dot(lhs[d], wv, preferred_element_type=F32)
        store_rows(v_refs[g], d, v)
        if d == 1:
            tail_block(g, 0, lambda kr=kr: kr)
            tail_block(g, A_W, lambda v=v: v)
        else:
            tail_block(g, 0, lambda wk=wk: rope(jnp.dot(xn, wk, preferred_element_type=F32), cos_n, sin_n))
            tail_block(g, A_W, lambda wv=wv: jnp.dot(xn, wv, preferred_element_type=F32))


def _inproj_attn(x, gain, w_att, invf, *, tm, pos_base, pos_step, tail_rows, out_dtype, row_dil, rider=_NO_RIDER):
    n = x.shape[0]
    n_tiles = n // tm
    assert 1 in row_dil
    n_orders = len(set(row_dil))
    tails = []
    tail_specs = []
    tail_shapes = []
    for g in range(A_GROUPS):
        rows = tail_rows[g]
        first_tile = (n - rows) // tm
        tails.append((first_tile, rows))
        blk = min(tm, rows)
        tail_specs.append(pl.BlockSpec((blk, 2 * A_W), functools.partial(
            lambda i, ft: (jnp.maximum(i - ft, 0), 0), ft=first_tile)))
        tail_shapes.append(jax.ShapeDtypeStruct((rows, 2 * A_W), F32))
    row_specs = [pl.BlockSpec((tm // d, d * A_W), lambda i: (i, 0)) for d in row_dil] * 3
    row_shapes = [jax.ShapeDtypeStruct((n // d, d * A_W), out_dtype) for d in row_dil] * 3
    kernel = functools.partial(_inproj_attn_kernel, tm=tm, n_tiles=n_tiles, pos_base=pos_base,
                               pos_step=pos_step, tails=tuple(tails), row_dil=tuple(row_dil),
                               rider_body=rider.body, n_rider_in=len(rider.inputs), n_rider_out=len(rider.out_shapes))
    outs = pl.pallas_call(
        kernel,
        out_shape=row_shapes + tail_shapes + list(rider.out_shapes),
        grid=(n_tiles,),
        in_specs=[pl.BlockSpec((tm, D_MODEL), lambda i: (i, 0)),
                  _resident((1, D_MODEL)),
                  _resident_columns(D_MODEL, 0, N_ATT),
                  _resident((1, LANES))] + list(rider.in_specs),
        out_specs=row_specs + tail_specs + list(rider.out_specs),
        scratch_shapes=[pltpu.VMEM((n_orders, tm, LANES), F32), pltpu.VMEM((n_orders, tm, LANES), F32),
                        pltpu.VMEM((D_MODEL // LANES, tm, LANES), F32), pltpu.VMEM((D_MODEL // LANES, tm, LANES), F32),
                        pltpu.VMEM((2, tm, D_MODEL), BF16)],
        compiler_params=_params(("arbitrary",), VMEM_LIMIT_BYTES),
        name="inproj_attn",
    )(x, gain, w_att, invf, *rider.inputs)
    return outs[0:3], outs[3:6], outs[6:9], outs[9:12], outs[12:]


def _inproj_rest_kernel(x_ref, g_ref, w_ref, lbl_ref, *refs, layer, rider_body, n_rider_in):
    rider_in, refs = refs[:n_rider_in], refs[n_rider_in:]
    (az_ref, bq_ref, lf_ref, kin_ref, bi_ref, bz_ref, cq_ref, cz_ref, ga_ref, gb_ref, gc_ref) = refs[:11]
    rider_body(pl.program_id(0), *rider_in, *refs[11:])
    xn = _rmsnorm_rows(x_ref[...], g_ref[...]).astype(BF16)

    def proj(col, width):
        c0 = col - N_ATT
        return jnp.dot(xn, w_ref[:, c0:c0 + width], preferred_element_type=F32)

    logits = lbl_ref[...]
    e = jnp.exp(logits - jnp.max(logits, axis=0, keepdims=True))
    lb = jnp.sum(e[0:layer + 1, :], axis=0, keepdims=True) / jnp.sum(e, axis=0, keepdims=True)

    az_ref[...] = _silu(proj(COL_AZ, A_W)).astype(az_ref.dtype)
    bq_ref[...] = proj(COL_BQ, HG_W).astype(bq_ref.dtype)
    bf = proj(COL_BF, HG_W)
    lf_ref[...] = jnp.log(lb + (1.0 - lb) * _sigmoid(bf))
    kin_ref[...] = ((1.0 - lb) * _sigmoid(-bf)).astype(kin_ref.dtype)
    bi_ref[...] = proj(COL_BI, HG_W).astype(bi_ref.dtype)
    bz_ref[...] = _silu(proj(COL_BZ, HG_W)).astype(bz_ref.dtype)
    cq_ref[...] = proj(COL_CQ, XA_W).astype(cq_ref.dtype)
    cz_ref[...] = _silu(proj(COL_CZ, XA_W)).astype(cz_ref.dtype)
    ga_ref[...] = _sigmoid(proj(COL_GA, D_MODEL)).astype(ga_ref.dtype)
    gb_ref[...] = _sigmoid(proj(COL_GB, D_MODEL)).astype(gb_ref.dtype)
    gc_ref[...] = _sigmoid(proj(COL_GC, D_MODEL)).astype(gc_ref.dtype)


def _inproj_rest(x, gain, w_rest, lb_logits, *, tm, layer, hg_dtype, cq_dtype, rider=_NO_RIDER):
    n = x.shape[0]
    n_rest = N_IN - N_ATT
    s512 = pl.BlockSpec((tm, A_W), lambda i: (i, 0))
    s1024 = pl.BlockSpec((tm, D_MODEL), lambda i: (i, 0))

    def sds(width, dt):
        return jax.ShapeDtypeStruct((n, width), dt)

    outs = pl.pallas_call(
        functools.partial(_inproj_rest_kernel, layer=layer, rider_body=rider.body, n_rider_in=len(rider.inputs)),
        out_shape=[sds(A_W, BF16), sds(HG_W, hg_dtype), sds(HG_W, F32), sds(HG_W, hg_dtype), sds(HG_W, hg_dtype),
                   sds(HG_W, BF16), sds(XA_W, cq_dtype), sds(XA_W, BF16),
                   sds(D_MODEL, BF16), sds(D_MODEL, BF16), sds(D_MODEL, BF16)] + list(rider.out_shapes),
        grid=(n // tm,),
        in_specs=[pl.BlockSpec((tm, D_MODEL), lambda i: (i, 0)),
                  _resident((1, D_MODEL)),
                  _resident_columns(D_MODEL, N_ATT, n_rest),
                  _resident(lb_logits.shape)] + list(rider.in_specs),
        out_specs=[s512] * 8 + [s1024] * 3 + list(rider.out_specs),
        compiler_params=_params(("arbitrary",), VMEM_LIMIT_BYTES),
        name="inproj_rest",
    )(x, gain, w_rest, lb_logits, *rider.inputs)
    return outs[:11], outs[11:]


def _attn_prompt_kernel(q_ref, kc_ref, vc_ref, kp_ref, vp_ref, o_ref, st_ref, kbuf, vbuf, o_acc, st_acc,
                        *, tmv, dil, rps):
    n = pl.program_id(0)
    rstep = pl.program_id(1)
    blk = A_NKEY

    qi = lax.broadcasted_iota(jnp.int32, (blk, 2 * blk), 0)
    ki = lax.broadcasted_iota(jnp.int32, (blk, 2 * blk), 1)
    dist = qi + blk - ki
    band = (dist >= 0) & (dist <= A_NKEY)
    bias = jnp.where(band, 0.0, NEG_INF).astype(F32)
    bias_first = jnp.where(band & (ki >= blk), 0.0, NEG_INF).astype(F32)
    lane = lax.broadcasted_iota(jnp.int32, (blk, LANES), 1)
    low_head = lane < A_HEAD_DIM
    head_mask = (jnp.where(low_head, 1.0, 0.0).astype(BF16), jnp.where(low_head, 0.0, 1.0).astype(BF16))

    for rr in range(rps):
        cs = slice(rr * A_W, (rr + 1) * A_W)
        kbuf[rr, 0:blk, :] = kp_ref[:, cs]
        kbuf[rr, blk:, :] = kc_ref[:, cs]
        vbuf[rr, 0:blk, :] = vp_ref[:, cs]
        vbuf[rr, blk:, :] = vc_ref[:, cs]
        res = rstep * rps + rr

        def body(b, carry, cs=cs, res=res, rr=rr):
            r0 = pl.multiple_of(b * blk, blk)
            is_first = jnp.logical_and(n == 0, b == 0)
            bias_b = jnp.where(is_first, bias_first, bias)
            qb = q_ref[pl.ds(r0, blk), cs]
            kb = kbuf[rr, pl.ds(r0, 2 * blk), :]
            vb = vbuf[rr, pl.ds(r0, 2 * blk), :]
            out_rows = pl.ds(r0 * dil + res, blk, stride=dil)
            stats = jnp.zeros((blk, LANES), F32)
            for pair in range(A_HEADS // 2):
                sl = slice(pair * LANES, (pair + 1) * LANES)
                qp = qb[:, sl]
                kp = kb[:, sl]
                vp = vb[:, sl]
                outs = []
                for hh in range(2):
                    qm = qp * head_mask[hh]
                    s = pl.dot(qm, kp, trans_b=True) + bias_b
                    m = jnp.max(s, axis=-1, keepdims=True)
                    p = jnp.exp(s - m)
                    l = jnp.sum(p, axis=-1, keepdims=True)
                    pv = jnp.dot(p.astype(BF16), vp, preferred_element_type=F32)
                    outs.append(pv * (1.0 / l))
                    lse = m + jnp.log(l)
                    stats = jnp.where(lane == 2 * pair + hh, lse, stats)
                o_acc[pair, out_rows, :] = jnp.where(low_head, outs[0], outs[1])
            st_acc[out_rows, :] = stats
            return carry

        lax.fori_loop(0, tmv // blk, body, 0, unroll=min(ATTN_BLOCKS_IN_FLIGHT, tmv // blk))

    @pl.when(rstep == pl.num_programs(1) - 1)
    def _():
        for pair in range(A_HEADS // 2):
            o_ref[:, pair * LANES:(pair + 1) * LANES] = o_acc[pair].astype(o_ref.dtype)
        st_ref[...] = st_acc[...]


def _attn_prompt(qv, kv, vv, dil, *, rows_per_step):
    l = qv.shape[0]
    t = l * dil
    blk = A_NKEY
    tmv = max(rows_per_step // dil, blk)
    rps = min(dil, max(1, ATTN_BLOCKS_IN_FLIGHT * blk // tmv))
    assert t % (tmv * dil) == 0 and dil % rps == 0
    cur = pl.BlockSpec((tmv, rps * A_W), lambda n, r: (n, r))
    prev = pl.BlockSpec((blk, rps * A_W), lambda n, r: (jnp.maximum(n * (tmv // blk) - 1, 0), r))
    rows = tmv * dil
    return pl.pallas_call(
        functools.partial(_attn_prompt_kernel, tmv=tmv, dil=dil, rps=rps),
        out_shape=[jax.ShapeDtypeStruct((t, A_W), BF16), jax.ShapeDtypeStruct((t, LANES), F32)],
        grid=(t // rows, dil // rps),
        in_specs=[cur, cur, cur, prev, prev],
        out_specs=[pl.BlockSpec((rows, A_W), lambda n, r: (n, 0)), pl.BlockSpec((rows, LANES), lambda n, r: (n, 0))],
        scratch_shapes=[pltpu.VMEM((rps, tmv + blk, A_W), BF16), pltpu.VMEM((rps, tmv + blk, A_W), BF16),
                        pltpu.VMEM((A_HEADS // 2, rows, LANES), F32), pltpu.VMEM((rows, LANES), F32)],
        compiler_params=_params(("parallel", "arbitrary"), VMEM_LIMIT_BYTES),
        name=f"attn_prompt_d{dil}",
    )(qv, kv, vv, kv, vv)


def _hgrn_prompt_kernel(q_ref, lf_ref, k_ref, v_ref, gain_ref, o_ref, s_out_ref, st_ref, *, th):
    i = pl.program_id(0)
    c_rows = HG_CHUNK
    nblk = c_rows // HG_BLOCK
    shape3 = (nblk, HG_BLOCK, HG_K)

    @pl.when(i == 0)
    def _():
        st_ref[...] = jnp.zeros_like(st_ref)

    sub = lax.broadcasted_iota(jnp.int32, shape3, 1)
    a_row = lax.broadcasted_iota(jnp.int32, (c_rows, c_rows), 0) // HG_BLOCK
    a_col = lax.broadcasted_iota(jnp.int32, (c_rows, c_rows), 1) // HG_BLOCK

    def shift(x3, s):
        return pltpu.roll(x3, s, 1)

    def chunk(c, carry):
        r0 = pl.multiple_of(c * c_rows, c_rows)
        for h in range(HG_HEADS):
            sl = slice(h * HG_K, (h + 1) * HG_K)
            q = q_ref[pl.ds(r0, c_rows), sl].astype(F32)
            k = k_ref[pl.ds(r0, c_rows), sl].astype(F32)
            v = v_ref[pl.ds(r0, c_rows), sl].astype(F32)
            q3, k3, v3 = (a.reshape(shape3) for a in (q, k, v))
            cum = lf_ref[pl.ds(r0, c_rows), sl].reshape(shape3)
            s = 1
            while s < HG_BLOCK:
                cum = cum + jnp.where(sub >= s, shift(cum, s), 0.0)
                s *= 2
            anchors = [jnp.zeros((1, 1, HG_K), F32)]
            for j in range(nblk):
                anchors.append(anchors[j] + cum[j:j + 1, HG_BLOCK - 1:HG_BLOCK, :])
            b3 = cum + jnp.concatenate(anchors[:nblk], axis=0)
            b = b3.reshape(c_rows, HG_K)
            b_last = anchors[nblk].reshape(1, HG_K)

            o3 = jnp.sum(q3 * k3, axis=-1, keepdims=True) * v3
            for d in range(1, HG_BLOCK):
                dec = jnp.exp(jnp.where(sub >= d, cum - shift(cum, d), NEG_INF))
                a = jnp.sum(q3 * shift(k3, d) * dec, axis=-1, keepdims=True)
                o3 = o3 + a * shift(v3, d)
            o = o3.reshape(c_rows, HG_K)

            b_end = jnp.concatenate(anchors[1:], axis=0)
            k_hat = (k3 * jnp.exp(b_end - b3)).reshape(c_rows, HG_K).astype(BF16)
            q_ver = [(q3 * jnp.exp(jnp.minimum(b3 - anchors[j + 1], 0.0))).reshape(c_rows, HG_K).astype(BF16)
                     for j in range(nblk - 1)]
            r = pl.dot(jnp.concatenate(q_ver, axis=0), k_hat, trans_b=True)
            att = jnp.zeros((c_rows, c_rows), F32)
            for j in range(nblk - 1):
                att = jnp.where(a_col == j, r[j * c_rows:(j + 1) * c_rows, :], att)
            att = jnp.where(a_row > a_col, att, 0.0)
            vb = v.astype(BF16)
            o = o + jnp.dot(att.astype(BF16), vb, preferred_element_type=F32)

            st = st_ref[h]
            o = o + pl.dot((q * jnp.exp(b)).astype(BF16), st.astype(BF16), trans_b=True)
            k_end = (k * jnp.exp(b_last - b)).astype(BF16)
            st_ref[h] = st * jnp.exp(b_last) + pl.dot(vb, k_end, trans_a=True)

            on = o * lax.rsqrt(jnp.mean(o * o, axis=-1, keepdims=True) + EPS) * gain_ref[:, sl]
            o_ref[pl.ds(r0, c_rows), sl] = on.astype(o_ref.dtype)
        return carry

    n_chunks = th // c_rows
    lax.fori_loop(0, n_chunks, chunk, 0, unroll=n_chunks <= 2)

    @pl.when(i == pl.num_programs(0) - 1)
    def _():
        for h in range(HG_HEADS):
            s_out_ref[h] = st_ref[h].T


def _mem_kv_kernel(m_ref, g_ref, w_ref, o_ref):
    xn = _rmsnorm_rows(m_ref[...], g_ref[...]).astype(BF16)
    o_ref[...] = jnp.dot(xn, w_ref[...], preferred_element_type=F32)


def _mem_kv(mem, gain, w):
    m = mem.shape[0]
    return pl.pallas_call(
        _mem_kv_kernel,
        out_shape=jax.ShapeDtypeStruct((m, 2 * XA_W), F32),
        name="mem_kv",
    )(mem, gain, w)


def _softmax_rows(s):
    m = jnp.max(s, axis=-1, keepdims=True)
    p = jnp.exp(s - m)
    return p, jnp.sum(p, axis=-1, keepdims=True)


def _final_kernel(*refs, tm, merge, xattn):
    refs = list(refs)
    if merge:
        o_refs = [refs.pop(0) for _ in range(A_GROUPS)]
        st_refs = [refs.pop(0) for _ in range(A_GROUPS)]
    else:
        oa_ref = refs.pop(0)
    if xattn:
        cq_ref, mk_ref, mv_ref = (refs.pop(0) for _ in range(3))
    else:
        oc_ref = refs.pop(0)
    (ob_ref, az_ref, bz_ref, cz_ref, ga_ref, gb_ref, gc_ref, x_ref, pg_ref,
     wa_ref, wb_ref, wc_ref, wo_ref, out_ref) = refs

    if merge:
        lse = [r[...] for r in st_refs]
        mx = jnp.maximum(jnp.maximum(lse[0], lse[1]), lse[2])
        ex = [jnp.exp(x - mx) for x in lse]
        inv = 1.0 / (ex[0] + ex[1] + ex[2])
        alpha = [e * inv for e in ex]
        low_head = lax.broadcasted_iota(jnp.int32, (tm, LANES), 1) < A_HEAD_DIM
        slabs = []
        for c in range(A_W // LANES):
            sl = slice(c * LANES, (c + 1) * LANES)
            acc = jnp.zeros((tm, LANES), F32)
            for g in range(A_GROUPS):
                w = jnp.where(low_head, alpha[g][:, 2 * c:2 * c + 1], alpha[g][:, 2 * c + 1:2 * c + 2])
                acc = acc + w * o_refs[g][:, sl].astype(F32)
            slabs.append(acc)
        oa = jnp.concatenate(slabs, axis=-1)
    else:
        oa = oa_ref[...].astype(F32)
    ya = (oa * az_ref[...].astype(F32)).astype(BF16)

    if xattn:
        cq = cq_ref[...]
        parts = []
        for h in range(XA_HEADS):
            sl = slice(h * XA_HEAD_DIM, (h + 1) * XA_HEAD_DIM)
            s = pl.dot(cq[:, sl], mk_ref[:, sl], trans_b=True) * (XA_HEAD_DIM ** -0.5)
            p, l = _softmax_rows(s)
            parts.append(jnp.dot(p.astype(BF16), mv_ref[:, sl], preferred_element_type=F32) * (1.0 / l))
        oc = jnp.concatenate(parts, axis=-1)
    else:
        oc = oc_ref[...].astype(F32)
    yc = (oc * cz_ref[...].astype(F32)).astype(BF16)

    yb = (ob_ref[...].astype(F32) * bz_ref[...].astype(F32)).astype(BF16)

    merged = (ga_ref[...].astype(F32) * jnp.dot(ya, wa_ref[...], preferred_element_type=F32)
              + gb_ref[...].astype(F32) * jnp.dot(yb, wb_ref[...], preferred_element_type=F32)
              + gc_ref[...].astype(F32) * jnp.dot(yc, wc_ref[...], preferred_element_type=F32))
    z = jnp.dot(merged.astype(BF16), wo_ref[...], preferred_element_type=F32)
    out_ref[...] = x_ref[...] + _rmsnorm_rows(z, pg_ref[...])


def _final(att_in, xa_in, ob, az, bz, cz, ga, gb, gc, x, post_g, w_pa, w_pb, w_pc, w_out, *, tm, merge, xattn):
    n = x.shape[0]

    def rows(width):
        return pl.BlockSpec((tm, width), lambda i: (i, 0))

    if merge:
        att_specs = [rows(A_W)] * A_GROUPS + [rows(LANES)] * A_GROUPS
    else:
        att_specs = [rows(A_W)]
    if xattn:
        xa_specs = [rows(XA_W), _resident((MEM_LEN, XA_W)), _resident((MEM_LEN, XA_W))]
    else:
        xa_specs = [rows(XA_W)]
    in_specs = (att_specs + xa_specs + [rows(HG_W), rows(A_W), rows(HG_W), rows(XA_W)]
                + [rows(D_MODEL)] * 4 + [_resident((1, D_MODEL))]
                + [_resident((A_W, D_MODEL)), _resident((HG_W, D_MODEL)), _resident((XA_W, D_MODEL)),
                   _resident((D_MODEL, D_MODEL))])
    return pl.pallas_call(
        functools.partial(_final_kernel, tm=tm, merge=merge, xattn=xattn),
        out_shape=jax.ShapeDtypeStruct((n, D_MODEL), F32),
        grid=(n // tm,),
        in_specs=in_specs,
        out_specs=rows(D_MODEL),
        compiler_params=_params(("parallel",), VMEM_LIMIT_BYTES),
        name="final_merge" if merge else "final",
    )(*att_in, *xa_in, ob, az, bz, cz, ga, gb, gc, x, post_g, w_pa, w_pb, w_pc, w_out)


def _head_rows(q_row, head_dim):
    w = q_row.shape[-1]
    sub = lax.broadcasted_iota(jnp.int32, (8, w), 0)
    lane_head = lax.broadcasted_iota(jnp.int32, (8, w), 1) // head_dim
    keep = sub == lane_head
    return jnp.where(keep, jnp.broadcast_to(q_row, (8, w)), 0.0), keep


def _attn_sample_kernel(q0, q1, q2, k0, k1, k2, v0, v1, v2, c0, c1, c2, o_ref):
    q_refs = (q0, q1, q2)
    k_refs = (k0, k1, k2)
    v_refs = (v0, v1, v2)
    c_refs = (c0, c1, c2)
    b = pl.program_id(0)
    outs = []
    lses = []
    keep = None
    for g in range(A_GROUPS):
        dil = A_DILATIONS[g]
        wb = c_refs[g].shape[-1]
        q_row = q_refs[g][pl.ds(b, 1), :]
        k_new = k_refs[g][pl.ds(b, 1), :]
        v_new = v_refs[g][pl.ds(b, 1), :]
        qm, keep = _head_rows(q_row, A_HEAD_DIM)
        kt = c_refs[g][0].astype(BF16)
        vt = c_refs[g][1].astype(BF16)
        s = jnp.dot(qm.astype(BF16), kt, preferred_element_type=F32)
        w_pos = lax.broadcasted_iota(jnp.int32, (8, wb), 1)
        s = jnp.where(w_pos % dil == 0, s, NEG_INF)
        s_new = jnp.sum(qm * k_new, axis=-1, keepdims=True)
        m = jnp.maximum(jnp.max(s, axis=-1, keepdims=True), s_new)
        p = jnp.exp(s - m)
        p_new = jnp.exp(s_new - m)
        l = jnp.sum(p, axis=-1, keepdims=True) + p_new
        o = (pl.dot(p.astype(BF16), vt, trans_b=True) + p_new * v_new) * (1.0 / l)
        outs.append(o)
        lses.append(m + jnp.log(l))
    mx = jnp.maximum(jnp.maximum(lses[0], lses[1]), lses[2])
    ex = [jnp.exp(x - mx) for x in lses]
    inv = 1.0 / (ex[0] + ex[1] + ex[2])
    oa = (ex[0] * outs[0] + ex[1] * outs[1] + ex[2] * outs[2]) * inv
    o_ref[pl.ds(b, 1), :] = jnp.sum(jnp.where(keep, oa, 0.0), axis=0, keepdims=True)


def _hgrn_and_attn_sample_kernel(*refs, th):
    hg_in, at_in = refs[0:5], refs[5:17]
    hg_out, at_out, hg_state = refs[17:19], refs[19], refs[20]
    _hgrn_prompt_kernel(*hg_in, *hg_out, hg_state, th=th)
    _attn_sample_kernel(*at_in, at_out)


def _hgrn_and_attn_sample(q, lf, k, v, gain, qs, ks, vs, caches):
    t = q.shape[0]
    bd = qs[0].shape[0]
    assert t % bd == 0 and (t // bd) % HG_CHUNK == 0
    th = t // bd
    views = []
    c_specs = []
    for g in range(A_GROUPS):
        c = caches[g]
        wb = c.shape[1]
        assert wb == A_NKEY * A_DILATIONS[g], "window buffer must hold exactly 128 dilated keys"
        views.append(jnp.transpose(c, (0, 2, 3, 4, 1)).reshape(bd, 2, A_W, wb))
        c_specs.append(pl.BlockSpec((None, 2, A_W, wb), lambda b: (b, 0, 0, 0)))
    rows = pl.BlockSpec((th, HG_W), lambda i: (i, 0))
    full = pl.BlockSpec((bd, A_W), lambda b: (0, 0))
    state = pl.BlockSpec((HG_HEADS, HG_K, HG_V), lambda i: (0, 0, 0))
    return pl.pallas_call(
        functools.partial(_hgrn_and_attn_sample_kernel, th=th),
        out_shape=[jax.ShapeDtypeStruct((t, HG_W), BF16), jax.ShapeDtypeStruct((HG_HEADS, HG_K, HG_V), F32),
                   jax.ShapeDtypeStruct((bd, A_W), F32)],
        grid=(bd,),
        in_specs=[rows] * 4 + [_resident((1, HG_W))] + [full] * 9 + c_specs,
        out_specs=[rows, state, full],
        scratch_shapes=[pltpu.VMEM((HG_HEADS, HG_V, HG_K), F32)],
        compiler_params=_params(("arbitrary",), VMEM_LIMIT_BYTES),
        name="hgrn_prompt_attn_sample",
    )(q, lf, k, v, gain, *qs, *ks, *vs, *views)


def _xattn_sample_body(step, q_ref, c_ref, o_ref, *, sb):
    stride = 2 * XA_HEADS
    for b in range(sb):
        row = pl.ds(step * sb + b, 1)
        q_row = q_ref[row, :]
        outs = []
        for h in range(XA_HEADS):
            sl = slice(h * XA_HEAD_DIM, (h + 1) * XA_HEAD_DIM)
            kh = c_ref[b, pl.ds(h, MEM_LEN, stride=stride), :]
            vh = c_ref[b, pl.ds(XA_HEADS + h, MEM_LEN, stride=stride), :]
            s = jnp.sum(kh * q_row[:, sl], axis=-1, keepdims=True) * (XA_HEAD_DIM ** -0.5)
            m = jnp.max(s, axis=0, keepdims=True)
            p = jnp.exp(s - m)
            l = jnp.sum(p, axis=0, keepdims=True)
            outs.append(jnp.sum(p * vh, axis=0, keepdims=True) * (1.0 / l))
        o_ref[row, :] = jnp.concatenate(outs, axis=-1)


def _xattn_sample_rider(cq, mem_cache, *, steps):
    bd = cq.shape[0]
    assert bd % steps == 0
    sb = bd // steps
    rows = MEM_LEN * 2 * XA_HEADS
    view = mem_cache.reshape(bd, rows, XA_HEAD_DIM)
    full = pl.BlockSpec((bd, XA_W), lambda i: (0, 0))
    return _Rider(
        body=functools.partial(_xattn_sample_body, sb=sb),
        inputs=(cq, view),
        in_specs=(full, pl.BlockSpec((sb, rows, XA_HEAD_DIM), lambda i: (i, 0, 0))),
        out_shapes=(jax.ShapeDtypeStruct((bd, XA_W), F32),),
        out_specs=(full,))


def _hgrn_sample_body(step, qt_ref, lft_ref, kt_ref, v_ref, s0_ref, gain_ref, o_ref, s_ref, *, sb):
    for b in range(sb):
        row = pl.ds(step * sb + b, 1)
        v_all = v_ref[row, :]
        outs = []
        for h in range(HG_HEADS):
            sl = slice(h * HG_K, (h + 1) * HG_K)
            d_col = jnp.exp(lft_ref[0, sl, b:b + 1])
            k_col = kt_ref[0, sl, b:b + 1]
            q_col = qt_ref[0, sl, b:b + 1]
            s_new = d_col * s0_ref[b, h] + k_col * v_all[:, sl]
            s_ref[b, h] = s_new
            o = jnp.sum(q_col * s_new, axis=0, keepdims=True)
            outs.append(o * lax.rsqrt(jnp.mean(o * o, axis=-1, keepdims=True) + EPS) * gain_ref[:, sl])
        o_ref[row, :] = jnp.concatenate(outs, axis=-1)


def _hgrn_sample_rider(q, lf, kin, v, s0, gain, *, steps):
    bd = q.shape[0]
    assert bd % steps == 0
    sb = bd // steps

    def cols(a):
        return a.reshape(steps, sb, HG_W).transpose(0, 2, 1)

    col_spec = pl.BlockSpec((1, HG_W, sb), lambda i: (i, 0, 0))
    st_spec = pl.BlockSpec((sb, HG_HEADS, HG_K, HG_V), lambda i: (i, 0, 0, 0))
    full = pl.BlockSpec((bd, HG_W), lambda i: (0, 0))
    return _Rider(
        body=functools.partial(_hgrn_sample_body, sb=sb),
        inputs=(cols(q), cols(lf), cols(kin), v, s0, gain),
        in_specs=(col_spec, col_spec, col_spec, full, st_spec, _resident((1, HG_W))),
        out_shapes=(jax.ShapeDtypeStruct((bd, HG_W), F32), jax.ShapeDtypeStruct(s0.shape, F32)),
        out_specs=(full, st_spec))


def kernel(x_prompt, x_sample, mem_prompt, cache_win128_kv, cache_win512_kv, cache_win2048_kv, state_hgrn, cache_mem_kv, norm_pre, norm_post, w_in, hgrn_lb_logits, hgrn_out_norm, mem_norm, w_mem_kv, w_branch_a, w_branch_b, w_branch_c, w_out):
    depth = w_in.shape[0]
    assert depth == 1, "single-layer trunk"
    layer = 0
    bp, t, _ = x_prompt.shape
    bd, s_len, _ = x_sample.shape
    assert bp == 1 and s_len == 1
    caches = (cache_win128_kv[layer], cache_win512_kv[layer], cache_win2048_kv[layer])

    w_att = w_rest = w_in[layer].astype(BF16)
    pre_g = norm_pre[layer].reshape(1, D_MODEL)
    post_g = norm_post[layer].reshape(1, D_MODEL)
    hg_g = hgrn_out_norm[layer].reshape(1, HG_W)
    w_pa, w_pb, w_pc, w_o = (w[layer].astype(BF16) for w in (w_branch_a, w_branch_b, w_branch_c, w_out))
    half = A_HEAD_DIM // 2
    inv = ROPE_THETA ** (-jnp.arange(0, A_HEAD_DIM, 2, dtype=F32) / A_HEAD_DIM)
    invf = jnp.tile(inv, LANES // half).reshape(1, LANES)

    xs = x_sample.reshape(bd, D_MODEL)
    q_s, k_s, v_s, tails_s, _ = _inproj_attn(xs, pre_g, w_att, invf, tm=bd, pos_base=PAST_LEN, pos_step=0,
                                             tail_rows=(bd,) * A_GROUPS, out_dtype=F32, row_dil=(1,) * A_GROUPS)
    (az_s, bq_s, lf_s, kin_s, bi_s, bz_s, cq_s, cz_s, ga_s, gb_s, gc_s), _ = _inproj_rest(
        xs, pre_g, w_rest, hgrn_lb_logits, tm=bd, layer=layer, hg_dtype=F32, cq_dtype=F32)

    xp = x_prompt.reshape(t, D_MODEL)
    tm_p = 512
    steps_p = t // tm_p
    tail_rows = tuple(min(w, t) for w in A_WINDOWS)
    q_p, k_p, v_p, tails_p, _ = _inproj_attn(
        xp, pre_g, w_att, invf, tm=tm_p, pos_base=0, pos_step=1, tail_rows=tail_rows, out_dtype=BF16,
        row_dil=A_DILATIONS)
    (az, bq, lf, kin, bi, bz, cq, cz, ga, gb, gc), (ob_s, s_s, oc_s) = _inproj_rest(
        xp, pre_g, w_rest, hgrn_lb_logits, tm=tm_p, layer=layer, hg_dtype=BF16, cq_dtype=BF16,
        rider=_join_riders(_hgrn_sample_rider(bq_s, lf_s, kin_s, bi_s, state_hgrn[layer], hg_g, steps=steps_p),
                           _xattn_sample_rider(cq_s, cache_mem_kv[layer], steps=steps_p)))
    att = [_attn_prompt(q_p[g], k_p[g], v_p[g], A_DILATIONS[g], rows_per_step=2048) for g in range(A_GROUPS)]
    ob_p, s_p, oa_s = _hgrn_and_attn_sample(bq, lf, kin, bi, hg_g, q_s, k_s, v_s, caches)

    mem_kv = _mem_kv(mem_prompt.reshape(MEM_LEN, D_MODEL), mem_norm[layer].reshape(1, D_MODEL),
                     w_mem_kv[layer].astype(BF16))
    mk = mem_kv[:, :XA_W].astype(BF16)
    mv = mem_kv[:, XA_W:].astype(BF16)
    y_p = _final([a[0] for a in att] + [a[1] for a in att], [cq, mk, mv], ob_p, az, bz, cz, ga, gb, gc,
                 xp, post_g, w_pa, w_pb, w_pc, w_o, tm=512, merge=True, xattn=True)

    y_s = _final([oa_s], [oc_s], ob_s, az_s, bz_s, cz_s, ga_s, gb_s, gc_s,
                 xs, post_g, w_pa, w_pb, w_pc, w_o, tm=bd, merge=False, xattn=False)

    new_win_p = [tails_p[g].reshape(1, 1, tail_rows[g], 2, A_HEADS, A_HEAD_DIM) for g in range(A_GROUPS)]
    new_win_s = [tails_s[g].reshape(1, bd, 1, 2, A_HEADS, A_HEAD_DIM) for g in range(A_GROUPS)]
    return (y_p.reshape(bp, t, D_MODEL), y_s.reshape(bd, 1, D_MODEL),
            new_win_p[0], new_win_p[1], new_win_p[2],
            s_p.reshape(1, 1, HG_HEADS, HG_K, HG_V),
            mem_kv.reshape(1, 1, MEM_LEN, 2, XA_HEADS, XA_HEAD_DIM),
            new_win_s[0], new_win_s[1], new_win_s[2],
            s_s.reshape(1, bd, HG_HEADS, HG_K, HG_V))
```

```python
import functools
from typing import Callable, NamedTuple

import jax
import jax.numpy as jnp
from jax import lax
from jax.experimental import pallas as pl
from jax.experimental.pallas import tpu as pltpu

D_MODEL = 1024
PAST_LEN = 8192
A_WINDOWS = (128, 512, 2048)
A_DILATIONS = (1, 4, 16)
A_GROUPS = 3
A_HEADS = 8
A_HEAD_DIM = 64
A_NKEY = 128
ROPE_THETA = 10000.0
HG_HEADS = 4
HG_K = 128
HG_V = 128
HG_CHUNK = 64
HG_BLOCK = 8
ATTN_BLOCKS_IN_FLIGHT = 4
MEM_LEN = 256
XA_HEADS = 4
XA_HEAD_DIM = 128
EPS = 1e-6

A_W = A_HEADS * A_HEAD_DIM
HG_W = HG_HEADS * HG_K
XA_W = XA_HEADS * XA_HEAD_DIM
N_ATT = 3 * A_GROUPS * A_W
COL_AZ = N_ATT
COL_BQ = COL_AZ + A_W
COL_BF = COL_BQ + HG_W
COL_BI = COL_BF + HG_W
COL_BZ = COL_BI + HG_W
COL_CQ = COL_BZ + HG_W
COL_CZ = COL_CQ + XA_W
COL_GA = COL_CZ + XA_W
COL_GB = COL_GA + D_MODEL
COL_GC = COL_GB + D_MODEL
N_IN = COL_GC + D_MODEL

LANES = 128
VMEM_LIMIT_BYTES = 56 * 1024 * 1024

F32 = jnp.float32
BF16 = jnp.bfloat16
NEG_INF = float("-inf")


def _params(semantics, vmem=None):
    return pltpu.CompilerParams(dimension_semantics=semantics, vmem_limit_bytes=vmem)


def _resident(shape):
    return pl.BlockSpec(shape, lambda *_: (0,) * len(shape), pipeline_mode=pl.Buffered(1))


def _resident_columns(rows, col0, width):
    return pl.BlockSpec((pl.Element(rows), pl.Element(width)), lambda *_: (0, col0), pipeline_mode=pl.Buffered(1))


class _Rider(NamedTuple):
    body: Callable
    inputs: tuple = ()
    in_specs: tuple = ()
    out_shapes: tuple = ()
    out_specs: tuple = ()


_NO_RIDER = _Rider(body=lambda step: None)


def _join_riders(a, b):
    na_in, nb_in, na_out = len(a.inputs), len(b.inputs), len(a.out_shapes)

    def body(step, *refs):
        ins, outs = refs[:na_in + nb_in], refs[na_in + nb_in:]
        a.body(step, *ins[:na_in], *outs[:na_out])
        b.body(step, *ins[na_in:], *outs[na_out:])

    return _Rider(body, a.inputs + b.inputs, a.in_specs + b.in_specs, a.out_shapes + b.out_shapes,
                  a.out_specs + b.out_specs)


def _rmsnorm_rows(x, gain):
    ms = jnp.mean(x * x, axis=-1, keepdims=True)
    return x * lax.rsqrt(ms + EPS) * gain


def _sigmoid(x):
    return 1.0 / (1.0 + jnp.exp(-x))


def _silu(x):
    return x * _sigmoid(x)


def _tile_row_order(tm, d):
    j = lax.broadcasted_iota(jnp.int32, (tm, LANES), 0)
    per = tm // d
    return (j % per) * d + j // per


def _inproj_attn_kernel(x_ref, g_ref, w_ref, invf_ref, *refs, tm, n_tiles, pos_base, pos_step, tails, row_dil,
                        rider_body, n_rider_in, n_rider_out):
    rider_in, refs = refs[:n_rider_in], refs[n_rider_in:]
    q_refs = refs[0:3]
    k_refs = refs[3:6]
    v_refs = refs[6:9]
    t_refs = refs[9:12]
    rider_out = refs[12:12 + n_rider_out]
    cos_tab, sin_tab, xa_ref, xb_ref, xp_ref = refs[12 + n_rider_out:]
    i = pl.program_id(0)
    n_slab = A_W // LANES
    n_xslab = D_MODEL // LANES
    dils = sorted(set(row_dil))

    @pl.when(i == 0)
    def _():
        for t_idx, d in enumerate(dils):
            ang0 = (_tile_row_order(tm, d) * pos_step).astype(F32) * invf_ref[...]
            cos_tab[t_idx] = jnp.cos(ang0)
            sin_tab[t_idx] = jnp.sin(ang0)

    rider_body(i, *rider_in, *rider_out)

    lane = lax.broadcasted_iota(jnp.int32, (tm, LANES), 1)
    first_half = (lane % A_HEAD_DIM) < (A_HEAD_DIM // 2)
    base =(pos_base + i * tm * pos_step).astype(F32) * invf_ref[...]
    cos_b = jnp.cos(base)
    sin_b = jnp.sin(base)

    def trig(d):
        t_idx = dils.index(d)
        c0, s0 = cos_tab[t_idx], sin_tab[t_idx]
        cos = c0 * cos_b - s0 * sin_b
        sin = s0 * cos_b + c0 * sin_b
        return cos, jnp.where(first_half, -sin, sin)

    def rope(acc, cos, sin_signed):
        outs = []
        for c in range(n_slab):
            xc = acc[:, c * LANES:(c + 1) * LANES]
            partner = jnp.where(first_half, pltpu.roll(xc, LANES - 32, 1), pltpu.roll(xc, 32, 1))
            outs.append(xc * cos + partner * sin_signed)
        return jnp.concatenate(outs, axis=-1)

    xn32 = _rmsnorm_rows(x_ref[...], g_ref[...])
    xn = xn32.astype(BF16)

    lhs = {1: xn}
    if max(row_dil) > 1:
        assert set(row_dil) <= {1, 4, 16}
        per4 = tm // 4
        for c in range(n_xslab):
            xa_ref[c] = xn32[:, c * LANES:(c + 1) * LANES]
        for c in range(n_xslab):
            for r in range(4):
                blk4 = xa_ref[c, pl.ds(r, per4, stride=4), :]
                xb_ref[c, r * per4:(r + 1) * per4, :] = blk4
                xp_ref[0, r * per4:(r + 1) * per4, c * LANES:(c + 1) * LANES] = blk4.astype(BF16)
        lhs[4] = xp_ref[0]
        if 16 in row_dil:
            per16 = tm // 16
            for c in range(n_xslab):
                for r1 in range(4):
                    for r2 in range(4):
                        res = 4 * r2 + r1
                        xp_ref[1, res * per16:(res + 1) * per16, c * LANES:(c + 1) * LANES] = (
                            xb_ref[c, pl.ds(r1 * per4 + r2, per16, stride=4), :].astype(BF16))
            lhs[16] = xp_ref[1]

    def store_rows(ref, d, val):
        per = tm // d
        for r in range(d):
            ref[:, r * A_W:(r + 1) * A_W] = val[r * per:(r + 1) * per, :].astype(ref.dtype)

    def tail_block(g, col0, natural_val):
        first_tile, rows = tails[g]
        cond = (i >= first_tile) if rows >= tm else (i == n_tiles - 1)

        @pl.when(cond)
        def _():
            val = natural_val()
            t_refs[g][:, col0:col0 + A_W] = val if rows >= tm else val[tm - rows:, :]

    cos_n, sin_n = trig(1)
    for g in range(A_GROUPS):
        d = row_dil[g]
        cos, sin_s = (cos_n, sin_n) if d == 1 else trig(d)
        wq = w_ref[:, g * A_W:(g + 1) * A_W]
        wk = w_ref[:, (A_GROUPS + g) * A_W:(A_GROUPS + g + 1) * A_W]
        wv = w_ref[:, (2 * A_GROUPS + g) * A_W:(2 * A_GROUPS + g + 1) * A_W]
        q = rope(jnp.dot(lhs[d], wq, preferred_element_type=F32), cos, sin_s) * (A_HEAD_DIM ** -0.5)
        store_rows(q_refs[g], d, q)
        kr = rope(jnp.dot(lhs[d], wk, preferred_element_type=F32), cos, sin_s)
        store_rows(k_refs[g], d, kr)
        v = jnp.dot(lhs[d], wv, preferred_element_type=F32)
        store_rows(v_refs[g], d, v)
        if d == 1:
            tail_block(g, 0, lambda kr=kr: kr)
            tail_block(g, A_W, lambda v=v: v)
        else:
            tail_block(g, 0, lambda wk=wk: rope(jnp.dot(xn, wk, preferred_element_type=F32), cos_n, sin_n))
            tail_block(g, A_W, lambda wv=wv: jnp.dot(xn, wv, preferred_element_type=F32))


def _inproj_attn(x, gain, w_att, invf, *, tm, pos_base, pos_step, tail_rows, out_dtype, row_dil, rider=_NO_RIDER):
    n = x.shape[0]
    n_tiles = n // tm
    assert 1 in row_dil
    n_orders = len(set(row_dil))
    tails = []
    tail_specs = []
    tail_shapes = []
    for g in range(A_GROUPS):
        rows = tail_rows[g]
        first_tile = (n - rows) // tm
        tails.append((first_tile, rows))
        blk = min(tm, rows)
        tail_specs.append(pl.BlockSpec((blk, 2 * A_W), functools.partial(
            lambda i, ft: (jnp.maximum(i - ft, 0), 0), ft=first_tile)))
        tail_shapes.append(jax.ShapeDtypeStruct((rows, 2 * A_W), F32))
    row_specs = [pl.BlockSpec((tm // d, d * A_W), lambda i: (i, 0)) for d in row_dil] * 3
    row_shapes = [jax.ShapeDtypeStruct((n // d, d * A_W), out_dtype) for d in row_dil] * 3
    kernel = functools.partial(_inproj_attn_kernel, tm=tm, n_tiles=n_tiles, pos_base=pos_base,
                               pos_step=pos_step, tails=tuple(tails), row_dil=tuple(row_dil),
                               rider_body=rider.body, n_rider_in=len(rider.inputs), n_rider_out=len(rider.out_shapes))
    outs = pl.pallas_call(
        kernel,
        out_shape=row_shapes + tail_shapes + list(rider.out_shapes),
        grid=(n_tiles,),
        in_specs=[pl.BlockSpec((tm, D_MODEL), lambda i: (i, 0)),
                  _resident((1, D_MODEL)),
                  _resident_columns(D_MODEL, 0, N_ATT),
                  _resident((1, LANES))] + list(rider.in_specs),
        out_specs=row_specs + tail_specs + list(rider.out_specs),
        scratch_shapes=[pltpu.VMEM((n_orders, tm, LANES), F32), pltpu.VMEM((n_orders, tm, LANES), F32),
                        pltpu.VMEM((D_MODEL // LANES, tm, LANES), F32), pltpu.VMEM((D_MODEL // LANES, tm, LANES), F32),
                        pltpu.VMEM((2, tm, D_MODEL), BF16)],
        compiler_params=_params(("arbitrary",), VMEM_LIMIT_BYTES),
        name="inproj_attn",
    )(x, gain, w_att, invf, *rider.inputs)
    return outs[0:3], outs[3:6], outs[6:9], outs[9:12], outs[12:]


def _inproj_rest_kernel(x_ref, g_ref, w_ref, lbl_ref, *refs, layer, rider_body, n_rider_in):
    rider_in, refs = refs[:n_rider_in], refs[n_rider_in:]
    (az_ref, bq_ref, lf_ref, kin_ref, bi_ref, bz_ref, cq_ref, cz_ref, ga_ref, gb_ref, gc_ref) = refs[:11]
    rider_body(pl.program_id(0), *rider_in, *refs[11:])
    xn = _rmsnorm_rows(x_ref[...], g_ref[...]).astype(BF16)

    def proj(col, width):
        c0 = col - N_ATT
        return jnp.dot(xn, w_ref[:, c0:c0 + width], preferred_element_type=F32)

    logits = lbl_ref[...]
    e = jnp.exp(logits - jnp.max(logits, axis=0, keepdims=True))
    lb = jnp.sum(e[0:layer + 1, :], axis=0, keepdims=True) / jnp.sum(e, axis=0, keepdims=True)

    az_ref[...] = _silu(proj(COL_AZ, A_W)).astype(az_ref.dtype)
    bq_ref[...] = proj(COL_BQ, HG_W).astype(bq_ref.dtype)
    bf = proj(COL_BF, HG_W)
    lf_ref[...] = jnp.log(lb + (1.0 - lb) * _sigmoid(bf))
    kin_ref[...] = ((1.0 - lb) * _sigmoid(-bf)).astype(kin_ref.dtype)
    bi_ref[...] = proj(COL_BI, HG_W).astype(bi_ref.dtype)
    bz_ref[...] = _silu(proj(COL_BZ, HG_W)).astype(bz_ref.dtype)
    cq_ref[...] = proj(COL_CQ, XA_W).astype(cq_ref.dtype)
    cz_ref[...] = _silu(proj(COL_CZ, XA_W)).astype(cz_ref.dtype)
    ga_ref[...] = _sigmoid(proj(COL_GA, D_MODEL)).astype(ga_ref.dtype)
    gb_ref[...] = _sigmoid(proj(COL_GB, D_MODEL)).astype(gb_ref.dtype)
    gc_ref[...] = _sigmoid(proj(COL_GC, D_MODEL)).astype(gc_ref.dtype)


def _inproj_rest(x, gain, w_rest, lb_logits, *, tm, layer, hg_dtype, cq_dtype, rider=_NO_RIDER):
    n = x.shape[0]
    n_rest = N_IN - N_ATT
    s512 = pl.BlockSpec((tm, A_W), lambda i: (i, 0))
    s1024 = pl.BlockSpec((tm, D_MODEL), lambda i: (i, 0))

    def sds(width, dt):
        return jax.ShapeDtypeStruct((n, width), dt)

    outs = pl.pallas_call(
        functools.partial(_inproj_rest_kernel, layer=layer, rider_body=rider.body, n_rider_in=len(rider.inputs)),
        out_shape=[sds(A_W, BF16), sds(HG_W, hg_dtype), sds(HG_W, F32), sds(HG_W, hg_dtype), sds(HG_W, hg_dtype),
                   sds(HG_W, BF16), sds(XA_W, cq_dtype), sds(XA_W, BF16),
                   sds(D_MODEL, BF16), sds(D_MODEL, BF16), sds(D_MODEL, BF16)] + list(rider.out_shapes),
        grid=(n // tm,),
        in_specs=[pl.BlockSpec((tm, D_MODEL), lambda i: (i, 0)),
                  _resident((1, D_MODEL)),
                  _resident_columns(D_MODEL, N_ATT, n_rest),
                  _resident(lb_logits.shape)] + list(rider.in_specs),
        out_specs=[s512] * 8 + [s1024] * 3 + list(rider.out_specs),
        compiler_params=_params(("arbitrary",), VMEM_LIMIT_BYTES),
        name="inproj_rest",
    )(x, gain, w_rest, lb_logits, *rider.inputs)
    return outs[:11], outs[11:]


def _attn_prompt_kernel(q_ref, kc_ref, vc_ref, kp_ref, vp_ref, o_ref, st_ref, kbuf, vbuf, o_acc, st_acc,
                        *, tmv, dil, rps):
    n = pl.program_id(0)
    rstep = pl.program_id(1)
    blk = A_NKEY

    qi = lax.broadcasted_iota(jnp.int32, (blk, 2 * blk), 0)
    ki = lax.broadcasted_iota(jnp.int32, (blk, 2 * blk), 1)
    dist = qi + blk - ki
    band = (dist >= 0) & (dist <= A_NKEY)
    bias = jnp.where(band, 0.0, NEG_INF).astype(F32)
    bias_first = jnp.where(band & (ki >= blk), 0.0, NEG_INF).astype(F32)
    lane = lax.broadcasted_iota(jnp.int32, (blk, LANES), 1)
    low_head = lane < A_HEAD_DIM
    head_mask = (jnp.where(low_head, 1.0, 0.0).astype(BF16), jnp.where(low_head, 0.0, 1.0).astype(BF16))

    for rr in range(rps):
        cs = slice(rr * A_W, (rr + 1) * A_W)
        kbuf[rr, 0:blk, :] = kp_ref[:, cs]
        kbuf[rr, blk:, :] = kc_ref[:, cs]
        vbuf[rr, 0:blk, :] = vp_ref[:, cs]
        vbuf[rr, blk:, :] = vc_ref[:, cs]
        res = rstep * rps + rr

        def body(b, carry, cs=cs, res=res, rr=rr):
            r0 = pl.multiple_of(b * blk, blk)
            is_first = jnp.logical_and(n == 0, b == 0)
            bias_b = jnp.where(is_first, bias_first, bias)
            qb = q_ref[pl.ds(r0, blk), cs]
            kb = kbuf[rr, pl.ds(r0, 2 * blk), :]
            vb = vbuf[rr, pl.ds(r0, 2 * blk), :]
            out_rows = pl.ds(r0 * dil + res, blk, stride=dil)
            stats = jnp.zeros((blk, LANES), F32)
            for pair in range(A_HEADS // 2):
                sl = slice(pair * LANES, (pair + 1) * LANES)
                qp = qb[:, sl]
                kp = kb[:, sl]
                vp = vb[:, sl]
                outs = []
                for hh in range(2):
                    qm = qp * head_mask[hh]
                    s = pl.dot(qm, kp, trans_b=True) + bias_b
                    m = jnp.max(s, axis=-1, keepdims=True)
                    p = jnp.exp(s - m)
                    l = jnp.sum(p, axis=-1, keepdims=True)
                    pv = jnp.dot(p.astype(BF16), vp, preferred_element_type=F32)
                    outs.append(pv * (1.0 / l))
                    lse = m + jnp.log(l)
                    stats = jnp.where(lane == 2 * pair + hh, lse, stats)
                o_acc[pair, out_rows, :] = jnp.where(low_head, outs[0], outs[1])
            st_acc[out_rows, :] = stats
            return carry

        lax.fori_loop(0, tmv // blk, body, 0, unroll=min(ATTN_BLOCKS_IN_FLIGHT, tmv // blk))

    @pl.when(rstep == pl.num_programs(1) - 1)
    def _():
        for pair in range(A_HEADS // 2):
            o_ref[:, pair * LANES:(pair + 1) * LANES] = o_acc[pair].astype(o_ref.dtype)
        st_ref[...] = st_acc[...]


def _attn_prompt(qv, kv, vv, dil, *, rows_per_step):
    l = qv.shape[0]
    t = l * dil
    blk = A_NKEY
    tmv = max(rows_per_step // dil, blk)
    rps = min(dil, max(1, ATTN_BLOCKS_IN_FLIGHT * blk // tmv))
    assert t % (tmv * dil) == 0 and dil % rps == 0
    cur = pl.BlockSpec((tmv, rps * A_W), lambda n, r: (n, r))
    prev = pl.BlockSpec((blk, rps * A_W), lambda n, r: (jnp.maximum(n * (tmv // blk) - 1, 0), r))
    rows = tmv * dil
    return pl.pallas_call(
        functools.partial(_attn_prompt_kernel, tmv=tmv, dil=dil, rps=rps),
        out_shape=[jax.ShapeDtypeStruct((t, A_W), BF16), jax.ShapeDtypeStruct((t, LANES), F32)],
        grid=(t // rows, dil // rps),
        in_specs=[cur, cur, cur, prev, prev],
        out_specs=[pl.BlockSpec((rows, A_W), lambda n, r: (n, 0)), pl.BlockSpec((rows, LANES), lambda n, r: (n, 0))],
        scratch_shapes=[pltpu.VMEM((rps, tmv + blk, A_W), BF16), pltpu.VMEM((rps, tmv + blk, A_W), BF16),
                        pltpu.VMEM((A_HEADS // 2, rows, LANES), F32), pltpu.VMEM((rows, LANES), F32)],
        compiler_params=_params(("parallel", "arbitrary"), VMEM_LIMIT_BYTES),
        name=f"attn_prompt_d{dil}",
    )(qv, kv, vv, kv, vv)


def _hgrn_prompt_kernel(q_ref, lf_ref, k_ref, v_ref, gain_ref, o_ref, s_out_ref, st_ref, *, th,
                        co_scheduled=lambda: None):
    i = pl.program_id(0)
    c_rows = HG_CHUNK
    nblk = c_rows // HG_BLOCK
    shape3 = (nblk, HG_BLOCK, HG_K)

    @pl.when(i == 0)
    def _():
        st_ref[...] = jnp.zeros_like(st_ref)

    co_scheduled()

    sub = lax.broadcasted_iota(jnp.int32, shape3, 1)
    a_row = lax.broadcasted_iota(jnp.int32, (c_rows, c_rows), 0) // HG_BLOCK
    a_col = lax.broadcasted_iota(jnp.int32, (c_rows, c_rows), 1) // HG_BLOCK

    def shift(x3, s):
        return pltpu.roll(x3, s, 1)

    def chunk(c, carry):
        r0 = pl.multiple_of(c * c_rows, c_rows)
        for h in range(HG_HEADS):
            sl = slice(h * HG_K, (h + 1) * HG_K)
            q = q_ref[pl.ds(r0, c_rows), sl].astype(F32)
            k = k_ref[pl.ds(r0, c_rows), sl].astype(F32)
            v = v_ref[pl.ds(r0, c_rows), sl].astype(F32)
            q3, k3, v3 = (a.reshape(shape3) for a in (q, k, v))
            cum = lf_ref[pl.ds(r0, c_rows), sl].reshape(shape3)
            s = 1
            while s < HG_BLOCK:
                cum = cum + jnp.where(sub >= s, shift(cum, s), 0.0)
                s *= 2
            anchors = [jnp.zeros((1, 1, HG_K), F32)]
            for j in range(nblk):
                anchors.append(anchors[j] + cum[j:j + 1, HG_BLOCK - 1:HG_BLOCK, :])
            b3 = cum + jnp.concatenate(anchors[:nblk], axis=0)
            b = b3.reshape(c_rows, HG_K)
            b_last = anchors[nblk].reshape(1, HG_K)

            o3 = jnp.sum(q3 * k3, axis=-1, keepdims=True) * v3
            for d in range(1, HG_BLOCK):
                dec = jnp.exp(jnp.where(sub >= d, cum - shift(cum, d), NEG_INF))
                a = jnp.sum(q3 * shift(k3, d) * dec, axis=-1, keepdims=True)
                o3 = o3 + a * shift(v3, d)
            o = o3.reshape(c_rows, HG_K)

            b_end = jnp.concatenate(anchors[1:], axis=0)
            k_hat = (k3 * jnp.exp(b_end - b3)).reshape(c_rows, HG_K).astype(BF16)
            q_ver = [(q3 * jnp.exp(jnp.minimum(b3 - anchors[j + 1], 0.0))).reshape(c_rows, HG_K).astype(BF16)
                     for j in range(nblk - 1)]
            r = pl.dot(jnp.concatenate(q_ver, axis=0), k_hat, trans_b=True)
            att = jnp.zeros((c_rows, c_rows), F32)
            for j in range(nblk - 1):
                att = jnp.where(a_col == j, r[j * c_rows:(j + 1) * c_rows, :], att)
            att = jnp.where(a_row > a_col, att, 0.0)
            vb = v.astype(BF16)
            o = o + jnp.dot(att.astype(BF16), vb, preferred_element_type=F32)

            st = st_ref[h]
            o = o + pl.dot((q * jnp.exp(b)).astype(BF16), st.astype(BF16), trans_b=True)
            k_end = (k * jnp.exp(b_last - b)).astype(BF16)
            st_ref[h] = st * jnp.exp(b_last) + pl.dot(vb, k_end, trans_a=True)

            on = o * lax.rsqrt(jnp.mean(o * o, axis=-1, keepdims=True) + EPS) * gain_ref[:, sl]
            o_ref[pl.ds(r0, c_rows), sl] = on.astype(o_ref.dtype)
        return carry

    n_chunks = th // c_rows
    lax.fori_loop(0, n_chunks, chunk, 0, unroll=n_chunks <= 2)

    @pl.when(i == pl.num_programs(0) - 1)
    def _():
        for h in range(HG_HEADS):
            s_out_ref[h] = st_ref[h].T


def _mem_kv_kernel(m_ref, g_ref, w_ref, o_ref):
    xn = _rmsnorm_rows(m_ref[...], g_ref[...]).astype(BF16)
    o_ref[...] = jnp.dot(xn, w_ref[...], preferred_element_type=F32)


def _mem_kv(mem, gain, w):
    m = mem.shape[0]
    return pl.pallas_call(
        _mem_kv_kernel,
        out_shape=jax.ShapeDtypeStruct((m, 2 * XA_W), F32),
        name="mem_kv",
    )(mem, gain, w)


def _softmax_rows(s):
    m = jnp.max(s, axis=-1, keepdims=True)
    p = jnp.exp(s - m)
    return p, jnp.sum(p, axis=-1, keepdims=True)


def _final_kernel(*refs, tm, merge, xattn):
    refs = list(refs)
    if merge:
        o_refs = [refs.pop(0) for _ in range(A_GROUPS)]
        st_refs = [refs.pop(0) for _ in range(A_GROUPS)]
    else:
        oa_ref = refs.pop(0)
    if xattn:
        cq_ref, mk_ref, mv_ref = (refs.pop(0) for _ in range(3))
    else:
        oc_ref = refs.pop(0)
    (ob_ref, az_ref, bz_ref, cz_ref, ga_ref, gb_ref, gc_ref, x_ref, pg_ref,
     wa_ref, wb_ref, wc_ref, wo_ref, out_ref) = refs

    if merge:
        lse = [r[...] for r in st_refs]
        mx = jnp.maximum(jnp.maximum(lse[0], lse[1]), lse[2])
        ex = [jnp.exp(x - mx) for x in lse]
        inv = 1.0 / (ex[0] + ex[1] + ex[2])
        e_row = lax.broadcasted_iota(jnp.int32, (LANES, A_W), 0)
        e_col = lax.broadcasted_iota(jnp.int32, (LANES, A_W), 1) // A_HEAD_DIM
        spread = jnp.where(e_row == e_col, 1.0, 0.0).astype(BF16)
        oa = jnp.zeros((tm, A_W), F32)
        for g in range(A_GROUPS):
            alpha = ex[g] * inv
            hi = alpha.astype(BF16)
            lo = (alpha - hi.astype(F32)).astype(BF16)
            w = (jnp.dot(hi, spread, preferred_element_type=F32)
                 + jnp.dot(lo, spread, preferred_element_type=F32))
            oa = oa + w * o_refs[g][...].astype(F32)
    else:
        oa = oa_ref[...].astype(F32)
    ya = (oa * az_ref[...].astype(F32)).astype(BF16)

    if xattn:
        cq = cq_ref[...]
        parts = []
        for h in range(XA_HEADS):
            sl = slice(h * XA_HEAD_DIM, (h + 1) * XA_HEAD_DIM)
            s = pl.dot(cq[:, sl], mk_ref[:, sl], trans_b=True) * (XA_HEAD_DIM ** -0.5)
            p, l = _softmax_rows(s)
            parts.append(jnp.dot(p.astype(BF16), mv_ref[:, sl], preferred_element_type=F32) * (1.0 / l))
        oc = jnp.concatenate(parts, axis=-1)
    else:
        oc = oc_ref[...].astype(F32)
    yc = (oc * cz_ref[...].astype(F32)).astype(BF16)

    yb = (ob_ref[...].astype(F32) * bz_ref[...].astype(F32)).astype(BF16)

    merged = (ga_ref[...].astype(F32) * jnp.dot(ya, wa_ref[...], preferred_element_type=F32)
              + gb_ref[...].astype(F32) * jnp.dot(yb, wb_ref[...], preferred_element_type=F32)
              + gc_ref[...].astype(F32) * jnp.dot(yc, wc_ref[...], preferred_element_type=F32))
    z = jnp.dot(merged.astype(BF16), wo_ref[...], preferred_element_type=F32)
    out_ref[...] = x_ref[...] + _rmsnorm_rows(z, pg_ref[...])


def _final(att_in, xa_in, ob, az, bz, cz, ga, gb, gc, x, post_g, w_pa, w_pb, w_pc, w_out, *, tm, merge, xattn):
    n = x.shape[0]

    def rows(width):
        return pl.BlockSpec((tm, width), lambda i: (i, 0))

    if merge:
        att_specs = [rows(A_W)] * A_GROUPS + [rows(LANES)] * A_GROUPS
    else:
        att_specs = [rows(A_W)]
    if xattn:
        xa_specs = [rows(XA_W), _resident((MEM_LEN, XA_W)), _resident((MEM_LEN, XA_W))]
    else:
        xa_specs = [rows(XA_W)]
    in_specs = (att_specs + xa_specs + [rows(HG_W), rows(A_W), rows(HG_W), rows(XA_W)]
                + [rows(D_MODEL)] * 4 + [_resident((1, D_MODEL))]
                + [_resident((A_W, D_MODEL)), _resident((HG_W, D_MODEL)), _resident((XA_W, D_MODEL)),
                   _resident((D_MODEL, D_MODEL))])
    return pl.pallas_call(
        functools.partial(_final_kernel, tm=tm, merge=merge, xattn=xattn),
        out_shape=jax.ShapeDtypeStruct((n, D_MODEL), F32),
        grid=(n // tm,),
        in_specs=in_specs,
        out_specs=rows(D_MODEL),
        compiler_params=_params(("parallel",), VMEM_LIMIT_BYTES),
        name="final_merge" if merge else "final",
    )(*att_in, *xa_in, ob, az, bz, cz, ga, gb, gc, x, post_g, w_pa, w_pb, w_pc, w_out)


def _head_rows(q_row, head_dim):
    w = q_row.shape[-1]
    sub = lax.broadcasted_iota(jnp.int32, (8, w), 0)
    lane_head = lax.broadcasted_iota(jnp.int32, (8, w), 1) // head_dim
    keep = sub == lane_head
    return jnp.where(keep, jnp.broadcast_to(q_row, (8, w)), 0.0), keep


def _attn_sample_kernel(q0, q1, q2, k0, k1, k2, v0, v1, v2, c0, c1, c2, o_ref):
    q_refs = (q0, q1, q2)
    k_refs = (k0, k1, k2)
    v_refs = (v0, v1, v2)
    c_refs = (c0, c1, c2)
    b = pl.program_id(0)
    outs = []
    lses = []
    keep = None
    for g in range(A_GROUPS):
        dil = A_DILATIONS[g]
        wb = c_refs[g].shape[-1]
        q_row = q_refs[g][pl.ds(b, 1), :]
        k_new = k_refs[g][pl.ds(b, 1), :]
        v_new = v_refs[g][pl.ds(b, 1), :]
        qm, keep = _head_rows(q_row, A_HEAD_DIM)
        kt = c_refs[g][0].astype(BF16)
        vt = c_refs[g][1].astype(BF16)
        s = jnp.dot(qm.astype(BF16), kt, preferred_element_type=F32)
        w_pos = lax.broadcasted_iota(jnp.int32, (8, wb), 1)
        s = jnp.where(w_pos % dil == 0, s, NEG_INF)
        s_new = jnp.sum(qm * k_new, axis=-1, keepdims=True)
        m = jnp.maximum(jnp.max(s, axis=-1, keepdims=True), s_new)
        p = jnp.exp(s - m)
        p_new = jnp.exp(s_new - m)
        l = jnp.sum(p, axis=-1, keepdims=True) + p_new
        o = (pl.dot(p.astype(BF16), vt, trans_b=True) + p_new * v_new) * (1.0 / l)
        outs.append(o)
        lses.append(m + jnp.log(l))
    mx = jnp.maximum(jnp.maximum(lses[0], lses[1]), lses[2])
    ex = [jnp.exp(x - mx) for x in lses]
    inv = 1.0 / (ex[0] + ex[1] + ex[2])
    oa = (ex[0] * outs[0] + ex[1] * outs[1] + ex[2] * outs[2]) * inv
    o_ref[pl.ds(b, 1), :] = jnp.sum(jnp.where(keep, oa, 0.0), axis=0, keepdims=True)


def _hgrn_and_attn_sample_kernel(*refs, th):
    hg_in, at_in = refs[0:5], refs[5:17]
    hg_out, at_out, hg_state = refs[17:19], refs[19], refs[20]
    _hgrn_prompt_kernel(*hg_in, *hg_out, hg_state, th=th,
                        co_scheduled=lambda: _attn_sample_kernel(*at_in, at_out))


def _hgrn_and_attn_sample(q, lf, k, v, gain, qs, ks, vs, caches):
    t = q.shape[0]
    bd = qs[0].shape[0]
    assert t % bd == 0 and (t // bd) % HG_CHUNK == 0
    th = t // bd
    views = []
    c_specs = []
    for g in range(A_GROUPS):
        c = caches[g]
        wb = c.shape[1]
        assert wb == A_NKEY * A_DILATIONS[g], "window buffer must hold exactly 128 dilated keys"
        views.append(jnp.transpose(c, (0, 2, 3, 4, 1)).reshape(bd, 2, A_W, wb))
        c_specs.append(pl.BlockSpec((None, 2, A_W, wb), lambda b: (b, 0, 0, 0)))
    rows = pl.BlockSpec((th, HG_W), lambda i: (i, 0))
    full = pl.BlockSpec((bd, A_W), lambda b: (0, 0))
    state = pl.BlockSpec((HG_HEADS, HG_K, HG_V), lambda i: (0, 0, 0))
    return pl.pallas_call(
        functools.partial(_hgrn_and_attn_sample_kernel, th=th),
        out_shape=[jax.ShapeDtypeStruct((t, HG_W), BF16), jax.ShapeDtypeStruct((HG_HEADS, HG_K, HG_V), F32),
                   jax.ShapeDtypeStruct((bd, A_W), F32)],
        grid=(bd,),
        in_specs=[rows] * 4 + [_resident((1, HG_W))] + [full] * 9 + c_specs,
        out_specs=[rows, state, full],
        scratch_shapes=[pltpu.VMEM((HG_HEADS, HG_V, HG_K), F32)],
        compiler_params=_params(("arbitrary",), VMEM_LIMIT_BYTES),
        name="hgrn_prompt_attn_sample",
    )(q, lf, k, v, gain, *qs, *ks, *vs, *views)


def _xattn_sample_body(step, q_ref, c_ref, o_ref, *, sb):
    stride = 2 * XA_HEADS
    for b in range(sb):
        row = pl.ds(step * sb + b, 1)
        q_row = q_ref[row, :]
        outs = []
        for h in range(XA_HEADS):
            sl = slice(h * XA_HEAD_DIM, (h + 1) * XA_HEAD_DIM)
            kh = c_ref[b, pl.ds(h, MEM_LEN, stride=stride), :]
            vh = c_ref[b, pl.ds(XA_HEADS + h, MEM_LEN, stride=stride), :]
            s = jnp.sum(kh * q_row[:, sl], axis=-1, keepdims=True) * (XA_HEAD_DIM ** -0.5)
            m = jnp.max(s, axis=0, keepdims=True)
            p = jnp.exp(s - m)
            l = jnp.sum(p, axis=0, keepdims=True)
            outs.append(jnp.sum(p * vh, axis=0, keepdims=True) * (1.0 / l))
        o_ref[row, :] = jnp.concatenate(outs, axis=-1)


def _xattn_sample_rider(cq, mem_cache, *, steps):
    bd = cq.shape[0]
    assert bd % steps == 0
    sb = bd // steps
    rows = MEM_LEN * 2 * XA_HEADS
    view = mem_cache.reshape(bd, rows, XA_HEAD_DIM)
    full = pl.BlockSpec((bd, XA_W), lambda i: (0, 0))
    return _Rider(
        body=functools.partial(_xattn_sample_body, sb=sb),
        inputs=(cq, view),
        in_specs=(full, pl.BlockSpec((sb, rows, XA_HEAD_DIM), lambda i: (i, 0, 0))),
        out_shapes=(jax.ShapeDtypeStruct((bd, XA_W), F32),),
        out_specs=(full,))


def _hgrn_sample_body(step, qt_ref, lft_ref, kt_ref, v_ref, s0_ref, gain_ref, o_ref, s_ref, *, sb):
    for b in range(sb):
        row = pl.ds(step * sb + b, 1)
        v_all = v_ref[row, :]
        outs = []
        for h in range(HG_HEADS):
            sl = slice(h * HG_K, (h + 1) * HG_K)
            d_col = jnp.exp(lft_ref[0, sl, b:b + 1])
            k_col = kt_ref[0, sl, b:b + 1]
            q_col = qt_ref[0, sl, b:b + 1]
            s_new = d_col * s0_ref[b, h] + k_col * v_all[:, sl]
            s_ref[b, h] = s_new
            o = jnp.sum(q_col * s_new, axis=0, keepdims=True)
            outs.append(o * lax.rsqrt(jnp.mean(o * o, axis=-1, keepdims=True) + EPS) * gain_ref[:, sl])
        o_ref[row, :] = jnp.concatenate(outs, axis=-1)


def _hgrn_sample_rider(q, lf, kin, v, s0, gain, *, steps):
    bd = q.shape[0]
    assert bd % steps == 0
    sb = bd // steps

    def cols(a):
        return a.reshape(steps, sb, HG_W).transpose(0, 2, 1)

    col_spec = pl.BlockSpec((1, HG_W, sb), lambda i: (i, 0, 0))
    st_spec = pl.BlockSpec((sb, HG_HEADS, HG_K, HG_V), lambda i: (i, 0, 0, 0))
    full = pl.BlockSpec((bd, HG_W), lambda i: (0, 0))
    return _Rider(
        body=functools.partial(_hgrn_sample_body, sb=sb),
        inputs=(cols(q), cols(lf), cols(kin), v, s0, gain),
        in_specs=(col_spec, col_spec, col_spec, full, st_spec, _resident((1, HG_W))),
        out_shapes=(jax.ShapeDtypeStruct((bd, HG_W), F32), jax.ShapeDtypeStruct(s0.shape, F32)),
        out_specs=(full, st_spec))


def kernel(x_prompt, x_sample, mem_prompt, cache_win128_kv, cache_win512_kv, cache_win2048_kv, state_hgrn, cache_mem_kv, norm_pre, norm_post, w_in, hgrn_lb_logits, hgrn_out_norm, mem_norm, w_mem_kv, w_branch_a, w_branch_b, w_branch_c, w_out):
    depth = w_in.shape[0]
    assert depth == 1, "single-layer trunk"
    layer = 0
    bp, t, _ = x_prompt.shape
    bd, s_len, _ = x_sample.shape
    assert bp == 1 and s_len == 1
    caches = (cache_win128_kv[layer], cache_win512_kv[layer], cache_win2048_kv[layer])

    w_att = w_rest = w_in[layer].astype(BF16)
    pre_g = norm_pre[layer].reshape(1, D_MODEL)
    post_g = norm_post[layer].reshape(1, D_MODEL)
    hg_g = hgrn_out_norm[layer].reshape(1, HG_W)
    w_pa, w_pb, w_pc, w_o = (w[layer].astype(BF16) for w in (w_branch_a, w_branch_b, w_branch_c, w_out))
    half = A_HEAD_DIM // 2
    inv = ROPE_THETA ** (-jnp.arange(0, A_HEAD_DIM, 2, dtype=F32) / A_HEAD_DIM)
    invf = jnp.tile(inv, LANES // half).reshape(1, LANES)

    xs = x_sample.reshape(bd, D_MODEL)
    q_s, k_s, v_s, tails_s, _ = _inproj_attn(xs, pre_g, w_att, invf, tm=bd, pos_base=PAST_LEN, pos_step=0,
                                             tail_rows=(bd,) * A_GROUPS, out_dtype=F32, row_dil=(1,) * A_GROUPS)
    (az_s, bq_s, lf_s, kin_s, bi_s, bz_s, cq_s, cz_s, ga_s, gb_s, gc_s), _ = _inproj_rest(
        xs, pre_g, w_rest, hgrn_lb_logits, tm=bd, layer=layer, hg_dtype=F32, cq_dtype=F32)

    xp = x_prompt.reshape(t, D_MODEL)
    tm_p = 512
    steps_p = t // tm_p
    tail_rows = tuple(min(w, t) for w in A_WINDOWS)
    q_p, k_p, v_p, tails_p, _ = _inproj_attn(
        xp, pre_g, w_att, invf, tm=tm_p, pos_base=0, pos_step=1, tail_rows=tail_rows, out_dtype=BF16,
        row_dil=A_DILATIONS)
    (az, bq, lf, kin, bi, bz, cq, cz, ga, gb, gc), (ob_s, s_s, oc_s) = _inproj_rest(
        xp, pre_g, w_rest, hgrn_lb_logits, tm=tm_p, layer=layer, hg_dtype=BF16, cq_dtype=BF16,
        rider=_join_riders(_hgrn_sample_rider(bq_s, lf_s, kin_s, bi_s, state_hgrn[layer], hg_g, steps=steps_p),
                           _xattn_sample_rider(cq_s, cache_mem_kv[layer], steps=steps_p)))
    att = [_attn_prompt(q_p[g], k_p[g], v_p[g], A_DILATIONS[g], rows_per_step=2048) for g in range(A_GROUPS)]
    ob_p, s_p, oa_s = _hgrn_and_attn_sample(bq, lf, kin, bi, hg_g, q_s, k_s, v_s, caches)

    mem_kv = _mem_kv(mem_prompt.reshape(MEM_LEN, D_MODEL), mem_norm[layer].reshape(1, D_MODEL),
                     w_mem_kv[layer].astype(BF16))
    mk = mem_kv[:, :XA_W].astype(BF16)
    mv = mem_kv[:, XA_W:].astype(BF16)
    y_p = _final([a[0] for a in att] + [a[1] for a in att], [cq, mk, mv], ob_p, az, bz, cz, ga, gb, gc,
                 xp, post_g, w_pa, w_pb, w_pc, w_o, tm=512, merge=True, xattn=True)

    y_s = _final([oa_s], [oc_s], ob_s, az_s, bz_s, cz_s, ga_s, gb_s, gc_s,
                 xs, post_g, w_pa, w_pb, w_pc, w_o, tm=bd, merge=False, xattn=False)

    new_win_p = [tails_p[g].reshape(1, 1, tail_rows[g], 2, A_HEADS, A_HEAD_DIM) for g in range(A_GROUPS)]
    new_win_s = [tails_s[g].reshape(1, bd, 1, 2, A_HEADS, A_HEAD_DIM) for g in range(A_GROUPS)]
    return (y_p.reshape(bp, t, D_MODEL), y_s.reshape(bd, 1, D_MODEL),
            new_win_p[0], new_win_p[1], new_win_p[2],
            s_p.reshape(1, 1, HG_HEADS, HG_K, HG_V),
            mem_kv.reshape(1, 1, MEM_LEN, 2, XA_HEADS, XA_HEAD_DIM),
            new_win_s[0], new_win_s[1], new_win_s[2],
            s_s.reshape(1, bd, HG_HEADS, HG_K, HG_V))
```

```python
import functools
from typing import Callable, NamedTuple

import jax
import jax.numpy as jnp
from jax import lax
from jax.experimental import pallas as pl
from jax.experimental.pallas import tpu as pltpu

D_MODEL = 1024
PAST_LEN = 8192
A_WINDOWS = (128, 512, 2048)
A_DILATIONS = (1, 4, 16)
A_GROUPS = 3
A_HEADS = 8
A_HEAD_DIM = 64
A_NKEY = 128
ROPE_THETA = 10000.0
HG_HEADS = 4
HG_K = 128
HG_V = 128
HG_CHUNK = 64
HG_BLOCK = 8
ATTN_BLOCKS_IN_FLIGHT = 4
MEM_LEN = 256
XA_HEADS = 4
XA_HEAD_DIM = 128
EPS = 1e-6

A_W = A_HEADS * A_HEAD_DIM
HG_W = HG_HEADS * HG_K
XA_W = XA_HEADS * XA_HEAD_DIM
N_ATT = 3 * A_GROUPS * A_W
COL_AZ = N_ATT
COL_BQ = COL_AZ + A_W
COL_BF = COL_BQ + HG_W
COL_BI = COL_BF + HG_W
COL_BZ = COL_BI + HG_W
COL_CQ = COL_BZ + HG_W
COL_CZ = COL_CQ + XA_W
COL_GA = COL_CZ + XA_W
COL_GB = COL_GA + D_MODEL
COL_GC = COL_GB + D_MODEL
N_IN = COL_GC + D_MODEL

LANES = 128
VMEM_LIMIT_BYTES = 56 * 1024 * 1024

F32 = jnp.float32
BF16 = jnp.bfloat16
NEG_INF = float("-inf")


def _params(semantics, vmem=None):
    return pltpu.CompilerParams(dimension_semantics=semantics, vmem_limit_bytes=vmem)


def _resident(shape):
    return pl.BlockSpec(shape, lambda *_: (0,) * len(shape), pipeline_mode=pl.Buffered(1))


def _resident_columns(rows, col0, width):
    return pl.BlockSpec((pl.Element(rows), pl.Element(width)), lambda *_: (0, col0), pipeline_mode=pl.Buffered(1))


class _Rider(NamedTuple):
    body: Callable
    inputs: tuple = ()
    in_specs: tuple = ()
    out_shapes: tuple = ()
    out_specs: tuple = ()


_NO_RIDER = _Rider(body=lambda step: None)


def _join_riders(a, b):
    na_in, nb_in, na_out = len(a.inputs), len(b.inputs), len(a.out_shapes)

    def body(step, *refs):
        ins, outs = refs[:na_in + nb_in], refs[na_in + nb_in:]
        a.body(step, *ins[:na_in], *outs[:na_out])
        b.body(step, *ins[na_in:], *outs[na_out:])

    return _Rider(body, a.inputs + b.inputs, a.in_specs + b.in_specs, a.out_shapes + b.out_shapes,
                  a.out_specs + b.out_specs)


def _rmsnorm_rows(x, gain):
    ms = jnp.mean(x * x, axis=-1, keepdims=True)
    return x * lax.rsqrt(ms + EPS) * gain


def _sigmoid(x):
    return 1.0 / (1.0 + jnp.exp(-x))


def _silu(x):
    return x * _sigmoid(x)


def _tile_row_order(tm, d):
    j = lax.broadcasted_iota(jnp.int32, (tm, LANES), 0)
    per = tm // d
    return (j % per) * d + j // per


def _inproj_attn_kernel(x_ref, g_ref, w_ref, invf_ref, *refs, tm, n_tiles, pos_base, pos_step, tails, row_dil,
                        rider_body, n_rider_in, n_rider_out):
    rider_in, refs = refs[:n_rider_in], refs[n_rider_in:]
    q_refs = refs[0:3]
    k_refs = refs[3:6]
    v_refs = refs[6:9]
    t_refs = refs[9:12]
    rider_out = refs[12:12 + n_rider_out]
    cos_tab, sin_tab, xa_ref, xb_ref, xp_ref = refs[12 + n_rider_out:]
    i = pl.program_id(0)
    n_slab = A_W // LANES
    n_xslab = D_MODEL // LANES
    dils = sorted(set(row_dil))

    @pl.when(i == 0)
    def _():
        for t_idx, d in enumerate(dils):
            ang0 = (_tile_row_order(tm, d) * pos_step).astype(F32) * invf_ref[...]
            cos_tab[t_idx] = jnp.cos(ang0)
            sin_tab[t_idx] = jnp.sin(ang0)

    rider_body(i, *rider_in, *rider_out)

    lane = lax.broadcasted_iota(jnp.int32, (tm, LANES), 1)
    first_half = (lane % A_HEAD_DIM) < (A_HEAD_DIM // 2)
    base =(pos_base + i * tm * pos_step).astype(F32) * invf_ref[...]
    cos_b = jnp.cos(base)
    sin_b = jnp.sin(base)

    def trig(d):
        t_idx = dils.index(d)
        c0, s0 = cos_tab[t_idx], sin_tab[t_idx]
        cos = c0 * cos_b - s0 * sin_b
        sin = s0 * cos_b + c0 * sin_b
        return cos, jnp.where(first_half, -sin, sin)

    def rope(acc, cos, sin_signed):
        outs = []
        for c in range(n_slab):
            xc = acc[:, c * LANES:(c + 1) * LANES]
            partner = jnp.where(first_half, pltpu.roll(xc, LANES - 32, 1), pltpu.roll(xc, 32, 1))
            outs.append(xc * cos + partner * sin_signed)
        return jnp.concatenate(outs, axis=-1)

    xn32 = _rmsnorm_rows(x_ref[...], g_ref[...])
    xn = xn32.astype(BF16)

    lhs = {1: xn}
    if max(row_dil) > 1:
        assert set(row_dil) <= {1, 4, 16}
        per4 = tm // 4
        for c in range(n_xslab):
            xa_ref[c] = xn32[:, c * LANES:(c + 1) * LANES]
        for c in range(n_xslab):
            for r in range(4):
                blk4 = xa_ref[c, pl.ds(r, per4, stride=4), :]
                xb_ref[c, r * per4:(r + 1) * per4, :] = blk4
                xp_ref[0, r * per4:(r + 1) * per4, c * LANES:(c + 1) * LANES] = blk4.astype(BF16)
        lhs[4] = xp_ref[0]
        if 16 in row_dil:
            per16 = tm // 16
            for c in range(n_xslab):
                for r1 in range(4):
                    for r2 in range(4):
                        res = 4 * r2 + r1
                        xp_ref[1, res * per16:(res + 1) * per16, c * LANES:(c + 1) * LANES] = (
                            xb_ref[c, pl.ds(r1 * per4 + r2, per16, stride=4), :].astype(BF16))
            lhs[16] = xp_ref[1]

    def store_rows(ref, d, val):
        per = tm // d
        for r in range(d):
            ref[:, r * A_W:(r + 1) * A_W] = val[r * per:(r + 1) * per, :].astype(ref.dtype)

    def tail_block(g, col0, natural_val):
        first_tile, rows = tails[g]
        cond = (i >= first_tile) if rows >= tm else (i == n_tiles - 1)

        @pl.when(cond)
        def _():
            val = natural_val()
            t_refs[g][:, col0:col0 + A_W] = val if rows >= tm else val[tm - rows:, :]

    cos_n, sin_n = trig(1)
    for g in range(A_GROUPS):
        d = row_dil[g]
        cos, sin_s = (cos_n, sin_n) if d == 1 else trig(d)
        wq = w_ref[:, g * A_W:(g + 1) * A_W]
        wk = w_ref[:, (A_GROUPS + g) * A_W:(A_GROUPS + g + 1) * A_W]
        wv = w_ref[:, (2 * A_GROUPS + g) * A_W:(2 * A_GROUPS + g + 1) * A_W]
        q = rope(jnp.dot(lhs[d], wq, preferred_element_type=F32), cos, sin_s) * (A_HEAD_DIM ** -0.5)
        store_rows(q_refs[g], d, q)
        kr = rope(jnp.dot(lhs[d], wk, preferred_element_type=F32), cos, sin_s)
        store_rows(k_refs[g], d, kr)
        v = jnp.dot(lhs[d], wv, preferred_element_type=F32)
        store_rows(v_refs[g], d, v)
        if d == 1:
            tail_block(g, 0, lambda kr=kr: kr)
            tail_block(g, A_W, lambda v=v: v)
        else:
            tail_block(g, 0, lambda wk=wk: rope(jnp.dot(xn, wk, preferred_element_type=F32), cos_n, sin_n))
            tail_block(g, A_W, lambda wv=wv: jnp.dot(xn, wv, preferred_element_type=F32))


def _inproj_attn(x, gain, w_att, invf, *, tm, pos_base, pos_step, tail_rows, out_dtype, row_dil, rider=_NO_RIDER):
    n = x.shape[0]
    n_tiles = n // tm
    assert 1 in row_dil
    n_orders = len(set(row_dil))
    tails = []
    tail_specs = []
    tail_shapes = []
    for g in range(A_GROUPS):
        rows = tail_rows[g]
        first_tile = (n - rows) // tm
        tails.append((first_tile, rows))
        blk = min(tm, rows)
        tail_specs.append(pl.BlockSpec((blk, 2 * A_W), functools.partial(
            lambda i, ft: (jnp.maximum(i - ft, 0), 0), ft=first_tile)))
        tail_shapes.append(jax.ShapeDtypeStruct((rows, 2 * A_W), F32))
    row_specs = [pl.BlockSpec((tm // d, d * A_W), lambda i: (i, 0)) for d in row_dil] * 3
    row_shapes = [jax.ShapeDtypeStruct((n // d, d * A_W), out_dtype) for d in row_dil] * 3
    kernel = functools.partial(_inproj_attn_kernel, tm=tm, n_tiles=n_tiles, pos_base=pos_base,
                               pos_step=pos_step, tails=tuple(tails), row_dil=tuple(row_dil),
                               rider_body=rider.body, n_rider_in=len(rider.inputs), n_rider_out=len(rider.out_shapes))
    outs = pl.pallas_call(
        kernel,
        out_shape=row_shapes + tail_shapes + list(rider.out_shapes),
        grid=(n_tiles,),
        in_specs=[pl.BlockSpec((tm, D_MODEL), lambda i: (i, 0)),
                  _resident((1, D_MODEL)),
                  _resident_columns(D_MODEL, 0, N_ATT),
                  _resident((1, LANES))] + list(rider.in_specs),
        out_specs=row_specs + tail_specs + list(rider.out_specs),
        scratch_shapes=[pltpu.VMEM((n_orders, tm, LANES), F32), pltpu.VMEM((n_orders, tm, LANES), F32),
                        pltpu.VMEM((D_MODEL // LANES, tm, LANES), F32), pltpu.VMEM((D_MODEL // LANES, tm, LANES), F32),
                        pltpu.VMEM((2, tm, D_MODEL), BF16)],
        compiler_params=_params(("arbitrary",), VMEM_LIMIT_BYTES),
        name="inproj_attn",
    )(x, gain, w_att, invf, *rider.inputs)
    return outs[0:3], outs[3:6], outs[6:9], outs[9:12], outs[12:]


def _inproj_rest_kernel(x_ref, g_ref, w_ref, lbl_ref, *refs, layer, rider_body, n_rider_in):
    rider_in, refs = refs[:n_rider_in], refs[n_rider_in:]
    (az_ref, bq_ref, lf_ref, kin_ref, bi_ref, bz_ref, cq_ref, cz_ref, ga_ref, gb_ref, gc_ref) = refs[:11]
    rider_body(pl.program_id(0), *rider_in, *refs[11:])
    xn = _rmsnorm_rows(x_ref[...], g_ref[...]).astype(BF16)

    def proj(col, width):
        c0 = col - N_ATT
        return jnp.dot(xn, w_ref[:, c0:c0 + width], preferred_element_type=F32)

    logits = lbl_ref[...]
    e = jnp.exp(logits - jnp.max(logits, axis=0, keepdims=True))
    lb = jnp.sum(e[0:layer + 1, :], axis=0, keepdims=True) / jnp.sum(e, axis=0, keepdims=True)

    az_ref[...] = _silu(proj(COL_AZ, A_W)).astype(az_ref.dtype)
    bq_ref[...] = proj(COL_BQ, HG_W).astype(bq_ref.dtype)
    bf = proj(COL_BF, HG_W)
    lf_ref[...] = jnp.log(lb + (1.0 - lb) * _sigmoid(bf))
    kin_ref[...] = ((1.0 - lb) * _sigmoid(-bf)).astype(kin_ref.dtype)
    bi_ref[...] = proj(COL_BI, HG_W).astype(bi_ref.dtype)
    bz_ref[...] = _silu(proj(COL_BZ, HG_W)).astype(bz_ref.dtype)
    cq_ref[...] = proj(COL_CQ, XA_W).astype(cq_ref.dtype)
    cz_ref[...] = _silu(proj(COL_CZ, XA_W)).astype(cz_ref.dtype)
    ga_ref[...] = _sigmoid(proj(COL_GA, D_MODEL)).astype(ga_ref.dtype)
    gb_ref[...] = _sigmoid(proj(COL_GB, D_MODEL)).astype(gb_ref.dtype)
    gc_ref[...] = _sigmoid(proj(COL_GC, D_MODEL)).astype(gc_ref.dtype)


def _inproj_rest(x, gain, w_rest, lb_logits, *, tm, layer, hg_dtype, cq_dtype, rider=_NO_RIDER):
    n = x.shape[0]
    n_rest = N_IN - N_ATT
    s512 = pl.BlockSpec((tm, A_W), lambda i: (i, 0))
    s1024 = pl.BlockSpec((tm, D_MODEL), lambda i: (i, 0))

    def sds(width, dt):
        return jax.ShapeDtypeStruct((n, width), dt)

    outs = pl.pallas_call(
        functools.partial(_inproj_rest_kernel, layer=layer, rider_body=rider.body, n_rider_in=len(rider.inputs)),
        out_shape=[sds(A_W, BF16), sds(HG_W, hg_dtype), sds(HG_W, F32), sds(HG_W, hg_dtype), sds(HG_W, hg_dtype),
                   sds(HG_W, BF16), sds(XA_W, cq_dtype), sds(XA_W, BF16),
                   sds(D_MODEL, BF16), sds(D_MODEL, BF16), sds(D_MODEL, BF16)] + list(rider.out_shapes),
        grid=(n // tm,),
        in_specs=[pl.BlockSpec((tm, D_MODEL), lambda i: (i, 0)),
                  _resident((1, D_MODEL)),
                  _resident_columns(D_MODEL, N_ATT, n_rest),
                  _resident(lb_logits.shape)] + list(rider.in_specs),
        out_specs=[s512] * 8 + [s1024] * 3 + list(rider.out_specs),
        compiler_params=_params(("arbitrary",), VMEM_LIMIT_BYTES),
        name="inproj_rest",
    )(x, gain, w_rest, lb_logits, *rider.inputs)
    return outs[:11], outs[11:]


def _attn_prompt_kernel(q_ref, kc_ref, vc_ref, kp_ref, vp_ref, *refs, tmv, dil, rps, rider_body, n_rider_in):
    rider_in, refs = refs[:n_rider_in], refs[n_rider_in:]
    o_ref, st_ref = refs[:2]
    rider_out = refs[2:len(refs) - 4]
    kbuf, vbuf, o_acc, st_acc = refs[len(refs) - 4:]
    n = pl.program_id(0)
    rstep = pl.program_id(1)
    blk = A_NKEY
    rider_body(n * pl.num_programs(1) + rstep, *rider_in, *rider_out)

    qi = lax.broadcasted_iota(jnp.int32, (blk, 2 * blk), 0)
    ki = lax.broadcasted_iota(jnp.int32, (blk, 2 * blk), 1)
    dist = qi + blk - ki
    band = (dist >= 0) & (dist <= A_NKEY)
    bias = jnp.where(band, 0.0, NEG_INF).astype(F32)
    bias_first = jnp.where(band & (ki >= blk), 0.0, NEG_INF).astype(F32)
    lane = lax.broadcasted_iota(jnp.int32, (blk, LANES), 1)
    low_head = lane < A_HEAD_DIM
    head_mask = (jnp.where(low_head, 1.0, 0.0).astype(BF16), jnp.where(low_head, 0.0, 1.0).astype(BF16))

    for rr in range(rps):
        cs = slice(rr * A_W, (rr + 1) * A_W)
        kbuf[rr, 0:blk, :] = kp_ref[:, cs]
        kbuf[rr, blk:, :] = kc_ref[:, cs]
        vbuf[rr, 0:blk, :] = vp_ref[:, cs]
        vbuf[rr, blk:, :] = vc_ref[:, cs]
        res = rstep * rps + rr

        def body(b, carry, cs=cs, res=res, rr=rr):
            r0 = pl.multiple_of(b * blk, blk)
            is_first = jnp.logical_and(n == 0, b == 0)
            bias_b = jnp.where(is_first, bias_first, bias)
            qb = q_ref[pl.ds(r0, blk), cs]
            kb = kbuf[rr, pl.ds(r0, 2 * blk), :]
            vb = vbuf[rr, pl.ds(r0, 2 * blk), :]
            out_rows = pl.ds(r0 * dil + res, blk, stride=dil)
            stats = jnp.zeros((blk, LANES), F32)
            for pair in range(A_HEADS // 2):
                sl = slice(pair * LANES, (pair + 1) * LANES)
                qp = qb[:, sl]
                kp = kb[:, sl]
                vp = vb[:, sl]
                outs = []
                for hh in range(2):
                    qm = qp * head_mask[hh]
                    s = pl.dot(qm, kp, trans_b=True) + bias_b
                    m = jnp.max(s, axis=-1, keepdims=True)
                    p = jnp.exp(s - m)
                    l = jnp.sum(p, axis=-1, keepdims=True)
                    pv = jnp.dot(p.astype(BF16), vp, preferred_element_type=F32)
                    outs.append(pv * (1.0 / l))
                    lse = m + jnp.log(l)
                    stats = jnp.where(lane == 2 * pair + hh, lse, stats)
                o_acc[pair, out_rows, :] = jnp.where(low_head, outs[0], outs[1])
            st_acc[out_rows, :] = stats
            return carry

        lax.fori_loop(0, tmv // blk, body, 0, unroll=min(ATTN_BLOCKS_IN_FLIGHT, tmv // blk))

    @pl.when(rstep == pl.num_programs(1) - 1)
    def _():
        for pair in range(A_HEADS // 2):
            o_ref[:, pair * LANES:(pair + 1) * LANES] = o_acc[pair].astype(o_ref.dtype)
        st_ref[...] = st_acc[...]


def _attn_prompt(qv, kv, vv, dil, *, rows_per_step, make_rider=None):
    l = qv.shape[0]
    t = l * dil
    blk = A_NKEY
    tmv = max(rows_per_step // dil, blk)
    rps = min(dil, max(1, ATTN_BLOCKS_IN_FLIGHT * blk // tmv))
    assert t % (tmv * dil) == 0 and dil % rps == 0
    cur = pl.BlockSpec((tmv, rps * A_W), lambda n, r: (n, r))
    prev = pl.BlockSpec((blk, rps * A_W), lambda n, r: (jnp.maximum(n * (tmv // blk) - 1, 0), r))
    rows = tmv * dil
    grid = (t // rows, dil // rps)
    rider = _NO_RIDER if make_rider is None else make_rider(grid[0] * grid[1], lambda n, r: n * grid[1] + r)
    outs = pl.pallas_call(
        functools.partial(_attn_prompt_kernel, tmv=tmv, dil=dil, rps=rps, rider_body=rider.body,
                          n_rider_in=len(rider.inputs)),
        out_shape=[jax.ShapeDtypeStruct((t, A_W), BF16), jax.ShapeDtypeStruct((t, LANES), F32)]
        + list(rider.out_shapes),
        grid=grid,
        in_specs=[cur, cur, cur, prev, prev] + list(rider.in_specs),
        out_specs=[pl.BlockSpec((rows, A_W), lambda n, r: (n, 0)), pl.BlockSpec((rows, LANES), lambda n, r: (n, 0))]
        + list(rider.out_specs),
        scratch_shapes=[pltpu.VMEM((rps, tmv + blk, A_W), BF16), pltpu.VMEM((rps, tmv + blk, A_W), BF16),
                        pltpu.VMEM((A_HEADS // 2, rows, LANES), F32), pltpu.VMEM((rows, LANES), F32)],
        compiler_params=_params(("arbitrary", "arbitrary"), VMEM_LIMIT_BYTES),
        name=f"attn_prompt_d{dil}",
    )(qv, kv, vv, kv, vv, *rider.inputs)
    return outs[0], outs[1], outs[2:]


def _hgrn_prompt_kernel(q_ref, lf_ref, k_ref, v_ref, gain_ref, o_ref, s_out_ref, st_ref, *, th,
                        co_scheduled=lambda: None):
    i = pl.program_id(0)
    c_rows = HG_CHUNK
    nblk = c_rows // HG_BLOCK
    shape3 = (nblk, HG_BLOCK, HG_K)

    @pl.when(i == 0)
    def _():
        st_ref[...] = jnp.zeros_like(st_ref)

    co_scheduled()

    sub = lax.broadcasted_iota(jnp.int32, shape3, 1)
    a_row = lax.broadcasted_iota(jnp.int32, (c_rows, c_rows), 0) // HG_BLOCK
    a_col = lax.broadcasted_iota(jnp.int32, (c_rows, c_rows), 1) // HG_BLOCK

    def shift(x3, s):
        return pltpu.roll(x3, s, 1)

    def chunk(c, carry):
        r0 = pl.multiple_of(c * c_rows, c_rows)
        for h in range(HG_HEADS):
            sl = slice(h * HG_K, (h + 1) * HG_K)
            q = q_ref[pl.ds(r0, c_rows), sl].astype(F32)
            k = k_ref[pl.ds(r0, c_rows), sl].astype(F32)
            v = v_ref[pl.ds(r0, c_rows), sl].astype(F32)
            q3, k3, v3 = (a.reshape(shape3) for a in (q, k, v))
            cum = lf_ref[pl.ds(r0, c_rows), sl].reshape(shape3)
            s = 1
            while s < HG_BLOCK:
                cum = cum + jnp.where(sub >= s, shift(cum, s), 0.0)
                s *= 2
            anchors = [jnp.zeros((1, 1, HG_K), F32)]
            for j in range(nblk):
                anchors.append(anchors[j] + cum[j:j + 1, HG_BLOCK - 1:HG_BLOCK, :])
            b3 = cum + jnp.concatenate(anchors[:nblk], axis=0)
            b = b3.reshape(c_rows, HG_K)
            b_last = anchors[nblk].reshape(1, HG_K)

            o3 = jnp.sum(q3 * k3, axis=-1, keepdims=True) * v3
            for d in range(1, HG_BLOCK):
                dec = jnp.exp(jnp.where(sub >= d, cum - shift(cum, d), NEG_INF))
                a = jnp.sum(q3 * shift(k3, d) * dec, axis=-1, keepdims=True)
                o3 = o3 + a * shift(v3, d)
            o = o3.reshape(c_rows, HG_K)

            b_end = jnp.concatenate(anchors[1:], axis=0)
            k_hat = (k3 * jnp.exp(b_end - b3)).reshape(c_rows, HG_K).astype(BF16)
            q_ver = [(q3 * jnp.exp(jnp.minimum(b3 - anchors[j + 1], 0.0))).reshape(c_rows, HG_K).astype(BF16)
                     for j in range(nblk - 1)]
            r = pl.dot(jnp.concatenate(q_ver, axis=0), k_hat, trans_b=True)
            att = jnp.zeros((c_rows, c_rows), F32)
            for j in range(nblk - 1):
                att = jnp.where(a_col == j, r[j * c_rows:(j + 1) * c_rows, :], att)
            att = jnp.where(a_row > a_col, att, 0.0)
            vb = v.astype(BF16)
            o = o + jnp.dot(att.astype(BF16), vb, preferred_element_type=F32)

            st = st_ref[h]
            o = o + pl.dot((q * jnp.exp(b)).astype(BF16), st.astype(BF16), trans_b=True)
            k_end = (k * jnp.exp(b_last - b)).astype(BF16)
            st_ref[h] = st * jnp.exp(b_last) + pl.dot(vb, k_end, trans_a=True)

            on = o * lax.rsqrt(jnp.mean(o * o, axis=-1, keepdims=True) + EPS) * gain_ref[:, sl]
            o_ref[pl.ds(r0, c_rows), sl] = on.astype(o_ref.dtype)
        return carry

    n_chunks = th // c_rows
    lax.fori_loop(0, n_chunks, chunk, 0, unroll=n_chunks <= 2)

    @pl.when(i == pl.num_programs(0) - 1)
    def _():
        for h in range(HG_HEADS):
            s_out_ref[h] = st_ref[h].T


def _mem_kv_kernel(m_ref, g_ref, w_ref, o_ref):
    xn = _rmsnorm_rows(m_ref[...], g_ref[...]).astype(BF16)
    o_ref[...] = jnp.dot(xn, w_ref[...], preferred_element_type=F32)


def _mem_kv(mem, gain, w):
    m = mem.shape[0]
    return pl.pallas_call(
        _mem_kv_kernel,
        out_shape=jax.ShapeDtypeStruct((m, 2 * XA_W), F32),
        name="mem_kv",
    )(mem, gain, w)


def _softmax_rows(s):
    m = jnp.max(s, axis=-1, keepdims=True)
    p = jnp.exp(s - m)
    return p, jnp.sum(p, axis=-1, keepdims=True)


def _final_kernel(*refs, tm, merge, xattn):
    refs = list(refs)
    if merge:
        o_refs = [refs.pop(0) for _ in range(A_GROUPS)]
        st_refs = [refs.pop(0) for _ in range(A_GROUPS)]
    else:
        oa_ref = refs.pop(0)
    if xattn:
        cq_ref, mk_ref, mv_ref = (refs.pop(0) for _ in range(3))
    else:
        oc_ref = refs.pop(0)
    (ob_ref, az_ref, bz_ref, cz_ref, ga_ref, gb_ref, gc_ref, x_ref, pg_ref,
     wa_ref, wb_ref, wc_ref, wo_ref, out_ref) = refs

    if merge:
        lse = [r[...] for r in st_refs]
        mx = jnp.maximum(jnp.maximum(lse[0], lse[1]), lse[2])
        ex = [jnp.exp(x - mx) for x in lse]
        inv = 1.0 / (ex[0] + ex[1] + ex[2])
        e_row = lax.broadcasted_iota(jnp.int32, (LANES, A_W), 0)
        e_col = lax.broadcasted_iota(jnp.int32, (LANES, A_W), 1) // A_HEAD_DIM
        spread = jnp.where(e_row == e_col, 1.0, 0.0).astype(BF16)
        oa = jnp.zeros((tm, A_W), F32)
        for g in range(A_GROUPS):
            alpha = ex[g] * inv
            hi = alpha.astype(BF16)
            lo = (alpha - hi.astype(F32)).astype(BF16)
            w = (jnp.dot(hi, spread, preferred_element_type=F32)
                 + jnp.dot(lo, spread, preferred_element_type=F32))
            oa = oa + w * o_refs[g][...].astype(F32)
    else:
        oa = oa_ref[...].astype(F32)
    ya = (oa * az_ref[...].astype(F32)).astype(BF16)

    if xattn:
        cq = cq_ref[...]
        parts = []
        for h in range(XA_HEADS):
            sl = slice(h * XA_HEAD_DIM, (h + 1) * XA_HEAD_DIM)
            s = pl.dot(cq[:, sl], mk_ref[:, sl], trans_b=True) * (XA_HEAD_DIM ** -0.5)
            p, l = _softmax_rows(s)
            parts.append(jnp.dot(p.astype(BF16), mv_ref[:, sl], preferred_element_type=F32) * (1.0 / l))
        oc = jnp.concatenate(parts, axis=-1)
    else:
        oc = oc_ref[...].astype(F32)
    yc = (oc * cz_ref[...].astype(F32)).astype(BF16)

    yb = (ob_ref[...].astype(F32) * bz_ref[...].astype(F32)).astype(BF16)

    merged = (ga_ref[...].astype(F32) * jnp.dot(ya, wa_ref[...], preferred_element_type=F32)
              + gb_ref[...].astype(F32) * jnp.dot(yb, wb_ref[...], preferred_element_type=F32)
              + gc_ref[...].astype(F32) * jnp.dot(yc, wc_ref[...], preferred_element_type=F32))
    z = jnp.dot(merged.astype(BF16), wo_ref[...], preferred_element_type=F32)
    out_ref[...] = x_ref[...] + _rmsnorm_rows(z, pg_ref[...])


def _final(att_in, xa_in, ob, az, bz, cz, ga, gb, gc, x, post_g, w_pa, w_pb, w_pc, w_out, *, tm, merge, xattn):
    n = x.shape[0]

    def rows(width):
        return pl.BlockSpec((tm, width), lambda i: (i, 0))

    if merge:
        att_specs = [rows(A_W)] * A_GROUPS + [rows(LANES)] * A_GROUPS
    else:
        att_specs = [rows(A_W)]
    if xattn:
        xa_specs = [rows(XA_W), _resident((MEM_LEN, XA_W)), _resident((MEM_LEN, XA_W))]
    else:
        xa_specs = [rows(XA_W)]
    in_specs = (att_specs + xa_specs + [rows(HG_W), rows(A_W), rows(HG_W), rows(XA_W)]
                + [rows(D_MODEL)] * 4 + [_resident((1, D_MODEL))]
                + [_resident((A_W, D_MODEL)), _resident((HG_W, D_MODEL)), _resident((XA_W, D_MODEL)),
                   _resident((D_MODEL, D_MODEL))])
    return pl.pallas_call(
        functools.partial(_final_kernel, tm=tm, merge=merge, xattn=xattn),
        out_shape=jax.ShapeDtypeStruct((n, D_MODEL), F32),
        grid=(n // tm,),
        in_specs=in_specs,
        out_specs=rows(D_MODEL),
        compiler_params=_params(("parallel",), VMEM_LIMIT_BYTES),
        name="final_merge" if merge else "final",
    )(*att_in, *xa_in, ob, az, bz, cz, ga, gb, gc, x, post_g, w_pa, w_pb, w_pc, w_out)


def _head_rows(q_row, head_dim):
    w = q_row.shape[-1]
    sub = lax.broadcasted_iota(jnp.int32, (8, w), 0)
    lane_head = lax.broadcasted_iota(jnp.int32, (8, w), 1) // head_dim
    keep = sub == lane_head
    return jnp.where(keep, jnp.broadcast_to(q_row, (8, w)), 0.0), keep


def _window_cache_view(cache, g):
    bd, wb = cache.shape[0], cache.shape[1]
    assert wb == A_NKEY * A_DILATIONS[g], "window buffer must hold exactly 128 dilated keys"
    return jnp.transpose(cache, (0, 2, 3, 4, 1)).reshape(bd, 2, A_W, wb)


def _attn_sample_group(dil, q_row, k_new, v_new, c_ref):
    wb = c_ref.shape[-1]
    qm, _ = _head_rows(q_row, A_HEAD_DIM)
    kt = c_ref[0].astype(BF16)
    vt = c_ref[1].astype(BF16)
    s = jnp.dot(qm.astype(BF16), kt, preferred_element_type=F32)
    w_pos = lax.broadcasted_iota(jnp.int32, (8, wb), 1)
    s = jnp.where(w_pos % dil == 0, s, NEG_INF)
    s_new = jnp.sum(qm * k_new, axis=-1, keepdims=True)
    m = jnp.maximum(jnp.max(s, axis=-1, keepdims=True), s_new)
    p = jnp.exp(s - m)
    p_new = jnp.exp(s_new - m)
    l = jnp.sum(p, axis=-1, keepdims=True) + p_new
    o = (pl.dot(p.astype(BF16), vt, trans_b=True) + p_new * v_new) * (1.0 / l)
    return o, m + jnp.log(l)


def _attn_sample_partial_body(step, q_ref, k_ref, v_ref, c_ref, o_ref, l_ref, *, dil, sb):
    for b in range(sb):
        row = pl.ds(step * sb + b, 1)
        o, lse = _attn_sample_group(dil, q_ref[row, :], k_ref[row, :], v_ref[row, :], c_ref.at[b])
        o_ref[b] = o
        l_ref[b] = jnp.broadcast_to(lse, (8, LANES))


def _attn_sample_partial_rider(g, q, k, v, cache, *, steps, step_of):
    bd = q.shape[0]
    assert bd % steps == 0
    sb = bd // steps
    view = _window_cache_view(cache, g)
    full = pl.BlockSpec((bd, A_W), lambda *idx: (0, 0))
    return _Rider(
        body=functools.partial(_attn_sample_partial_body, dil=A_DILATIONS[g], sb=sb),
        inputs=(q, k, v, view),
        in_specs=(full, full, full,
                  pl.BlockSpec((sb,) + view.shape[1:], lambda *idx: (step_of(*idx), 0, 0, 0))),
        out_shapes=(jax.ShapeDtypeStruct((bd, 8, A_W), F32), jax.ShapeDtypeStruct((bd, 8, LANES), F32)),
        out_specs=(pl.BlockSpec((sb, 8, A_W), lambda *idx: (step_of(*idx), 0, 0)),
                   pl.BlockSpec((sb, 8, LANES), lambda *idx: (step_of(*idx), 0, 0))))


def _attn_sample_kernel(q_ref, k_ref, v_ref, c_ref, po0, pl0, po1, pl1, o_ref):
    b = pl.program_id(0)
    row = pl.ds(b, 1)
    o2, lse2 = _attn_sample_group(A_DILATIONS[2], q_ref[row, :], k_ref[row, :], v_ref[row, :], c_ref)
    outs = [po0[...], po1[...], o2]
    lses = [pl0[:, 0:1], pl1[:, 0:1], lse2]
    mx = jnp.maximum(jnp.maximum(lses[0], lses[1]), lses[2])
    ex = [jnp.exp(x - mx) for x in lses]
    inv = 1.0 / (ex[0] + ex[1] + ex[2])
    oa = (ex[0] * outs[0] + ex[1] * outs[1] + ex[2] * outs[2]) * inv
    _, keep = _head_rows(jnp.zeros((1, A_W), F32), A_HEAD_DIM)
    o_ref[row, :] = jnp.sum(jnp.where(keep, oa, 0.0), axis=0, keepdims=True)


def _hgrn_and_attn_sample_kernel(*refs, th):
    hg_in, at_in = refs[0:5], refs[5:13]
    hg_out, at_out, hg_state = refs[13:15], refs[15], refs[16]
    _hgrn_prompt_kernel(*hg_in, *hg_out, hg_state, th=th,
                        co_scheduled=lambda: _attn_sample_kernel(*at_in, at_out))


def _hgrn_and_attn_sample(q, lf, k, v, gain, q_s, k_s, v_s, cache, partials):
    t = q.shape[0]
    bd = q_s.shape[0]
    assert t % bd == 0 and (t // bd) % HG_CHUNK == 0
    th = t // bd
    view = _window_cache_view(cache, A_GROUPS - 1)
    rows = pl.BlockSpec((th, HG_W), lambda i: (i, 0))
    full = pl.BlockSpec((bd, A_W), lambda b: (0, 0))
    state = pl.BlockSpec((HG_HEADS, HG_K, HG_V), lambda i: (0, 0, 0))
    part_specs = [pl.BlockSpec((None, 8, A_W), lambda b: (b, 0, 0)), pl.BlockSpec((None, 8, LANES), lambda b: (b, 0, 0))] * 2
    return pl.pallas_call(
        functools.partial(_hgrn_and_attn_sample_kernel, th=th),
        out_shape=[jax.ShapeDtypeStruct((t, HG_W), BF16), jax.ShapeDtypeStruct((HG_HEADS, HG_K, HG_V), F32),
                   jax.ShapeDtypeStruct((bd, A_W), F32)],
        grid=(bd,),
        in_specs=([rows] * 4 + [_resident((1, HG_W))] + [full] * 3
                  + [pl.BlockSpec((None,) + view.shape[1:], lambda b: (b, 0, 0, 0))] + part_specs),
        out_specs=[rows, state, full],
        scratch_shapes=[pltpu.VMEM((HG_HEADS, HG_V, HG_K), F32)],
        compiler_params=_params(("arbitrary",), VMEM_LIMIT_BYTES),
        name="hgrn_prompt_attn_sample",
    )(q, lf, k, v, gain, q_s, k_s, v_s, view, *partials)


def _xattn_sample_body(step, q_ref, c_ref, o_ref, *, sb):
    stride = 2 * XA_HEADS
    for b in range(sb):
        row = pl.ds(step * sb + b, 1)
        q_row = q_ref[row, :]
        outs = []
        for h in range(XA_HEADS):
            sl = slice(h * XA_HEAD_DIM, (h + 1) * XA_HEAD_DIM)
            kh = c_ref[b, pl.ds(h, MEM_LEN, stride=stride), :]
            vh = c_ref[b, pl.ds(XA_HEADS + h, MEM_LEN, stride=stride), :]
            s = jnp.sum(kh * q_row[:, sl], axis=-1, keepdims=True) * (XA_HEAD_DIM ** -0.5)
            m = jnp.max(s, axis=0, keepdims=True)
            p = jnp.exp(s - m)
            l = jnp.sum(p, axis=0, keepdims=True)
            outs.append(jnp.sum(p * vh, axis=0, keepdims=True) * (1.0 / l))
        o_ref[row, :] = jnp.concatenate(outs, axis=-1)


def _xattn_sample_rider(cq, mem_cache, *, steps):
    bd = cq.shape[0]
    assert bd % steps == 0
    sb = bd // steps
    rows = MEM_LEN * 2 * XA_HEADS
    view = mem_cache.reshape(bd, rows, XA_HEAD_DIM)
    full = pl.BlockSpec((bd, XA_W), lambda i: (0, 0))
    return _Rider(
        body=functools.partial(_xattn_sample_body, sb=sb),
        inputs=(cq, view),
        in_specs=(full, pl.BlockSpec((sb, rows, XA_HEAD_DIM), lambda i: (i, 0, 0))),
        out_shapes=(jax.ShapeDtypeStruct((bd, XA_W), F32),),
        out_specs=(full,))


def _hgrn_sample_body(step, qt_ref, lft_ref, kt_ref, v_ref, s0_ref, gain_ref, o_ref, s_ref, *, sb):
    for b in range(sb):
        row = pl.ds(step * sb + b, 1)
        v_all = v_ref[row, :]
        outs = []
        for h in range(HG_HEADS):
            sl = slice(h * HG_K, (h + 1) * HG_K)
            d_col = jnp.exp(lft_ref[0, sl, b:b + 1])
            k_col = kt_ref[0, sl, b:b + 1]
            q_col = qt_ref[0, sl, b:b + 1]
            s_new = d_col * s0_ref[b, h] + k_col * v_all[:, sl]
            s_ref[b, h] = s_new
            o = jnp.sum(q_col * s_new, axis=0, keepdims=True)
            outs.append(o * lax.rsqrt(jnp.mean(o * o, axis=-1, keepdims=True) + EPS) * gain_ref[:, sl])
        o_ref[row, :] = jnp.concatenate(outs, axis=-1)


def _hgrn_sample_rider(q, lf, kin, v, s0, gain, *, steps):
    bd = q.shape[0]
    assert bd % steps == 0
    sb = bd // steps

    def cols(a):
        return a.reshape(steps, sb, HG_W).transpose(0, 2, 1)

    col_spec = pl.BlockSpec((1, HG_W, sb), lambda i: (i, 0, 0))
    st_spec = pl.BlockSpec((sb, HG_HEADS, HG_K, HG_V), lambda i: (i, 0, 0, 0))
    full = pl.BlockSpec((bd, HG_W), lambda i: (0, 0))
    return _Rider(
        body=functools.partial(_hgrn_sample_body, sb=sb),
        inputs=(cols(q), cols(lf), cols(kin), v, s0, gain),
        in_specs=(col_spec, col_spec, col_spec, full, st_spec, _resident((1, HG_W))),
        out_shapes=(jax.ShapeDtypeStruct((bd, HG_W), F32), jax.ShapeDtypeStruct(s0.shape, F32)),
        out_specs=(full, st_spec))


def kernel(x_prompt, x_sample, mem_prompt, cache_win128_kv, cache_win512_kv, cache_win2048_kv, state_hgrn, cache_mem_kv, norm_pre, norm_post, w_in, hgrn_lb_logits, hgrn_out_norm, mem_norm, w_mem_kv, w_branch_a, w_branch_b, w_branch_c, w_out):
    depth = w_in.shape[0]
    assert depth == 1, "single-layer trunk"
    layer = 0
    bp, t, _ = x_prompt.shape
    bd, s_len, _ = x_sample.shape
    assert bp == 1 and s_len == 1
    caches = (cache_win128_kv[layer], cache_win512_kv[layer], cache_win2048_kv[layer])

    w_att = w_rest = w_in[layer].astype(BF16)
    pre_g = norm_pre[layer].reshape(1, D_MODEL)
    post_g = norm_post[layer].reshape(1, D_MODEL)
    hg_g = hgrn_out_norm[layer].reshape(1, HG_W)
    w_pa, w_pb, w_pc, w_o = (w[layer].astype(BF16) for w in (w_branch_a, w_branch_b, w_branch_c, w_out))
    half = A_HEAD_DIM // 2
    inv = ROPE_THETA ** (-jnp.arange(0, A_HEAD_DIM, 2, dtype=F32) / A_HEAD_DIM)
    invf = jnp.tile(inv, LANES // half).reshape(1, LANES)

    xs = x_sample.reshape(bd, D_MODEL)
    q_s, k_s, v_s, tails_s, _ = _inproj_attn(xs, pre_g, w_att, invf, tm=bd, pos_base=PAST_LEN, pos_step=0,
                                             tail_rows=(bd,) * A_GROUPS, out_dtype=F32, row_dil=(1,) * A_GROUPS)
    (az_s, bq_s, lf_s, kin_s, bi_s, bz_s, cq_s, cz_s, ga_s, gb_s, gc_s), _ = _inproj_rest(
        xs, pre_g, w_rest, hgrn_lb_logits, tm=bd, layer=layer, hg_dtype=F32, cq_dtype=F32)

    xp = x_prompt.reshape(t, D_MODEL)
    tm_p = 512
    steps_p = t // tm_p
    tail_rows = tuple(min(w, t) for w in A_WINDOWS)
    q_p, k_p, v_p, tails_p, _ = _inproj_attn(
        xp, pre_g, w_att, invf, tm=tm_p, pos_base=0, pos_step=1, tail_rows=tail_rows, out_dtype=BF16,
        row_dil=A_DILATIONS)
    (az, bq, lf, kin, bi, bz, cq, cz, ga, gb, gc), (ob_s, s_s, oc_s) = _inproj_rest(
        xp, pre_g, w_rest, hgrn_lb_logits, tm=tm_p, layer=layer, hg_dtype=BF16, cq_dtype=BF16,
        rider=_join_riders(_hgrn_sample_rider(bq_s, lf_s, kin_s, bi_s, state_hgrn[layer], hg_g, steps=steps_p),
                           _xattn_sample_rider(cq_s, cache_mem_kv[layer], steps=steps_p)))
    ride_on = {1: 0, 2: 1}
    att = []
    partials = {}
    for g in range(A_GROUPS):
        make_rider = None
        if g in ride_on:
            gs = ride_on[g]
            make_rider = functools.partial(_attn_sample_partial_rider, gs, q_s[gs], k_s[gs], v_s[gs], caches[gs])
            make_rider = (lambda steps, step_of, f=make_rider: f(steps=steps, step_of=step_of))
        o_g, st_g, extra = _attn_prompt(q_p[g], k_p[g], v_p[g], A_DILATIONS[g], rows_per_step=2048,
                                        make_rider=make_rider)
        att.append((o_g, st_g))
        if g in ride_on:
            partials[ride_on[g]] = extra
    g_wide = A_GROUPS - 1
    ob_p, s_p, oa_s = _hgrn_and_attn_sample(bq, lf, kin, bi, hg_g, q_s[g_wide], k_s[g_wide], v_s[g_wide],
                                            caches[g_wide], list(partials[0]) + list(partials[1]))

    mem_kv = _mem_kv(mem_prompt.reshape(MEM_LEN, D_MODEL), mem_norm[layer].reshape(1, D_MODEL),
                     w_mem_kv[layer].astype(BF16))
    mk = mem_kv[:, :XA_W].astype(BF16)
    mv = mem_kv[:, XA_W:].astype(BF16)
    y_p = _final([a[0] for a in att] + [a[1] for a in att], [cq, mk, mv], ob_p, az, bz, cz, ga, gb, gc,
                 xp, post_g, w_pa, w_pb, w_pc, w_o, tm=512, merge=True, xattn=True)

    y_s = _final([oa_s], [oc_s], ob_s, az_s, bz_s, cz_s, ga_s, gb_s, gc_s,
                 xs, post_g, w_pa, w_pb, w_pc, w_o, tm=bd, merge=False, xattn=False)

    new_win_p = [tails_p[g].reshape(1, 1, tail_rows[g], 2, A_HEADS, A_HEAD_DIM) for g in range(A_GROUPS)]
    new_win_s = [tails_s[g].reshape(1, bd, 1, 2, A_HEADS, A_HEAD_DIM) for g in range(A_GROUPS)]
    return (y_p.reshape(bp, t, D_MODEL), y_s.reshape(bd, 1, D_MODEL),
            new_win_p[0], new_win_p[1], new_win_p[2],
            s_p.reshape(1, 1, HG_HEADS, HG_K, HG_V),
            mem_kv.reshape(1, 1, MEM_LEN, 2, XA_HEADS, XA_HEAD_DIM),
            new_win_s[0], new_win_s[1], new_win_s[2],
            s_s.reshape(1, bd, HG_HEADS, HG_K, HG_V))
```

```python
import functools
from typing import Callable, NamedTuple

import jax
import jax.numpy as jnp
from jax import lax
from jax.experimental import pallas as pl
from jax.experimental.pallas import tpu as pltpu

D_MODEL = 1024
PAST_LEN = 8192
A_WINDOWS = (128, 512, 2048)
A_DILATIONS = (1, 4, 16)
A_GROUPS = 3
A_HEADS = 8
A_HEAD_DIM = 64
A_NKEY = 128
ROPE_THETA = 10000.0
HG_HEADS = 4
HG_K = 128
HG_V = 128
HG_CHUNK = 64
HG_BLOCK = 8
ATTN_BLOCKS_IN_FLIGHT = 4
MEM_LEN = 256
XA_HEADS = 4
XA_HEAD_DIM = 128
EPS = 1e-6

A_W = A_HEADS * A_HEAD_DIM
HG_W = HG_HEADS * HG_K
XA_W = XA_HEADS * XA_HEAD_DIM
N_ATT = 3 * A_GROUPS * A_W
COL_AZ = N_ATT
COL_BQ = COL_AZ + A_W
COL_BF = COL_BQ + HG_W
COL_BI = COL_BF + HG_W
COL_BZ = COL_BI + HG_W
COL_CQ = COL_BZ + HG_W
COL_CZ = COL_CQ + XA_W
COL_GA = COL_CZ + XA_W
COL_GB = COL_GA + D_MODEL
COL_GC = COL_GB + D_MODEL
N_IN = COL_GC + D_MODEL

LANES = 128
VMEM_LIMIT_BYTES = 56 * 1024 * 1024

F32 = jnp.float32
BF16 = jnp.bfloat16
NEG_INF = float("-inf")


def _params(semantics, vmem=None):
    return pltpu.CompilerParams(dimension_semantics=semantics, vmem_limit_bytes=vmem)


def _resident(shape):
    return pl.BlockSpec(shape, lambda *_: (0,) * len(shape), pipeline_mode=pl.Buffered(1))


def _resident_columns(rows, col0, width):
    return pl.BlockSpec((pl.Element(rows), pl.Element(width)), lambda *_: (0, col0), pipeline_mode=pl.Buffered(1))


class _Rider(NamedTuple):
    body: Callable
    inputs: tuple = ()
    in_specs: tuple = ()
    out_shapes: tuple = ()
    out_specs: tuple = ()


_NO_RIDER = _Rider(body=lambda step: None)


def _join_riders(a, b):
    na_in, nb_in, na_out = len(a.inputs), len(b.inputs), len(a.out_shapes)

    def body(step, *refs):
        ins, outs = refs[:na_in + nb_in], refs[na_in + nb_in:]
        a.body(step, *ins[:na_in], *outs[:na_out])
        b.body(step, *ins[na_in:], *outs[na_out:])

    return _Rider(body, a.inputs + b.inputs, a.in_specs + b.in_specs, a.out_shapes + b.out_shapes,
                  a.out_specs + b.out_specs)


def _rmsnorm_rows(x, gain):
    ms = jnp.mean(x * x, axis=-1, keepdims=True)
    return x * lax.rsqrt(ms + EPS) * gain


def _sigmoid(x):
    return 0.5 * jnp.tanh(0.5 * x) + 0.5


def _silu(x):
    return x * _sigmoid(x)


def _tile_row_order(tm, d):
    j = lax.broadcasted_iota(jnp.int32, (tm, LANES), 0)
    per = tm // d
    return (j % per) * d + j // per


def _inproj_attn_kernel(x_ref, g_ref, w_ref, invf_ref, *refs, tm, n_tiles, pos_base, pos_step, tails, row_dil,
                        rider_body, n_rider_in, n_rider_out):
    rider_in, refs = refs[:n_rider_in], refs[n_rider_in:]
    q_refs = refs[0:3]
    k_refs = refs[3:6]
    v_refs = refs[6:9]
    t_refs = refs[9:12]
    rider_out = refs[12:12 + n_rider_out]
    cos_tab, sin_tab, xa_ref, xb_ref, xp_ref = refs[12 + n_rider_out:]
    i = pl.program_id(0)
    n_slab = A_W // LANES
    n_xslab = D_MODEL // LANES
    dils = sorted(set(row_dil))

    @pl.when(i == 0)
    def _():
        for t_idx, d in enumerate(dils):
            ang0 = (_tile_row_order(tm, d) * pos_step).astype(F32) * invf_ref[...]
            cos_tab[t_idx] = jnp.cos(ang0)
            sin_tab[t_idx] = jnp.sin(ang0)

    rider_body(i, *rider_in, *rider_out)

    lane = lax.broadcasted_iota(jnp.int32, (tm, LANES), 1)
    first_half = (lane % A_HEAD_DIM) < (A_HEAD_DIM // 2)
    base =(pos_base + i * tm * pos_step).astype(F32) * invf_ref[...]
    cos_b = jnp.cos(base)
    sin_b = jnp.sin(base)

    def trig(d):
        t_idx = dils.index(d)
        c0, s0 = cos_tab[t_idx], sin_tab[t_idx]
        cos = c0 * cos_b - s0 * sin_b
        sin = s0 * cos_b + c0 * sin_b
        return cos, jnp.where(first_half, -sin, sin)

    def rope(acc, cos, sin_signed):
        outs = []
        for c in range(n_slab):
            xc = acc[:, c * LANES:(c + 1) * LANES]
            partner = jnp.where(first_half, pltpu.roll(xc, LANES - 32, 1), pltpu.roll(xc, 32, 1))
            outs.append(xc * cos + partner * sin_signed)
        return jnp.concatenate(outs, axis=-1)

    xn32 = _rmsnorm_rows(x_ref[...], g_ref[...])
    xn = xn32.astype(BF16)

    lhs = {1: xn}
    if max(row_dil) > 1:
        assert set(row_dil) <= {1, 4, 16}
        per4 = tm // 4
        for c in range(n_xslab):
            xa_ref[c] = xn32[:, c * LANES:(c + 1) * LANES]
        for c in range(n_xslab):
            for r in range(4):
                blk4 = xa_ref[c, pl.ds(r, per4, stride=4), :]
                xb_ref[c, r * per4:(r + 1) * per4, :] = blk4
                xp_ref[0, r * per4:(r + 1) * per4, c * LANES:(c + 1) * LANES] = blk4.astype(BF16)
        lhs[4] = xp_ref[0]
        if 16 in row_dil:
            per16 = tm // 16
            for c in range(n_xslab):
                for r1 in range(4):
                    for r2 in range(4):
                        res = 4 * r2 + r1
                        xp_ref[1, res * per16:(res + 1) * per16, c * LANES:(c + 1) * LANES] = (
                            xb_ref[c, pl.ds(r1 * per4 + r2, per16, stride=4), :].astype(BF16))
            lhs[16] = xp_ref[1]

    def store_rows(ref, d, val):
        per = tm // d
        for r in range(d):
            ref[:, r * A_W:(r + 1) * A_W] = val[r * per:(r + 1) * per, :].astype(ref.dtype)

    def tail_block(g, col0, natural_val):
        first_tile, rows = tails[g]
        cond = (i >= first_tile) if rows >= tm else (i == n_tiles - 1)

        @pl.when(cond)
        def _():
            val = natural_val()
            t_refs[g][:, col0:col0 + A_W] = val if rows >= tm else val[tm - rows:, :]

    cos_n, sin_n = trig(1)
    for g in range(A_GROUPS):
        d = row_dil[g]
        cos, sin_s = (cos_n, sin_n) if d == 1 else trig(d)
        wq = w_ref[:, g * A_W:(g + 1) * A_W]
        wk = w_ref[:, (A_GROUPS + g) * A_W:(A_GROUPS + g + 1) * A_W]
        wv = w_ref[:, (2 * A_GROUPS + g) * A_W:(2 * A_GROUPS + g + 1) * A_W]
        q = rope(jnp.dot(lhs[d], wq, preferred_element_type=F32), cos, sin_s) * (A_HEAD_DIM ** -0.5)
        store_rows(q_refs[g], d, q)
        kr = rope(jnp.dot(lhs[d], wk, preferred_element_type=F32), cos, sin_s)
        store_rows(k_refs[g], d, kr)
        v = jnp.dot(lhs[d], wv, preferred_element_type=F32)
        store_rows(v_refs[g], d, v)
        if d == 1:
            tail_block(g, 0, lambda kr=kr: kr)
            tail_block(g, A_W, lambda v=v: v)
        else:
            tail_block(g, 0, lambda wk=wk: rope(jnp.dot(xn, wk, preferred_element_type=F32), cos_n, sin_n))
            tail_block(g, A_W, lambda wv=wv: jnp.dot(xn, wv, preferred_element_type=F32))


def _inproj_attn(x, gain, w_att, invf, *, tm, pos_base, pos_step, tail_rows, out_dtype, row_dil, rider=_NO_RIDER):
    n = x.shape[0]
    n_tiles = n // tm
    assert 1 in row_dil
    n_orders = len(set(row_dil))
    tails = []
    tail_specs = []
    tail_shapes = []
    for g in range(A_GROUPS):
        rows = tail_rows[g]
        first_tile = (n - rows) // tm
        tails.append((first_tile, rows))
        blk = min(tm, rows)
        tail_specs.append(pl.BlockSpec((blk, 2 * A_W), functools.partial(
            lambda i, ft: (jnp.maximum(i - ft, 0), 0), ft=first_tile)))
        tail_shapes.append(jax.ShapeDtypeStruct((rows, 2 * A_W), F32))
    row_specs = [pl.BlockSpec((tm // d, d * A_W), lambda i: (i, 0)) for d in row_dil] * 3
    row_shapes = [jax.ShapeDtypeStruct((n // d, d * A_W), out_dtype) for d in row_dil] * 3
    kernel = functools.partial(_inproj_attn_kernel, tm=tm, n_tiles=n_tiles, pos_base=pos_base,
                               pos_step=pos_step, tails=tuple(tails), row_dil=tuple(row_dil),
                               rider_body=rider.body, n_rider_in=len(rider.inputs), n_rider_out=len(rider.out_shapes))
    outs = pl.pallas_call(
        kernel,
        out_shape=row_shapes + tail_shapes + list(rider.out_shapes),
        grid=(n_tiles,),
        in_specs=[pl.BlockSpec((tm, D_MODEL), lambda i: (i, 0)),
                  _resident((1, D_MODEL)),
                  _resident_columns(D_MODEL, 0, N_ATT),
                  _resident((1, LANES))] + list(rider.in_specs),
        out_specs=row_specs + tail_specs + list(rider.out_specs),
        scratch_shapes=[pltpu.VMEM((n_orders, tm, LANES), F32), pltpu.VMEM((n_orders, tm, LANES), F32),
                        pltpu.VMEM((D_MODEL // LANES, tm, LANES), F32), pltpu.VMEM((D_MODEL // LANES, tm, LANES), F32),
                        pltpu.VMEM((2, tm, D_MODEL), BF16)],
        compiler_params=_params(("arbitrary",), VMEM_LIMIT_BYTES),
        name="inproj_attn",
    )(x, gain, w_att, invf, *rider.inputs)
    return outs[0:3], outs[3:6], outs[6:9], outs[9:12], outs[12:]


def _inproj_rest_kernel(x_ref, g_ref, w_ref, lbl_ref, *refs, layer, rider_body, n_rider_in):
    rider_in, refs = refs[:n_rider_in], refs[n_rider_in:]
    (az_ref, bq_ref, lf_ref, kin_ref, bi_ref, bz_ref, cq_ref, cz_ref, ga_ref, gb_ref, gc_ref) = refs[:11]
    rider_body(pl.program_id(0), *rider_in, *refs[11:])
    xn = _rmsnorm_rows(x_ref[...], g_ref[...]).astype(BF16)

    def proj(col, width):
        c0 = col - N_ATT
        return jnp.dot(xn, w_ref[:, c0:c0 + width], preferred_element_type=F32)

    logits = lbl_ref[...]
    e = jnp.exp(logits - jnp.max(logits, axis=0, keepdims=True))
    lb = jnp.sum(e[0:layer + 1, :], axis=0, keepdims=True) / jnp.sum(e, axis=0, keepdims=True)

    az_ref[...] = _silu(proj(COL_AZ, A_W)).astype(az_ref.dtype)
    bq_ref[...] = proj(COL_BQ, HG_W).astype(bq_ref.dtype)
    bf = proj(COL_BF, HG_W)
    gate = _sigmoid(bf)
    lf_ref[...] = jnp.log(lb + (1.0 - lb) * gate)
    kin_ref[...] = ((1.0 - lb) * (1.0 - gate)).astype(kin_ref.dtype)
    bi_ref[...] = proj(COL_BI, HG_W).astype(bi_ref.dtype)
    bz_ref[...] = _silu(proj(COL_BZ, HG_W)).astype(bz_ref.dtype)
    cq_ref[...] = proj(COL_CQ, XA_W).astype(cq_ref.dtype)
    cz_ref[...] = _silu(proj(COL_CZ, XA_W)).astype(cz_ref.dtype)
    ga_ref[...] = _sigmoid(proj(COL_GA, D_MODEL)).astype(ga_ref.dtype)
    gb_ref[...] = _sigmoid(proj(COL_GB, D_MODEL)).astype(gb_ref.dtype)
    gc_ref[...] = _sigmoid(proj(COL_GC, D_MODEL)).astype(gc_ref.dtype)


def _inproj_rest(x, gain, w_rest, lb_logits, *, tm, layer, hg_dtype, cq_dtype, rider=_NO_RIDER):
    n = x.shape[0]
    n_rest = N_IN - N_ATT
    s512 = pl.BlockSpec((tm, A_W), lambda i: (i, 0))
    s1024 = pl.BlockSpec((tm, D_MODEL), lambda i: (i, 0))

    def sds(width, dt):
        return jax.ShapeDtypeStruct((n, width), dt)

    outs = pl.pallas_call(
        functools.partial(_inproj_rest_kernel, layer=layer, rider_body=rider.body, n_rider_in=len(rider.inputs)),
        out_shape=[sds(A_W, BF16), sds(HG_W, hg_dtype), sds(HG_W, F32), sds(HG_W, hg_dtype), sds(HG_W, hg_dtype),
                   sds(HG_W, BF16), sds(XA_W, cq_dtype), sds(XA_W, BF16),
                   sds(D_MODEL, BF16), sds(D_MODEL, BF16), sds(D_MODEL, BF16)] + list(rider.out_shapes),
        grid=(n // tm,),
        in_specs=[pl.BlockSpec((tm, D_MODEL), lambda i: (i, 0)),
                  _resident((1, D_MODEL)),
                  _resident_columns(D_MODEL, N_ATT, n_rest),
                  _resident(lb_logits.shape)] + list(rider.in_specs),
        out_specs=[s512] * 8 + [s1024] * 3 + list(rider.out_specs),
        compiler_params=_params(("arbitrary",), VMEM_LIMIT_BYTES),
        name="inproj_rest",
    )(x, gain, w_rest, lb_logits, *rider.inputs)
    return outs[:11], outs[11:]


def _attn_prompt_kernel(q_ref, kc_ref, vc_ref, kp_ref, vp_ref, o_ref, st_ref, kbuf, vbuf, o_acc, st_acc,
                        *, tmv, dil, rps):
    n = pl.program_id(0)
    rstep = pl.program_id(1)
    blk = A_NKEY

    qi = lax.broadcasted_iota(jnp.int32, (blk, 2 * blk), 0)
    ki = lax.broadcasted_iota(jnp.int32, (blk, 2 * blk), 1)
    dist = qi + blk - ki
    band = (dist >= 0) & (dist <= A_NKEY)
    bias = jnp.where(band, 0.0, NEG_INF).astype(F32)
    bias_first = jnp.where(band & (ki >= blk), 0.0, NEG_INF).astype(F32)
    lane = lax.broadcasted_iota(jnp.int32, (blk, LANES), 1)
    low_head = lane < A_HEAD_DIM
    head_mask = (jnp.where(low_head, 1.0, 0.0).astype(BF16), jnp.where(low_head, 0.0, 1.0).astype(BF16))

    for rr in range(rps):
        cs = slice(rr * A_W, (rr + 1) * A_W)
        kbuf[rr, 0:blk, :] = kp_ref[:, cs]
        kbuf[rr, blk:, :] = kc_ref[:, cs]
        vbuf[rr, 0:blk, :] = vp_ref[:, cs]
        vbuf[rr, blk:, :] = vc_ref[:, cs]
        res = rstep * rps + rr

        def body(b, carry, cs=cs, res=res, rr=rr):
            r0 = pl.multiple_of(b * blk, blk)
            is_first = jnp.logical_and(n == 0, b == 0)
            bias_b = jnp.where(is_first, bias_first, bias)
            qb = q_ref[pl.ds(r0, blk), cs]
            kb = kbuf[rr, pl.ds(r0, 2 * blk), :]
            vb = vbuf[rr, pl.ds(r0, 2 * blk), :]
            out_rows = pl.ds(r0 * dil + res, blk, stride=dil)
            stats = jnp.zeros((blk, LANES), F32)
            for pair in range(A_HEADS // 2):
                sl = slice(pair * LANES, (pair + 1) * LANES)
                qp = qb[:, sl]
                kp = kb[:, sl]
                vp = vb[:, sl]
                outs = []
                for hh in range(2):
                    qm = qp * head_mask[hh]
                    s = pl.dot(qm, kp, trans_b=True) + bias_b
                    m = jnp.max(s, axis=-1, keepdims=True)
                    p = jnp.exp(s - m)
                    l = jnp.sum(p, axis=-1, keepdims=True)
                    pv = jnp.dot(p.astype(BF16), vp, preferred_element_type=F32)
                    outs.append(pv * (1.0 / l))
                    lse = m + jnp.log(l)
                    stats = jnp.where(lane == 2 * pair + hh, lse, stats)
                o_acc[pair, out_rows, :] = jnp.where(low_head, outs[0], outs[1])
            st_acc[out_rows, :] = stats
            return carry

        lax.fori_loop(0, tmv // blk, body, 0, unroll=min(ATTN_BLOCKS_IN_FLIGHT, tmv // blk))

    @pl.when(rstep == pl.num_programs(1) - 1)
    def _():
        for pair in range(A_HEADS // 2):
            o_ref[:, pair * LANES:(pair + 1) * LANES] = o_acc[pair].astype(o_ref.dtype)
        st_ref[...] = st_acc[...]


def _attn_prompt(qv, kv, vv, dil, *, rows_per_step):
    l = qv.shape[0]
    t = l * dil
    blk = A_NKEY
    tmv = max(rows_per_step // dil, blk)
    rps = min(dil, max(1, ATTN_BLOCKS_IN_FLIGHT * blk // tmv))
    assert t % (tmv * dil) == 0 and dil % rps == 0
    cur = pl.BlockSpec((tmv, rps * A_W), lambda n, r: (n, r))
    prev = pl.BlockSpec((blk, rps * A_W), lambda n, r: (jnp.maximum(n * (tmv // blk) - 1, 0), r))
    rows = tmv * dil
    return pl.pallas_call(
        functools.partial(_attn_prompt_kernel, tmv=tmv, dil=dil, rps=rps),
        out_shape=[jax.ShapeDtypeStruct((t, A_W), BF16), jax.ShapeDtypeStruct((t, LANES), F32)],
        grid=(t // rows, dil // rps),
        in_specs=[cur, cur, cur, prev, prev],
        out_specs=[pl.BlockSpec((rows, A_W), lambda n, r: (n, 0)), pl.BlockSpec((rows, LANES), lambda n, r: (n, 0))],
        scratch_shapes=[pltpu.VMEM((rps, tmv + blk, A_W), BF16), pltpu.VMEM((rps, tmv + blk, A_W), BF16),
                        pltpu.VMEM((A_HEADS // 2, rows, LANES), F32), pltpu.VMEM((rows, LANES), F32)],
        compiler_params=_params(("parallel", "arbitrary"), VMEM_LIMIT_BYTES),
        name=f"attn_prompt_d{dil}",
    )(qv, kv, vv, kv, vv)


def _hgrn_prompt_kernel(q_ref, lf_ref, k_ref, v_ref, gain_ref, o_ref, s_out_ref, st_ref, *, th,
                        co_scheduled=lambda: None):
    i = pl.program_id(0)
    c_rows = HG_CHUNK
    nblk = c_rows // HG_BLOCK
    shape3 = (nblk, HG_BLOCK, HG_K)

    @pl.when(i == 0)
    def _():
        st_ref[...] = jnp.zeros_like(st_ref)

    co_scheduled()

    sub = lax.broadcasted_iota(jnp.int32, shape3, 1)
    a_row = lax.broadcasted_iota(jnp.int32, (c_rows, c_rows), 0) // HG_BLOCK
    a_col = lax.broadcasted_iota(jnp.int32, (c_rows, c_rows), 1) // HG_BLOCK

    def shift(x3, s):
        return pltpu.roll(x3, s, 1)

    def chunk(c, carry):
        r0 = pl.multiple_of(c * c_rows, c_rows)
        for h in range(HG_HEADS):
            sl = slice(h * HG_K, (h + 1) * HG_K)
            q = q_ref[pl.ds(r0, c_rows), sl].astype(F32)
            k = k_ref[pl.ds(r0, c_rows), sl].astype(F32)
            v = v_ref[pl.ds(r0, c_rows), sl].astype(F32)
            q3, k3, v3 = (a.reshape(shape3) for a in (q, k, v))
            cum = lf_ref[pl.ds(r0, c_rows), sl].reshape(shape3)
            s = 1
            while s < HG_BLOCK:
                cum = cum + jnp.where(sub >= s, shift(cum, s), 0.0)
                s *= 2
            anchors = [jnp.zeros((1, 1, HG_K), F32)]
            for j in range(nblk):
                anchors.append(anchors[j] + cum[j:j + 1, HG_BLOCK - 1:HG_BLOCK, :])
            b3 = cum + jnp.concatenate(anchors[:nblk], axis=0)
            b = b3.reshape(c_rows, HG_K)
            b_last = anchors[nblk].reshape(1, HG_K)

            o3 = jnp.sum(q3 * k3, axis=-1, keepdims=True) * v3
            for d in range(1, HG_BLOCK):
                dec = jnp.exp(jnp.where(sub >= d, cum - shift(cum, d), NEG_INF))
                a = jnp.sum(q3 * shift(k3, d) * dec, axis=-1, keepdims=True)
                o3 = o3 + a * shift(v3, d)
            o = o3.reshape(c_rows, HG_K)

            b_end = jnp.concatenate(anchors[1:], axis=0)
            k_hat = (k3 * jnp.exp(b_end - b3)).reshape(c_rows, HG_K).astype(BF16)
            q_ver = [(q3 * jnp.exp(jnp.minimum(b3 - anchors[j + 1], 0.0))).reshape(c_rows, HG_K).astype(BF16)
                     for j in range(nblk - 1)]
            r = pl.dot(jnp.concatenate(q_ver, axis=0), k_hat, trans_b=True)
            att = jnp.zeros((c_rows, c_rows), F32)
            for j in range(nblk - 1):
                att = jnp.where(a_col == j, r[j * c_rows:(j + 1) * c_rows, :], att)
            att = jnp.where(a_row > a_col, att, 0.0)
            vb = v.astype(BF16)
            o = o + jnp.dot(att.astype(BF16), vb, preferred_element_type=F32)

            st = st_ref[h]
            o = o + pl.dot((q * jnp.exp(b)).astype(BF16), st.astype(BF16), trans_b=True)
            k_end = (k * jnp.exp(b_last - b)).astype(BF16)
            st_ref[h] = st * jnp.exp(b_last) + pl.dot(vb, k_end, trans_a=True)

            on = o * lax.rsqrt(jnp.mean(o * o, axis=-1, keepdims=True) + EPS) * gain_ref[:, sl]
            o_ref[pl.ds(r0, c_rows), sl] = on.astype(o_ref.dtype)
        return carry

    n_chunks = th // c_rows
    lax.fori_loop(0, n_chunks, chunk, 0, unroll=n_chunks <= 4)

    @pl.when(i == pl.num_programs(0) - 1)
    def _():
        for h in range(HG_HEADS):
            s_out_ref[h] = st_ref[h].T


def _mem_kv_kernel(m_ref, g_ref, w_ref, o_ref):
    xn = _rmsnorm_rows(m_ref[...], g_ref[...]).astype(BF16)
    o_ref[...] = jnp.dot(xn, w_ref[...], preferred_element_type=F32)


def _mem_kv(mem, gain, w):
    m = mem.shape[0]
    return pl.pallas_call(
        _mem_kv_kernel,
        out_shape=jax.ShapeDtypeStruct((m, 2 * XA_W), F32),
        name="mem_kv",
    )(mem, gain, w)


def _softmax_rows(s):
    m = jnp.max(s, axis=-1, keepdims=True)
    p = jnp.exp(s - m)
    return p, jnp.sum(p, axis=-1, keepdims=True)


def _final_kernel(*refs, tm, merge, xattn):
    refs = list(refs)
    if merge:
        o_refs = [refs.pop(0) for _ in range(A_GROUPS)]
        st_refs = [refs.pop(0) for _ in range(A_GROUPS)]
    else:
        oa_ref = refs.pop(0)
    if xattn:
        cq_ref, mk_ref, mv_ref = (refs.pop(0) for _ in range(3))
    else:
        oc_ref = refs.pop(0)
    (ob_ref, az_ref, bz_ref, cz_ref, ga_ref, gb_ref, gc_ref, x_ref, pg_ref,
     wa_ref, wb_ref, wc_ref, wo_ref, out_ref) = refs

    if merge:
        lse = [r[...] for r in st_refs]
        mx = jnp.maximum(jnp.maximum(lse[0], lse[1]), lse[2])
        ex = [jnp.exp(x - mx) for x in lse]
        inv = 1.0 / (ex[0] + ex[1] + ex[2])
        e_row = lax.broadcasted_iota(jnp.int32, (LANES, A_W), 0)
        e_col = lax.broadcasted_iota(jnp.int32, (LANES, A_W), 1) // A_HEAD_DIM
        spread = jnp.where(e_row == e_col, 1.0, 0.0).astype(BF16)
        oa = jnp.zeros((tm, A_W), F32)
        for g in range(A_GROUPS):
            alpha = ex[g] * inv
            hi = alpha.astype(BF16)
            lo = (alpha - hi.astype(F32)).astype(BF16)
            w = (jnp.dot(hi, spread, preferred_element_type=F32)
                 + jnp.dot(lo, spread, preferred_element_type=F32))
            oa = oa + w * o_refs[g][...].astype(F32)
    else:
        oa = oa_ref[...].astype(F32)
    ya = (oa * az_ref[...].astype(F32)).astype(BF16)

    if xattn:
        cq = cq_ref[...]
        parts = []
        for h in range(XA_HEADS):
            sl = slice(h * XA_HEAD_DIM, (h + 1) * XA_HEAD_DIM)
            s = pl.dot(cq[:, sl], mk_ref[:, sl], trans_b=True) * (XA_HEAD_DIM ** -0.5)
            p, l = _softmax_rows(s)
            parts.append(jnp.dot(p.astype(BF16), mv_ref[:, sl], preferred_element_type=F32) * (1.0 / l))
        oc = jnp.concatenate(parts, axis=-1)
    else:
        oc = oc_ref[...].astype(F32)
    yc = (oc * cz_ref[...].astype(F32)).astype(BF16)

    yb = (ob_ref[...].astype(F32) * bz_ref[...].astype(F32)).astype(BF16)

    merged = (ga_ref[...].astype(F32) * jnp.dot(ya, wa_ref[...], preferred_element_type=F32)
              + gb_ref[...].astype(F32) * jnp.dot(yb, wb_ref[...], preferred_element_type=F32)
              + gc_ref[...].astype(F32) * jnp.dot(yc, wc_ref[...], preferred_element_type=F32))
    z = jnp.dot(merged.astype(BF16), wo_ref[...], preferred_element_type=F32)
    out_ref[...] = x_ref[...] + _rmsnorm_rows(z, pg_ref[...])


def _final(att_in, xa_in, ob, az, bz, cz, ga, gb, gc, x, post_g, w_pa, w_pb, w_pc, w_out, *, tm, merge, xattn):
    n = x.shape[0]

    def rows(width):
        return pl.BlockSpec((tm, width), lambda i: (i, 0))

    if merge:
        att_specs = [rows(A_W)] * A_GROUPS + [rows(LANES)] * A_GROUPS
    else:
        att_specs = [rows(A_W)]
    if xattn:
        xa_specs = [rows(XA_W), _resident((MEM_LEN, XA_W)), _resident((MEM_LEN, XA_W))]
    else:
        xa_specs = [rows(XA_W)]
    in_specs = (att_specs + xa_specs + [rows(HG_W), rows(A_W), rows(HG_W), rows(XA_W)]
                + [rows(D_MODEL)] * 4 + [_resident((1, D_MODEL))]
                + [_resident((A_W, D_MODEL)), _resident((HG_W, D_MODEL)), _resident((XA_W, D_MODEL)),
                   _resident((D_MODEL, D_MODEL))])
    return pl.pallas_call(
        functools.partial(_final_kernel, tm=tm, merge=merge, xattn=xattn),
        out_shape=jax.ShapeDtypeStruct((n, D_MODEL), F32),
        grid=(n // tm,),
        in_specs=in_specs,
        out_specs=rows(D_MODEL),
        compiler_params=_params(("parallel",), VMEM_LIMIT_BYTES),
        name="final_merge" if merge else "final",
    )(*att_in, *xa_in, ob, az, bz, cz, ga, gb, gc, x, post_g, w_pa, w_pb, w_pc, w_out)


def _head_rows(q_row, head_dim):
    w = q_row.shape[-1]
    sub = lax.broadcasted_iota(jnp.int32, (8, w), 0)
    lane_head = lax.broadcasted_iota(jnp.int32, (8, w), 1) // head_dim
    keep = sub == lane_head
    return jnp.where(keep, jnp.broadcast_to(q_row, (8, w)), 0.0), keep


def _window_cache_view(cache, g):
    bd, wb = cache.shape[0], cache.shape[1]
    assert wb == A_NKEY * A_DILATIONS[g], "window buffer must hold exactly 128 dilated keys"
    return jnp.transpose(cache, (0, 2, 3, 4, 1)).reshape(bd, 2, A_W, wb)


def _attn_sample_group(dil, q_row, k_new, v_new, c_ref):
    wb = c_ref.shape[-1]
    qm, _ = _head_rows(q_row, A_HEAD_DIM)
    kt = c_ref[0].astype(BF16)
    vt = c_ref[1].astype(BF16)
    s = jnp.dot(qm.astype(BF16), kt, preferred_element_type=F32)
    w_pos = lax.broadcasted_iota(jnp.int32, (8, wb), 1)
    s = jnp.where(w_pos % dil == 0, s, NEG_INF)
    s_new = jnp.sum(qm * k_new, axis=-1, keepdims=True)
    m = jnp.maximum(jnp.max(s, axis=-1, keepdims=True), s_new)
    p = jnp.exp(s - m)
    p_new = jnp.exp(s_new - m)
    l = jnp.sum(p, axis=-1, keepdims=True) + p_new
    o = (pl.dot(p.astype(BF16), vt, trans_b=True) + p_new * v_new) * (1.0 / l)
    return o, m + jnp.log(l)


def _attn_sample_kernel(q0, q1, q2, k0, k1, k2, v0, v1, v2, c0, c1, c2, o_ref, *, sb):
    q_refs, k_refs, v_refs, c_refs = (q0, q1, q2), (k0, k1, k2), (v0, v1, v2), (c0, c1, c2)
    _, keep = _head_rows(jnp.zeros((1, A_W), F32), A_HEAD_DIM)
    for b in range(sb):
        row = pl.ds(pl.program_id(0) * sb + b, 1)
        outs = []
        lses = []
        for g in range(A_GROUPS):
            o, lse = _attn_sample_group(A_DILATIONS[g], q_refs[g][row, :], k_refs[g][row, :], v_refs[g][row, :],
                                        c_refs[g].at[b])
            outs.append(o)
            lses.append(lse)
        mx = jnp.maximum(jnp.maximum(lses[0], lses[1]), lses[2])
        ex = [jnp.exp(x - mx) for x in lses]
        inv = 1.0 / (ex[0] + ex[1] + ex[2])
        oa = (ex[0] * outs[0] + ex[1] * outs[1] + ex[2] * outs[2]) * inv
        o_ref[row, :] = jnp.sum(jnp.where(keep, oa, 0.0), axis=0, keepdims=True)


def _hgrn_and_attn_sample_kernel(*refs, th, sb):
    hg_in, at_in = refs[0:5], refs[5:17]
    hg_out, at_out, hg_state = refs[17:19], refs[19], refs[20]
    _hgrn_prompt_kernel(*hg_in, *hg_out, hg_state, th=th,
                        co_scheduled=lambda: _attn_sample_kernel(*at_in, at_out, sb=sb))


def _hgrn_and_attn_sample(q, lf, k, v, gain, qs, ks, vs, caches, *, seqs_per_step):
    t = q.shape[0]
    bd = qs[0].shape[0]
    sb = seqs_per_step
    steps = bd // sb
    assert bd % sb == 0 and t % steps == 0 and (t // steps) % HG_CHUNK == 0
    th = t // steps
    views = [_window_cache_view(caches[g], g) for g in range(A_GROUPS)]
    c_specs = [pl.BlockSpec((sb,) + v_.shape[1:], lambda b: (b, 0, 0, 0)) for v_ in views]
    rows = pl.BlockSpec((th, HG_W), lambda i: (i, 0))
    full = pl.BlockSpec((bd, A_W), lambda b: (0, 0))
    state = pl.BlockSpec((HG_HEADS, HG_K, HG_V), lambda i: (0, 0, 0))
    return pl.pallas_call(
        functools.partial(_hgrn_and_attn_sample_kernel, th=th, sb=sb),
        out_shape=[jax.ShapeDtypeStruct((t, HG_W), BF16), jax.ShapeDtypeStruct((HG_HEADS, HG_K, HG_V), F32),
                   jax.ShapeDtypeStruct((bd, A_W), F32)],
        grid=(steps,),
        in_specs=[rows] * 4 + [_resident((1, HG_W))] + [full] * 9 + c_specs,
        out_specs=[rows, state, full],
        scratch_shapes=[pltpu.VMEM((HG_HEADS, HG_V, HG_K), F32)],
        compiler_params=_params(("arbitrary",), VMEM_LIMIT_BYTES),
        name="hgrn_prompt_attn_sample",
    )(q, lf, k, v, gain, *qs, *ks, *vs, *views)


def _xattn_sample_body(step, q_ref, c_ref, o_ref, *, sb):
    stride = 2 * XA_HEADS
    for b in range(sb):
        row = pl.ds(step * sb + b, 1)
        q_row = q_ref[row, :]
        outs = []
        for h in range(XA_HEADS):
            sl = slice(h * XA_HEAD_DIM, (h + 1) * XA_HEAD_DIM)
            kh = c_ref[b, pl.ds(h, MEM_LEN, stride=stride), :]
            vh = c_ref[b, pl.ds(XA_HEADS + h, MEM_LEN, stride=stride), :]
            s = jnp.sum(kh * q_row[:, sl], axis=-1, keepdims=True) * (XA_HEAD_DIM ** -0.5)
            m = jnp.max(s, axis=0, keepdims=True)
            p = jnp.exp(s - m)
            l = jnp.sum(p, axis=0, keepdims=True)
            outs.append(jnp.sum(p * vh, axis=0, keepdims=True) * (1.0 / l))
        o_ref[row, :] = jnp.concatenate(outs, axis=-1)


def _xattn_sample_rider(cq, mem_cache, *, steps):
    bd = cq.shape[0]
    assert bd % steps == 0
    sb = bd // steps
    rows = MEM_LEN * 2 * XA_HEADS
    view = mem_cache.reshape(bd, rows, XA_HEAD_DIM)
    full = pl.BlockSpec((bd, XA_W), lambda i: (0, 0))
    return _Rider(
        body=functools.partial(_xattn_sample_body, sb=sb),
        inputs=(cq, view),
        in_specs=(full, pl.BlockSpec((sb, rows, XA_HEAD_DIM), lambda i: (i, 0, 0))),
        out_shapes=(jax.ShapeDtypeStruct((bd, XA_W), F32),),
        out_specs=(full,))


def _hgrn_sample_body(step, qt_ref, lft_ref, kt_ref, v_ref, s0_ref, gain_ref, o_ref, s_ref, *, sb):
    for b in range(sb):
        row = pl.ds(step * sb + b, 1)
        v_all = v_ref[row, :]
        outs = []
        for h in range(HG_HEADS):
            sl = slice(h * HG_K, (h + 1) * HG_K)
            d_col = jnp.exp(lft_ref[0, sl, b:b + 1])
            k_col = kt_ref[0, sl, b:b + 1]
            q_col = qt_ref[0, sl, b:b + 1]
            s_new = d_col * s0_ref[b, h] + k_col * v_all[:, sl]
            s_ref[b, h] = s_new
            o = jnp.sum(q_col * s_new, axis=0, keepdims=True)
            outs.append(o * lax.rsqrt(jnp.mean(o * o, axis=-1, keepdims=True) + EPS) * gain_ref[:, sl])
        o_ref[row, :] = jnp.concatenate(outs, axis=-1)


def _hgrn_sample_rider(q, lf, kin, v, s0, gain, *, steps):
    bd = q.shape[0]
    assert bd % steps == 0
    sb = bd // steps

    def cols(a):
        return a.reshape(steps, sb, HG_W).transpose(0, 2, 1)

    col_spec = pl.BlockSpec((1, HG_W, sb), lambda i: (i, 0, 0))
    st_spec = pl.BlockSpec((sb, HG_HEADS, HG_K, HG_V), lambda i: (i, 0, 0, 0))
    full = pl.BlockSpec((bd, HG_W), lambda i: (0, 0))
    return _Rider(
        body=functools.partial(_hgrn_sample_body, sb=sb),
        inputs=(cols(q), cols(lf), cols(kin), v, s0, gain),
        in_specs=(col_spec, col_spec, col_spec, full, st_spec, _resident((1, HG_W))),
        out_shapes=(jax.ShapeDtypeStruct((bd, HG_W), F32), jax.ShapeDtypeStruct(s0.shape, F32)),
        out_specs=(full, st_spec))


def kernel(x_prompt, x_sample, mem_prompt, cache_win128_kv, cache_win512_kv, cache_win2048_kv, state_hgrn, cache_mem_kv, norm_pre, norm_post, w_in, hgrn_lb_logits, hgrn_out_norm, mem_norm, w_mem_kv, w_branch_a, w_branch_b, w_branch_c, w_out):
    depth = w_in.shape[0]
    assert depth == 1, "single-layer trunk"
    layer = 0
    bp, t, _ = x_prompt.shape
    bd, s_len, _ = x_sample.shape
    assert bp == 1 and s_len == 1
    caches = (cache_win128_kv[layer], cache_win512_kv[layer], cache_win2048_kv[layer])

    w_att = w_rest = w_in[layer].astype(BF16)
    pre_g = norm_pre[layer].reshape(1, D_MODEL)
    post_g = norm_post[layer].reshape(1, D_MODEL)
    hg_g = hgrn_out_norm[layer].reshape(1, HG_W)
    w_pa, w_pb, w_pc, w_o = (w[layer].astype(BF16) for w in (w_branch_a, w_branch_b, w_branch_c, w_out))
    half = A_HEAD_DIM // 2
    inv = ROPE_THETA ** (-jnp.arange(0, A_HEAD_DIM, 2, dtype=F32) / A_HEAD_DIM)
    invf = jnp.tile(inv, LANES // half).reshape(1, LANES)

    xs = x_sample.reshape(bd, D_MODEL)
    q_s, k_s, v_s, tails_s, _ = _inproj_attn(xs, pre_g, w_att, invf, tm=bd, pos_base=PAST_LEN, pos_step=0,
                                             tail_rows=(bd,) * A_GROUPS, out_dtype=F32, row_dil=(1,) * A_GROUPS)
    (az_s, bq_s, lf_s, kin_s, bi_s, bz_s, cq_s, cz_s, ga_s, gb_s, gc_s), _ = _inproj_rest(
        xs, pre_g, w_rest, hgrn_lb_logits, tm=bd, layer=layer, hg_dtype=F32, cq_dtype=F32)

    xp = x_prompt.reshape(t, D_MODEL)
    tm_p = 512
    steps_p = t // tm_p
    tail_rows = tuple(min(w, t) for w in A_WINDOWS)
    q_p, k_p, v_p, tails_p, _ = _inproj_attn(
        xp, pre_g, w_att, invf, tm=tm_p, pos_base=0, pos_step=1, tail_rows=tail_rows, out_dtype=BF16,
        row_dil=A_DILATIONS)
    (az, bq, lf, kin, bi, bz, cq, cz, ga, gb, gc), (ob_s, s_s, oc_s) = _inproj_rest(
        xp, pre_g, w_rest, hgrn_lb_logits, tm=tm_p, layer=layer, hg_dtype=BF16, cq_dtype=BF16,
        rider=_join_riders(_hgrn_sample_rider(bq_s, lf_s, kin_s, bi_s, state_hgrn[layer], hg_g, steps=steps_p),
                           _xattn_sample_rider(cq_s, cache_mem_kv[layer], steps=steps_p)))
    att = [_attn_prompt(q_p[g], k_p[g], v_p[g], A_DILATIONS[g], rows_per_step=2048) for g in range(A_GROUPS)]
    ob_p, s_p, oa_s = _hgrn_and_attn_sample(bq, lf, kin, bi, hg_g, q_s, k_s, v_s, caches, seqs_per_step=2)

    mem_kv = _mem_kv(mem_prompt.reshape(MEM_LEN, D_MODEL), mem_norm[layer].reshape(1, D_MODEL),
                     w_mem_kv[layer].astype(BF16))
    mk = mem_kv[:, :XA_W].astype(BF16)
    mv = mem_kv[:, XA_W:].astype(BF16)
    y_p = _final([a[0] for a in att] + [a[1] for a in att], [cq, mk, mv], ob_p, az, bz, cz, ga, gb, gc,
                 xp, post_g, w_pa, w_pb, w_pc, w_o, tm=512, merge=True, xattn=True)

    y_s = _final([oa_s], [oc_s], ob_s, az_s, bz_s, cz_s, ga_s, gb_s, gc_s,
                 xs, post_g, w_pa, w_pb, w_pc, w_o, tm=bd, merge=False, xattn=False)

    new_win_p = [tails_p[g].reshape(1, 1, tail_rows[g], 2, A_HEADS, A_HEAD_DIM) for g in range(A_GROUPS)]
    new_win_s = [tails_s[g].reshape(1, bd, 1, 2, A_HEADS, A_HEAD_DIM) for g in range(A_GROUPS)]
    return (y_p.reshape(bp, t, D_MODEL), y_s.reshape(bd, 1, D_MODEL),
            new_win_p[0], new_win_p[1], new_win_p[2],
            s_p.reshape(1, 1, HG_HEADS, HG_K, HG_V),
            mem_kv.reshape(1, 1, MEM_LEN, 2, XA_HEADS, XA_HEAD_DIM),
            new_win_s[0], new_win_s[1], new_win_s[2],
            s_s.reshape(1, bd, HG_HEADS, HG_K, HG_V))
```

```python
import functools
import math
from typing import Callable, NamedTuple

import jax
import jax.numpy as jnp
from jax import lax
from jax.experimental import pallas as pl
from jax.experimental.pallas import tpu as pltpu

D_MODEL = 1024
PAST_LEN = 8192
A_WINDOWS = (128, 512, 2048)
A_DILATIONS = (1, 4, 16)
A_GROUPS = 3
A_HEADS = 8
A_HEAD_DIM = 64
A_NKEY = 128
ROPE_THETA = 10000.0
HG_HEADS = 4
HG_K = 128
HG_V = 128
HG_CHUNK = 64
HG_BLOCK = 8
ATTN_BLOCKS_IN_FLIGHT = 4
MEM_LEN = 256
XA_HEADS = 4
XA_HEAD_DIM = 128
EPS = 1e-6

A_W = A_HEADS * A_HEAD_DIM
HG_W = HG_HEADS * HG_K
XA_W = XA_HEADS * XA_HEAD_DIM
N_ATT = 3 * A_GROUPS * A_W
COL_AZ = N_ATT
COL_BQ = COL_AZ + A_W
COL_BF = COL_BQ + HG_W
COL_BI = COL_BF + HG_W
COL_BZ = COL_BI + HG_W
COL_CQ = COL_BZ + HG_W
COL_CZ = COL_CQ + XA_W
COL_GA = COL_CZ + XA_W
COL_GB = COL_GA + D_MODEL
COL_GC = COL_GB + D_MODEL
N_IN = COL_GC + D_MODEL

LANES = 128
VMEM_LIMIT_BYTES = 56 * 1024 * 1024

F32 = jnp.float32
BF16 = jnp.bfloat16
NEG_INF = float("-inf")
LOG2E = math.log2(math.e)
LN2 = math.log(2.0)


def _params(semantics, vmem=None):
    return pltpu.CompilerParams(dimension_semantics=semantics, vmem_limit_bytes=vmem)


def _resident(shape):
    return pl.BlockSpec(shape, lambda *_: (0,) * len(shape), pipeline_mode=pl.Buffered(1))


def _resident_columns(rows, col0, width):
    return pl.BlockSpec((pl.Element(rows), pl.Element(width)), lambda *_: (0, col0), pipeline_mode=pl.Buffered(1))


class _Rider(NamedTuple):
    body: Callable
    inputs: tuple = ()
    in_specs: tuple = ()
    out_shapes: tuple = ()
    out_specs: tuple = ()


_NO_RIDER = _Rider(body=lambda step: None)


def _join_riders(a, b):
    na_in, nb_in, na_out = len(a.inputs), len(b.inputs), len(a.out_shapes)

    def body(step, *refs):
        ins, outs = refs[:na_in + nb_in], refs[na_in + nb_in:]
        a.body(step, *ins[:na_in], *outs[:na_out])
        b.body(step, *ins[na_in:], *outs[na_out:])

    return _Rider(body, a.inputs + b.inputs, a.in_specs + b.in_specs, a.out_shapes + b.out_shapes,
                  a.out_specs + b.out_specs)


def _rmsnorm_rows(x, gain):
    ms = jnp.mean(x * x, axis=-1, keepdims=True)
    return x * lax.rsqrt(ms + EPS) * gain


def _sigmoid(x):
    return 0.5 * jnp.tanh(0.5 * x) + 0.5


def _silu(x):
    return x * _sigmoid(x)


def _tile_row_order(tm, d):
    j = lax.broadcasted_iota(jnp.int32, (tm, LANES), 0)
    per = tm // d
    return (j % per) * d + j // per


def _inproj_attn_kernel(x_ref, g_ref, w_ref, invf_ref, *refs, tm, n_tiles, pos_base, pos_step, tails, row_dil,
                        q_scale, rider_body, n_rider_in, n_rider_out):
    rider_in, refs = refs[:n_rider_in], refs[n_rider_in:]
    q_refs = refs[0:3]
    k_refs = refs[3:6]
    v_refs = refs[6:9]
    t_refs = refs[9:12]
    rider_out = refs[12:12 + n_rider_out]
    cos_tab, sin_tab, xa_ref, xb_ref, xp_ref = refs[12 + n_rider_out:]
    i = pl.program_id(0)
    n_slab = A_W // LANES
    n_xslab = D_MODEL // LANES
    dils = sorted(set(row_dil))

    @pl.when(i == 0)
    def _():
        for t_idx, d in enumerate(dils):
            ang0 = (_tile_row_order(tm, d) * pos_step).astype(F32) * invf_ref[...]
            cos_tab[t_idx] = jnp.cos(ang0)
            sin_tab[t_idx] = jnp.sin(ang0)

    rider_body(i, *rider_in, *rider_out)

    lane = lax.broadcasted_iota(jnp.int32, (tm, LANES), 1)
    first_half = (lane % A_HEAD_DIM) < (A_HEAD_DIM // 2)
    base =(pos_base + i * tm * pos_step).astype(F32) * invf_ref[...]
    cos_b = jnp.cos(base)
    sin_b = jnp.sin(base)

    def trig(d):
        t_idx = dils.index(d)
        c0, s0 = cos_tab[t_idx], sin_tab[t_idx]
        cos = c0 * cos_b - s0 * sin_b
        sin = s0 * cos_b + c0 * sin_b
        return cos, jnp.where(first_half, -sin, sin)

    def rope(acc, cos, sin_signed):
        outs = []
        for c in range(n_slab):
            xc = acc[:, c * LANES:(c + 1) * LANES]
            partner = jnp.where(first_half, pltpu.roll(xc, LANES - 32, 1), pltpu.roll(xc, 32, 1))
            outs.append(xc * cos + partner * sin_signed)
        return jnp.concatenate(outs, axis=-1)

    xn32 = _rmsnorm_rows(x_ref[...], g_ref[...])
    xn = xn32.astype(BF16)

    lhs = {1: xn}
    if max(row_dil) > 1:
        assert set(row_dil) <= {1, 4, 16}
        per4 = tm // 4
        for c in range(n_xslab):
            xa_ref[c] = xn32[:, c * LANES:(c + 1) * LANES]
        for c in range(n_xslab):
            for r in range(4):
                blk4 = xa_ref[c, pl.ds(r, per4, stride=4), :]
                xb_ref[c, r * per4:(r + 1) * per4, :] = blk4
                xp_ref[0, r * per4:(r + 1) * per4, c * LANES:(c + 1) * LANES] = blk4.astype(BF16)
        lhs[4] = xp_ref[0]
        if 16 in row_dil:
            per16 = tm // 16
            for c in range(n_xslab):
                for r1 in range(4):
                    for r2 in range(4):
                        res = 4 * r2 + r1
                        xp_ref[1, res * per16:(res + 1) * per16, c * LANES:(c + 1) * LANES] = (
                            xb_ref[c, pl.ds(r1 * per4 + r2, per16, stride=4), :].astype(BF16))
            lhs[16] = xp_ref[1]

    def store_rows(ref, d, val):
        per = tm // d
        for r in range(d):
            ref[:, r * A_W:(r + 1) * A_W] = val[r * per:(r + 1) * per, :].astype(ref.dtype)

    def tail_block(g, col0, natural_val):
        first_tile, rows = tails[g]
        cond = (i >= first_tile) if rows >= tm else (i == n_tiles - 1)

        @pl.when(cond)
        def _():
            val = natural_val()
            t_refs[g][:, col0:col0 + A_W] = val if rows >= tm else val[tm - rows:, :]

    cos_n, sin_n = trig(1)
    for g in range(A_GROUPS):
        d = row_dil[g]
        cos, sin_s = (cos_n, sin_n) if d == 1 else trig(d)
        wq = w_ref[:, g * A_W:(g + 1) * A_W]
        wk = w_ref[:, (A_GROUPS + g) * A_W:(A_GROUPS + g + 1) * A_W]
        wv = w_ref[:, (2 * A_GROUPS + g) * A_W:(2 * A_GROUPS + g + 1) * A_W]
        q = rope(jnp.dot(lhs[d], wq, preferred_element_type=F32), cos, sin_s) * q_scale
        store_rows(q_refs[g], d, q)
        kr = rope(jnp.dot(lhs[d], wk, preferred_element_type=F32), cos, sin_s)
        store_rows(k_refs[g], d, kr)
        v = jnp.dot(lhs[d], wv, preferred_element_type=F32)
        store_rows(v_refs[g], d, v)
        if d == 1:
            tail_block(g, 0, lambda kr=kr: kr)
            tail_block(g, A_W, lambda v=v: v)
        else:
            tail_block(g, 0, lambda wk=wk: rope(jnp.dot(xn, wk, preferred_element_type=F32), cos_n, sin_n))
            tail_block(g, A_W, lambda wv=wv: jnp.dot(xn, wv, preferred_element_type=F32))


def _inproj_attn(x, gain, w_att, invf, *, tm, pos_base, pos_step, tail_rows, out_dtype, row_dil, q_scale,
                 rider=_NO_RIDER):
    n = x.shape[0]
    n_tiles = n // tm
    assert 1 in row_dil
    n_orders = len(set(row_dil))
    tails = []
    tail_specs = []
    tail_shapes = []
    for g in range(A_GROUPS):
        rows = tail_rows[g]
        first_tile = (n - rows) // tm
        tails.append((first_tile, rows))
        blk = min(tm, rows)
        tail_specs.append(pl.BlockSpec((blk, 2 * A_W), functools.partial(
            lambda i, ft: (jnp.maximum(i - ft, 0), 0), ft=first_tile)))
        tail_shapes.append(jax.ShapeDtypeStruct((rows, 2 * A_W), F32))
    row_specs = [pl.BlockSpec((tm // d, d * A_W), lambda i: (i, 0)) for d in row_dil] * 3
    row_shapes = [jax.ShapeDtypeStruct((n // d, d * A_W), out_dtype) for d in row_dil] * 3
    kernel = functools.partial(_inproj_attn_kernel, tm=tm, n_tiles=n_tiles, pos_base=pos_base,
                               pos_step=pos_step, tails=tuple(tails), row_dil=tuple(row_dil), q_scale=q_scale,
                               rider_body=rider.body, n_rider_in=len(rider.inputs), n_rider_out=len(rider.out_shapes))
    outs = pl.pallas_call(
        kernel,
        out_shape=row_shapes + tail_shapes + list(rider.out_shapes),
        grid=(n_tiles,),
        in_specs=[pl.BlockSpec((tm, D_MODEL), lambda i: (i, 0)),
                  _resident((1, D_MODEL)),
                  _resident_columns(D_MODEL, 0, N_ATT),
                  _resident((1, LANES))] + list(rider.in_specs),
        out_specs=row_specs + tail_specs + list(rider.out_specs),
        scratch_shapes=[pltpu.VMEM((n_orders, tm, LANES), F32), pltpu.VMEM((n_orders, tm, LANES), F32),
                        pltpu.VMEM((D_MODEL // LANES, tm, LANES), F32), pltpu.VMEM((D_MODEL // LANES, tm, LANES), F32),
                        pltpu.VMEM((2, tm, D_MODEL), BF16)],
        compiler_params=_params(("arbitrary",), VMEM_LIMIT_BYTES),
        name="inproj_attn",
    )(x, gain, w_att, invf, *rider.inputs)
    return outs[0:3], outs[3:6], outs[6:9], outs[9:12], outs[12:]


def _inproj_rest_kernel(x_ref, g_ref, w_ref, lbl_ref, *refs, layer, rider_body, n_rider_in):
    rider_in, refs = refs[:n_rider_in], refs[n_rider_in:]
    (az_ref, bq_ref, lf_ref, kin_ref, bi_ref, bz_ref, cq_ref, cz_ref, ga_ref, gb_ref, gc_ref) = refs[:11]
    rider_body(pl.program_id(0), *rider_in, *refs[11:])
    xn = _rmsnorm_rows(x_ref[...], g_ref[...]).astype(BF16)

    def proj(col, width):
        c0 = col - N_ATT
        return jnp.dot(xn, w_ref[:, c0:c0 + width], preferred_element_type=F32)

    logits = lbl_ref[...]
    e = jnp.exp(logits - jnp.max(logits, axis=0, keepdims=True))
    lb = jnp.sum(e[0:layer + 1, :], axis=0, keepdims=True) / jnp.sum(e, axis=0, keepdims=True)

    az_ref[...] = _silu(proj(COL_AZ, A_W)).astype(az_ref.dtype)
    bq_ref[...] = proj(COL_BQ, HG_W).astype(bq_ref.dtype)
    bf = proj(COL_BF, HG_W)
    gate = _sigmoid(bf)
    lf_ref[...] = jnp.log(lb + (1.0 - lb) * gate)
    kin_ref[...] = ((1.0 - lb) * (1.0 - gate)).astype(kin_ref.dtype)
    bi_ref[...] = proj(COL_BI, HG_W).astype(bi_ref.dtype)
    bz_ref[...] = _silu(proj(COL_BZ, HG_W)).astype(bz_ref.dtype)
    cq_ref[...] = proj(COL_CQ, XA_W).astype(cq_ref.dtype)
    cz_ref[...] = _silu(proj(COL_CZ, XA_W)).astype(cz_ref.dtype)
    ga_ref[...] = _sigmoid(proj(COL_GA, D_MODEL)).astype(ga_ref.dtype)
    gb_ref[...] = _sigmoid(proj(COL_GB, D_MODEL)).astype(gb_ref.dtype)
    gc_ref[...] = _sigmoid(proj(COL_GC, D_MODEL)).astype(gc_ref.dtype)


def _inproj_rest(x, gain, w_rest, lb_logits, *, tm, layer, hg_dtype, cq_dtype, rider=_NO_RIDER):
    n = x.shape[0]
    n_rest = N_IN - N_ATT
    s512 = pl.BlockSpec((tm, A_W), lambda i: (i, 0))
    s1024 = pl.BlockSpec((tm, D_MODEL), lambda i: (i, 0))

    def sds(width, dt):
        return jax.ShapeDtypeStruct((n, width), dt)

    outs = pl.pallas_call(
        functools.partial(_inproj_rest_kernel, layer=layer, rider_body=rider.body, n_rider_in=len(rider.inputs)),
        out_shape=[sds(A_W, BF16), sds(HG_W, hg_dtype), sds(HG_W, F32), sds(HG_W, hg_dtype), sds(HG_W, hg_dtype),
                   sds(HG_W, BF16), sds(XA_W, cq_dtype), sds(XA_W, BF16),
                   sds(D_MODEL, BF16), sds(D_MODEL, BF16), sds(D_MODEL, BF16)] + list(rider.out_shapes),
        grid=(n // tm,),
        in_specs=[pl.BlockSpec((tm, D_MODEL), lambda i: (i, 0)),
                  _resident((1, D_MODEL)),
                  _resident_columns(D_MODEL, N_ATT, n_rest),
                  _resident(lb_logits.shape)] + list(rider.in_specs),
        out_specs=[s512] * 8 + [s1024] * 3 + list(rider.out_specs),
        compiler_params=_params(("arbitrary",), VMEM_LIMIT_BYTES),
        name="inproj_rest",
    )(x, gain, w_rest, lb_logits, *rider.inputs)
    return outs[:11], outs[11:]


def _attn_prompt_kernel(q_ref, kc_ref, vc_ref, kp_ref, vp_ref, o_ref, st_ref, kbuf, vbuf, o_acc, st_acc,
                        *, tmv, dil, rps):
    n = pl.program_id(0)
    rstep = pl.program_id(1)
    blk = A_NKEY

    qi = lax.broadcasted_iota(jnp.int32, (blk, 2 * blk), 0)
    ki = lax.broadcasted_iota(jnp.int32, (blk, 2 * blk), 1)
    dist = qi + blk - ki
    band = (dist >= 0) & (dist <= A_NKEY)
    bias = jnp.where(band, 0.0, NEG_INF).astype(F32)
    bias_first = jnp.where(band & (ki >= blk), 0.0, NEG_INF).astype(F32)
    lane = lax.broadcasted_iota(jnp.int32, (blk, LANES), 1)
    low_head = lane < A_HEAD_DIM
    head_mask = (jnp.where(low_head, 1.0, 0.0).astype(BF16), jnp.where(low_head, 0.0, 1.0).astype(BF16))

    for rr in range(rps):
        cs = slice(rr * A_W, (rr + 1) * A_W)
        kbuf[rr, 0:blk, :] = kp_ref[:, cs]
        kbuf[rr, blk:, :] = kc_ref[:, cs]
        vbuf[rr, 0:blk, :] = vp_ref[:, cs]
        vbuf[rr, blk:, :] = vc_ref[:, cs]
        res = rstep * rps + rr

        def body(b, carry, cs=cs, res=res, rr=rr):
            r0 = pl.multiple_of(b * blk, blk)
            is_first = jnp.logical_and(n == 0, b == 0)
            bias_b = jnp.where(is_first, bias_first, bias)
            qb = q_ref[pl.ds(r0, blk), cs]
            kb = kbuf[rr, pl.ds(r0, 2 * blk), :]
            vb = vbuf[rr, pl.ds(r0, 2 * blk), :]
            out_rows = pl.ds(r0 * dil + res, blk, stride=dil)
            stats = jnp.zeros((blk, LANES), F32)
            for pair in range(A_HEADS // 2):
                sl = slice(pair * LANES, (pair + 1) * LANES)
                qp = qb[:, sl]
                kp = kb[:, sl]
                vp = vb[:, sl]
                outs = []
                for hh in range(2):
                    qm = qp * head_mask[hh]
                    s = pl.dot(qm, kp, trans_b=True) + bias_b
                    m = jnp.max(s, axis=-1, keepdims=True)
                    p = jnp.exp2(s - m)
                    l = jnp.sum(p, axis=-1, keepdims=True)
                    pv = jnp.dot(p.astype(BF16), vp, preferred_element_type=F32)
                    outs.append(pv * (1.0 / l))
                    lse = m * LN2 + jnp.log(l)
                    stats = jnp.where(lane == 2 * pair + hh, lse, stats)
                o_acc[pair, out_rows, :] = jnp.where(low_head, outs[0], outs[1])
            st_acc[out_rows, :] = stats
            return carry

        lax.fori_loop(0, tmv // blk, body, 0, unroll=min(ATTN_BLOCKS_IN_FLIGHT, tmv // blk))

    @pl.when(rstep == pl.num_programs(1) - 1)
    def _():
        for pair in range(A_HEADS // 2):
            o_ref[:, pair * LANES:(pair + 1) * LANES] = o_acc[pair].astype(o_ref.dtype)
        st_ref[...] = st_acc[...]


def _attn_prompt(qv, kv, vv, dil, *, rows_per_step):
    l = qv.shape[0]
    t = l * dil
    blk = A_NKEY
    tmv = max(rows_per_step // dil, blk)
    rps = min(dil, max(1, ATTN_BLOCKS_IN_FLIGHT * blk // tmv))
    assert t % (tmv * dil) == 0 and dil % rps == 0
    cur = pl.BlockSpec((tmv, rps * A_W), lambda n, r: (n, r))
    prev = pl.BlockSpec((blk, rps * A_W), lambda n, r: (jnp.maximum(n * (tmv // blk) - 1, 0), r))
    rows = tmv * dil
    return pl.pallas_call(
        functools.partial(_attn_prompt_kernel, tmv=tmv, dil=dil, rps=rps),
        out_shape=[jax.ShapeDtypeStruct((t, A_W), BF16), jax.ShapeDtypeStruct((t, LANES), F32)],
        grid=(t // rows, dil // rps),
        in_specs=[cur, cur, cur, prev, prev],
        out_specs=[pl.BlockSpec((rows, A_W), lambda n, r: (n, 0)), pl.BlockSpec((rows, LANES), lambda n, r: (n, 0))],
        scratch_shapes=[pltpu.VMEM((rps, tmv + blk, A_W), BF16), pltpu.VMEM((rps, tmv + blk, A_W), BF16),
                        pltpu.VMEM((A_HEADS // 2, rows, LANES), F32), pltpu.VMEM((rows, LANES), F32)],
        compiler_params=_params(("parallel", "arbitrary"), VMEM_LIMIT_BYTES),
        name=f"attn_prompt_d{dil}",
    )(qv, kv, vv, kv, vv)


def _hgrn_prompt_kernel(q_ref, lf_ref, k_ref, v_ref, gain_ref, o_ref, s_out_ref, st_ref, *, th,
                        co_scheduled=lambda: None):
    i = pl.program_id(0)
    c_rows = HG_CHUNK
    nblk = c_rows // HG_BLOCK
    shape3 = (nblk, HG_BLOCK, HG_K)

    @pl.when(i == 0)
    def _():
        st_ref[...] = jnp.zeros_like(st_ref)

    co_scheduled()

    sub = lax.broadcasted_iota(jnp.int32, shape3, 1)
    a_row = lax.broadcasted_iota(jnp.int32, (c_rows, c_rows), 0) // HG_BLOCK
    a_col = lax.broadcasted_iota(jnp.int32, (c_rows, c_rows), 1) // HG_BLOCK

    def shift(x3, s):
        return pltpu.roll(x3, s, 1)

    def chunk(c, carry):
        r0 = pl.multiple_of(c * c_rows, c_rows)
        for h in range(HG_HEADS):
            sl = slice(h * HG_K, (h + 1) * HG_K)
            q = q_ref[pl.ds(r0, c_rows), sl].astype(F32)
            k = k_ref[pl.ds(r0, c_rows), sl].astype(F32)
            v = v_ref[pl.ds(r0, c_rows), sl].astype(F32)
            q3, k3, v3 = (a.reshape(shape3) for a in (q, k, v))
            cum = lf_ref[pl.ds(r0, c_rows), sl].reshape(shape3)
            s = 1
            while s < HG_BLOCK:
                cum = cum + jnp.where(sub >= s, shift(cum, s), 0.0)
                s *= 2
            anchors = [jnp.zeros((1, 1, HG_K), F32)]
            for j in range(nblk):
                anchors.append(anchors[j] + cum[j:j + 1, HG_BLOCK - 1:HG_BLOCK, :])
            b3 = cum + jnp.concatenate(anchors[:nblk], axis=0)
            b = b3.reshape(c_rows, HG_K)
            b_last = anchors[nblk].reshape(1, HG_K)

            o3 = jnp.sum(q3 * k3, axis=-1, keepdims=True) * v3
            for d in range(1, HG_BLOCK):
                dec = jnp.exp(jnp.where(sub >= d, cum - shift(cum, d), NEG_INF))
                a = jnp.sum(q3 * shift(k3, d) * dec, axis=-1, keepdims=True)
                o3 = o3 + a * shift(v3, d)
            o = o3.reshape(c_rows, HG_K)

            b_end = jnp.concatenate(anchors[1:], axis=0)
            k_hat = (k3 * jnp.exp(b_end - b3)).reshape(c_rows, HG_K).astype(BF16)
            q_ver = [(q3 * jnp.exp(jnp.minimum(b3 - anchors[j + 1], 0.0))).reshape(c_rows, HG_K).astype(BF16)
                     for j in range(nblk - 1)]
            r = pl.dot(jnp.concatenate(q_ver, axis=0), k_hat, trans_b=True)
            att = jnp.zeros((c_rows, c_rows), F32)
            for j in range(nblk - 1):
                att = jnp.where(a_col == j, r[j * c_rows:(j + 1) * c_rows, :], att)
            att = jnp.where(a_row > a_col, att, 0.0)
            vb = v.astype(BF16)
            o = o + jnp.dot(att.astype(BF16), vb, preferred_element_type=F32)

            st = st_ref[h]
            o = o + pl.dot((q * jnp.exp(b)).astype(BF16), st.astype(BF16), trans_b=True)
            k_end = (k * jnp.exp(b_last - b)).astype(BF16)
            st_ref[h] = st * jnp.exp(b_last) + pl.dot(vb, k_end, trans_a=True)

            on = o * lax.rsqrt(jnp.mean(o * o, axis=-1, keepdims=True) + EPS) * gain_ref[:, sl]
            o_ref[pl.ds(r0, c_rows), sl] = on.astype(o_ref.dtype)
        return carry

    n_chunks = th // c_rows
    lax.fori_loop(0, n_chunks, chunk, 0, unroll=n_chunks <= 4)

    @pl.when(i == pl.num_programs(0) - 1)
    def _():
        for h in range(HG_HEADS):
            s_out_ref[h] = st_ref[h].T


def _mem_kv_kernel(m_ref, g_ref, w_ref, o_ref):
    xn = _rmsnorm_rows(m_ref[...], g_ref[...]).astype(BF16)
    o_ref[...] = jnp.dot(xn, w_ref[...], preferred_element_type=F32)


def _mem_kv(mem, gain, w):
    m = mem.shape[0]
    return pl.pallas_call(
        _mem_kv_kernel,
        out_shape=jax.ShapeDtypeStruct((m, 2 * XA_W), F32),
        name="mem_kv",
    )(mem, gain, w)


def _softmax2_rows(s2):
    m = jnp.max(s2, axis=-1, keepdims=True)
    p = jnp.exp2(s2 - m)
    return p, jnp.sum(p, axis=-1, keepdims=True)


def _xattn_rows(cq, mk_ref, mv_ref):
    parts = []
    for h in range(XA_HEADS):
        sl = slice(h * XA_HEAD_DIM, (h + 1) * XA_HEAD_DIM)
        s2 = pl.dot(cq[:, sl], mk_ref[:, sl], trans_b=True) * (XA_HEAD_DIM ** -0.5 * LOG2E)
        p, l = _softmax2_rows(s2)
        parts.append(jnp.dot(p.astype(BF16), mv_ref[:, sl], preferred_element_type=F32) * (1.0 / l))
    return jnp.concatenate(parts, axis=-1)


def _final_kernel(*refs, tm, merge):
    refs = list(refs)
    if merge:
        o_refs = [refs.pop(0) for _ in range(A_GROUPS)]
        st_refs = [refs.pop(0) for _ in range(A_GROUPS)]
    else:
        oa_ref = refs.pop(0)
    (oc_ref, ob_ref, az_ref, bz_ref, cz_ref, ga_ref, gb_ref, gc_ref, x_ref, pg_ref,
     wa_ref, wb_ref, wc_ref, wo_ref, out_ref) = refs

    if merge:
        lse = [r[...] for r in st_refs]
        mx = jnp.maximum(jnp.maximum(lse[0], lse[1]), lse[2])
        ex = [jnp.exp(x - mx) for x in lse]
        inv = 1.0 / (ex[0] + ex[1] + ex[2])
        e_row = lax.broadcasted_iota(jnp.int32, (LANES, A_W), 0)
        e_col = lax.broadcasted_iota(jnp.int32, (LANES, A_W), 1) // A_HEAD_DIM
        spread = jnp.where(e_row == e_col, 1.0, 0.0).astype(BF16)
        oa = jnp.zeros((tm, A_W), F32)
        for g in range(A_GROUPS):
            alpha = ex[g] * inv
            hi = alpha.astype(BF16)
            lo = (alpha - hi.astype(F32)).astype(BF16)
            w = (jnp.dot(hi, spread, preferred_element_type=F32)
                 + jnp.dot(lo, spread, preferred_element_type=F32))
            oa = oa + w * o_refs[g][...].astype(F32)
    else:
        oa = oa_ref[...].astype(F32)
    ya = (oa * az_ref[...].astype(F32)).astype(BF16)

    yc = (oc_ref[...].astype(F32) * cz_ref[...].astype(F32)).astype(BF16)

    yb = (ob_ref[...].astype(F32) * bz_ref[...].astype(F32)).astype(BF16)

    merged = (ga_ref[...].astype(F32) * jnp.dot(ya, wa_ref[...], preferred_element_type=F32)
              + gb_ref[...].astype(F32) * jnp.dot(yb, wb_ref[...], preferred_element_type=F32)
              + gc_ref[...].astype(F32) * jnp.dot(yc, wc_ref[...], preferred_element_type=F32))
    z = jnp.dot(merged.astype(BF16), wo_ref[...], preferred_element_type=F32)
    out_ref[...] = x_ref[...] + _rmsnorm_rows(z, pg_ref[...])


def _final(att_in, oc, ob, az, bz, cz, ga, gb, gc, x, post_g, w_pa, w_pb, w_pc, w_out, *, tm, merge):
    n = x.shape[0]

    def rows(width):
        return pl.BlockSpec((tm, width), lambda i: (i, 0))

    if merge:
        att_specs = [rows(A_W)] * A_GROUPS + [rows(LANES)] * A_GROUPS
    else:
        att_specs = [rows(A_W)]
    in_specs = (att_specs + [rows(XA_W), rows(HG_W), rows(A_W), rows(HG_W), rows(XA_W)]
                + [rows(D_MODEL)] * 4 + [_resident((1, D_MODEL))]
                + [_resident((A_W, D_MODEL)), _resident((HG_W, D_MODEL)), _resident((XA_W, D_MODEL)),
                   _resident((D_MODEL, D_MODEL))])
    return pl.pallas_call(
        functools.partial(_final_kernel, tm=tm, merge=merge),
        out_shape=jax.ShapeDtypeStruct((n, D_MODEL), F32),
        grid=(n // tm,),
        in_specs=in_specs,
        out_specs=rows(D_MODEL),
        compiler_params=_params(("parallel",), VMEM_LIMIT_BYTES),
        name="final_merge" if merge else "final",
    )(*att_in, oc, ob, az, bz, cz, ga, gb, gc, x, post_g, w_pa, w_pb, w_pc, w_out)


def _head_rows(q_row, head_dim):
    w = q_row.shape[-1]
    sub = lax.broadcasted_iota(jnp.int32, (8, w), 0)
    lane_head = lax.broadcasted_iota(jnp.int32, (8, w), 1) // head_dim
    keep = sub == lane_head
    return jnp.where(keep, jnp.broadcast_to(q_row, (8, w)), 0.0), keep


def _window_cache_view(cache, g):
    bd, wb = cache.shape[0], cache.shape[1]
    assert wb == A_NKEY * A_DILATIONS[g], "window buffer must hold exactly 128 dilated keys"
    return jnp.transpose(cache, (0, 2, 3, 4, 1)).reshape(bd, 2, A_W, wb)


def _attn_sample_group(dil, q_row, k_new, v_new, c_ref):
    wb = c_ref.shape[-1]
    qm, _ = _head_rows(q_row, A_HEAD_DIM)
    kt = c_ref[0].astype(BF16)
    vt = c_ref[1].astype(BF16)
    s = jnp.dot(qm.astype(BF16), kt, preferred_element_type=F32)
    w_pos = lax.broadcasted_iota(jnp.int32, (8, wb), 1)
    s = jnp.where(w_pos % dil == 0, s, NEG_INF)
    s_new = jnp.sum(qm * k_new, axis=-1, keepdims=True)
    m = jnp.maximum(jnp.max(s, axis=-1, keepdims=True), s_new)
    p = jnp.exp(s - m)
    p_new = jnp.exp(s_new - m)
    l = jnp.sum(p, axis=-1, keepdims=True) + p_new
    o = (pl.dot(p.astype(BF16), vt, trans_b=True) + p_new * v_new) * (1.0 / l)
    return o, m + jnp.log(l)


def _attn_sample_kernel(q0, q1, q2, k0, k1, k2, v0, v1, v2, c0, c1, c2, o_ref, *, sb):
    q_refs, k_refs, v_refs, c_refs = (q0, q1, q2), (k0, k1, k2), (v0, v1, v2), (c0, c1, c2)
    _, keep = _head_rows(jnp.zeros((1, A_W), F32), A_HEAD_DIM)
    for b in range(sb):
        row = pl.ds(pl.program_id(0) * sb + b, 1)
        outs = []
        lses = []
        for g in range(A_GROUPS):
            o, lse = _attn_sample_group(A_DILATIONS[g], q_refs[g][row, :], k_refs[g][row, :], v_refs[g][row, :],
                                        c_refs[g].at[b])
            outs.append(o)
            lses.append(lse)
        mx = jnp.maximum(jnp.maximum(lses[0], lses[1]), lses[2])
        ex = [jnp.exp(x - mx) for x in lses]
        inv = 1.0 / (ex[0] + ex[1] + ex[2])
        oa = (ex[0] * outs[0] + ex[1] * outs[1] + ex[2] * outs[2]) * inv
        o_ref[row, :] = jnp.sum(jnp.where(keep, oa, 0.0), axis=0, keepdims=True)


def _hgrn_and_attn_sample_kernel(*refs, th, sb):
    hg_in, at_in, (cq_ref, mk_ref, mv_ref) = refs[0:5], refs[5:17], refs[17:20]
    hg_out, at_out, oc_ref, hg_state = refs[20:22], refs[22], refs[23], refs[24]

    def co_scheduled():
        _attn_sample_kernel(*at_in, at_out, sb=sb)
        oc_ref[...] = _xattn_rows(cq_ref[...], mk_ref, mv_ref).astype(oc_ref.dtype)

    _hgrn_prompt_kernel(*hg_in, *hg_out, hg_state, th=th, co_scheduled=co_scheduled)


def _hgrn_and_attn_sample(q, lf, k, v, gain, qs, ks, vs, caches, cq, mk, mv, *, seqs_per_step):
    t = q.shape[0]
    bd = qs[0].shape[0]
    sb = seqs_per_step
    steps = bd // sb
    assert bd % sb == 0 and t % steps == 0 and (t // steps) % HG_CHUNK == 0
    th = t // steps
    views = [_window_cache_view(caches[g], g) for g in range(A_GROUPS)]
    c_specs = [pl.BlockSpec((sb,) + v_.shape[1:], lambda b: (b, 0, 0, 0)) for v_ in views]
    rows = pl.BlockSpec((th, HG_W), lambda i: (i, 0))
    full = pl.BlockSpec((bd, A_W), lambda b: (0, 0))
    state = pl.BlockSpec((HG_HEADS, HG_K, HG_V), lambda i: (0, 0, 0))
    return pl.pallas_call(
        functools.partial(_hgrn_and_attn_sample_kernel, th=th, sb=sb),
        out_shape=[jax.ShapeDtypeStruct((t, HG_W), BF16), jax.ShapeDtypeStruct((HG_HEADS, HG_K, HG_V), F32),
                   jax.ShapeDtypeStruct((bd, A_W), F32), jax.ShapeDtypeStruct((t, XA_W), BF16)],
        grid=(steps,),
        in_specs=([rows] * 4 + [_resident((1, HG_W))] + [full] * 9 + c_specs
                  + [rows, _resident((MEM_LEN, XA_W)), _resident((MEM_LEN, XA_W))]),
        out_specs=[rows, state, full, rows],
        scratch_shapes=[pltpu.VMEM((HG_HEADS, HG_V, HG_K), F32)],
        compiler_params=_params(("arbitrary",), VMEM_LIMIT_BYTES),
        name="hgrn_prompt_attn_sample",
    )(q, lf, k, v, gain, *qs, *ks, *vs, *views, cq, mk, mv)


def _xattn_sample_body(step, q_ref, c_ref, o_ref, *, sb):
    stride = 2 * XA_HEADS
    for b in range(sb):
        row = pl.ds(step * sb + b, 1)
        q_row = q_ref[row, :]
        outs = []
        for h in range(XA_HEADS):
            sl = slice(h * XA_HEAD_DIM, (h + 1) * XA_HEAD_DIM)
            kh = c_ref[b, pl.ds(h, MEM_LEN, stride=stride), :]
            vh = c_ref[b, pl.ds(XA_HEADS + h, MEM_LEN, stride=stride), :]
            s = jnp.sum(kh * q_row[:, sl], axis=-1, keepdims=True) * (XA_HEAD_DIM ** -0.5)
            m = jnp.max(s, axis=0, keepdims=True)
            p = jnp.exp(s - m)
            l = jnp.sum(p, axis=0, keepdims=True)
            outs.append(jnp.sum(p * vh, axis=0, keepdims=True) * (1.0 / l))
        o_ref[row, :] = jnp.concatenate(outs, axis=-1)


def _xattn_sample_rider(cq, mem_cache, *, steps):
    bd = cq.shape[0]
    assert bd % steps == 0
    sb = bd // steps
    rows = MEM_LEN * 2 * XA_HEADS
    view = mem_cache.reshape(bd, rows, XA_HEAD_DIM)
    full = pl.BlockSpec((bd, XA_W), lambda i: (0, 0))
    return _Rider(
        body=functools.partial(_xattn_sample_body, sb=sb),
        inputs=(cq, view),
        in_specs=(full, pl.BlockSpec((sb, rows, XA_HEAD_DIM), lambda i: (i, 0, 0))),
        out_shapes=(jax.ShapeDtypeStruct((bd, XA_W), F32),),
        out_specs=(full,))


def _hgrn_sample_body(step, qt_ref, lft_ref, kt_ref, v_ref, s0_ref, gain_ref, o_ref, s_ref, *, sb):
    for b in range(sb):
        row = pl.ds(step * sb + b, 1)
        v_all = v_ref[row, :]
        outs = []
        for h in range(HG_HEADS):
            sl = slice(h * HG_K, (h + 1) * HG_K)
            d_col = jnp.exp(lft_ref[0, sl, b:b + 1])
            k_col = kt_ref[0, sl, b:b + 1]
            q_col = qt_ref[0, sl, b:b + 1]
            s_new = d_col * s0_ref[b, h] + k_col * v_all[:, sl]
            s_ref[b, h] = s_new
            o = jnp.sum(q_col * s_new, axis=0, keepdims=True)
            outs.append(o * lax.rsqrt(jnp.mean(o * o, axis=-1, keepdims=True) + EPS) * gain_ref[:, sl])
        o_ref[row, :] = jnp.concatenate(outs, axis=-1)


def _hgrn_sample_rider(q, lf, kin, v, s0, gain, *, steps):
    bd = q.shape[0]
    assert bd % steps == 0
    sb = bd // steps

    def cols(a):
        return a.reshape(steps, sb, HG_W).transpose(0, 2, 1)

    col_spec = pl.BlockSpec((1, HG_W, sb), lambda i: (i, 0, 0))
    st_spec = pl.BlockSpec((sb, HG_HEADS, HG_K, HG_V), lambda i: (i, 0, 0, 0))
    full = pl.BlockSpec((bd, HG_W), lambda i: (0, 0))
    return _Rider(
        body=functools.partial(_hgrn_sample_body, sb=sb),
        inputs=(cols(q), cols(lf), cols(kin), v, s0, gain),
        in_specs=(col_spec, col_spec, col_spec, full, st_spec, _resident((1, HG_W))),
        out_shapes=(jax.ShapeDtypeStruct((bd, HG_W), F32), jax.ShapeDtypeStruct(s0.shape, F32)),
        out_specs=(full, st_spec))


def kernel(x_prompt, x_sample, mem_prompt, cache_win128_kv, cache_win512_kv, cache_win2048_kv, state_hgrn, cache_mem_kv, norm_pre, norm_post, w_in, hgrn_lb_logits, hgrn_out_norm, mem_norm, w_mem_kv, w_branch_a, w_branch_b, w_branch_c, w_out):
    depth = w_in.shape[0]
    assert depth == 1, "single-layer trunk"
    layer = 0
    bp, t, _ = x_prompt.shape
    bd, s_len, _ = x_sample.shape
    assert bp == 1 and s_len == 1
    caches = (cache_win128_kv[layer], cache_win512_kv[layer], cache_win2048_kv[layer])

    w_att = w_rest = w_in[layer].astype(BF16)
    pre_g = norm_pre[layer].reshape(1, D_MODEL)
    post_g = norm_post[layer].reshape(1, D_MODEL)
    hg_g = hgrn_out_norm[layer].reshape(1, HG_W)
    w_pa, w_pb, w_pc, w_o = (w[layer].astype(BF16) for w in (w_branch_a, w_branch_b, w_branch_c, w_out))
    half = A_HEAD_DIM // 2
    inv = ROPE_THETA ** (-jnp.arange(0, A_HEAD_DIM, 2, dtype=F32) / A_HEAD_DIM)
    invf = jnp.tile(inv, LANES // half).reshape(1, LANES)

    xs = x_sample.reshape(bd, D_MODEL)
    q_s, k_s, v_s, tails_s, _ = _inproj_attn(xs, pre_g, w_att, invf, tm=bd, pos_base=PAST_LEN, pos_step=0,
                                             tail_rows=(bd,) * A_GROUPS, out_dtype=F32, row_dil=(1,) * A_GROUPS,
                                             q_scale=A_HEAD_DIM ** -0.5)
    (az_s, bq_s, lf_s, kin_s, bi_s, bz_s, cq_s, cz_s, ga_s, gb_s, gc_s), _ = _inproj_rest(
        xs, pre_g, w_rest, hgrn_lb_logits, tm=bd, layer=layer, hg_dtype=F32, cq_dtype=F32)

    xp = x_prompt.reshape(t, D_MODEL)
    tm_p = 512
    steps_p = t // tm_p
    tail_rows = tuple(min(w, t) for w in A_WINDOWS)
    q_p, k_p, v_p, tails_p, _ = _inproj_attn(
        xp, pre_g, w_att, invf, tm=tm_p, pos_base=0, pos_step=1, tail_rows=tail_rows, out_dtype=BF16,
        row_dil=A_DILATIONS, q_scale=A_HEAD_DIM ** -0.5 * LOG2E)
    (az, bq, lf, kin, bi, bz, cq, cz, ga, gb, gc), (ob_s, s_s, oc_s) = _inproj_rest(
        xp, pre_g, w_rest, hgrn_lb_logits, tm=tm_p, layer=layer, hg_dtype=BF16, cq_dtype=BF16,
        rider=_join_riders(_hgrn_sample_rider(bq_s, lf_s, kin_s, bi_s, state_hgrn[layer], hg_g, steps=steps_p),
                           _xattn_sample_rider(cq_s, cache_mem_kv[layer], steps=steps_p)))
    att = [_attn_prompt(q_p[g], k_p[g], v_p[g], A_DILATIONS[g], rows_per_step=2048) for g in range(A_GROUPS)]
    mem_kv = _mem_kv(mem_prompt.reshape(MEM_LEN, D_MODEL), mem_norm[layer].reshape(1, D_MODEL),
                     w_mem_kv[layer].astype(BF16))
    mk = mem_kv[:, :XA_W].astype(BF16)
    mv = mem_kv[:, XA_W:].astype(BF16)
    ob_p, s_p, oa_s, oc_p = _hgrn_and_attn_sample(bq, lf, kin, bi, hg_g, q_s, k_s, v_s, caches, cq, mk, mv,
                                                  seqs_per_step=2)
    y_p = _final([a[0] for a in att] + [a[1] for a in att], oc_p, ob_p, az, bz, cz, ga, gb, gc,
                 xp, post_g, w_pa, w_pb, w_pc, w_o, tm=512, merge=True)

    y_s = _final([oa_s], oc_s, ob_s, az_s, bz_s, cz_s, ga_s, gb_s, gc_s,
                 xs, post_g, w_pa, w_pb, w_pc, w_o, tm=bd, merge=False)

    new_win_p = [tails_p[g].reshape(1, 1, tail_rows[g], 2, A_HEADS, A_HEAD_DIM) for g in range(A_GROUPS)]
    new_win_s = [tails_s[g].reshape(1, bd, 1, 2, A_HEADS, A_HEAD_DIM) for g in range(A_GROUPS)]
    return (y_p.reshape(bp, t, D_MODEL), y_s.reshape(bd, 1, D_MODEL),
            new_win_p[0], new_win_p[1], new_win_p[2],
            s_p.reshape(1, 1, HG_HEADS, HG_K, HG_V),
            mem_kv.reshape(1, 1, MEM_LEN, 2, XA_HEADS, XA_HEAD_DIM),
            new_win_s[0], new_win_s[1], new_win_s[2],
            s_s.reshape(1, bd, HG_HEADS, HG_K, HG_V))
```

```python
import functools
from typing import Callable, NamedTuple

import jax
import jax.numpy as jnp
from jax import lax
from jax.experimental import pallas as pl
from jax.experimental.pallas import tpu as pltpu

D_MODEL = 1024
PAST_LEN = 8192
A_WINDOWS = (128, 512, 2048)
A_DILATIONS = (1, 4, 16)
A_GROUPS = 3
A_HEADS = 8
A_HEAD_DIM = 64
A_NKEY = 128
ROPE_THETA = 10000.0
HG_HEADS = 4
HG_K = 128
HG_V = 128
HG_CHUNK = 64
HG_BLOCK = 8
ATTN_BLOCKS_IN_FLIGHT = 4
MEM_LEN = 256
XA_HEADS = 4
XA_HEAD_DIM = 128
EPS = 1e-6

A_W = A_HEADS * A_HEAD_DIM
HG_W = HG_HEADS * HG_K
XA_W = XA_HEADS * XA_HEAD_DIM
N_ATT = 3 * A_GROUPS * A_W
COL_AZ = N_ATT
COL_BQ = COL_AZ + A_W
COL_BF = COL_BQ + HG_W
COL_BI = COL_BF + HG_W
COL_BZ = COL_BI + HG_W
COL_CQ = COL_BZ + HG_W
COL_CZ = COL_CQ + XA_W
COL_GA = COL_CZ + XA_W
COL_GB = COL_GA + D_MODEL
COL_GC = COL_GB + D_MODEL
N_IN = COL_GC + D_MODEL

LANES = 128
SUBLANES = 8
VMEM_LIMIT_BYTES = 56 * 1024 * 1024


class _Tiling(NamedTuple):
    proj_rows: int
    final_rows: int
    attn_rows: int
    fused_seqs: int


def _tiling(t, bd):
    tile = _Tiling(proj_rows=512, final_rows=512, attn_rows=2048, fused_seqs=2)
    assert t % tile.attn_rows == 0 and t % tile.proj_rows == 0 and t % tile.final_rows == 0
    assert bd % tile.fused_seqs == 0 and bd % (t // tile.proj_rows) == 0
    return tile

F32 = jnp.float32
BF16 = jnp.bfloat16
NEG_INF = float("-inf")


def _params(semantics, vmem=None):
    return pltpu.CompilerParams(dimension_semantics=semantics, vmem_limit_bytes=vmem)


def _resident(shape):
    return pl.BlockSpec(shape, lambda *_: (0,) * len(shape), pipeline_mode=pl.Buffered(1))


def _resident_columns(rows, col0, width):
    return pl.BlockSpec((pl.Element(rows), pl.Element(width)), lambda *_: (0, col0), pipeline_mode=pl.Buffered(1))


class _Rider(NamedTuple):
    body: Callable
    inputs: tuple = ()
    in_specs: tuple = ()
    out_shapes: tuple = ()
    out_specs: tuple = ()


_NO_RIDER = _Rider(body=lambda step: None)


def _join_riders(a, b):
    na_in, nb_in, na_out = len(a.inputs), len(b.inputs), len(a.out_shapes)

    def body(step, *refs):
        ins, outs = refs[:na_in + nb_in], refs[na_in + nb_in:]
        a.body(step, *ins[:na_in], *outs[:na_out])
        b.body(step, *ins[na_in:], *outs[na_out:])

    return _Rider(body, a.inputs + b.inputs, a.in_specs + b.in_specs, a.out_shapes + b.out_shapes,
                  a.out_specs + b.out_specs)


def _rmsnorm_rows(x, gain):
    ms = jnp.mean(x * x, axis=-1, keepdims=True)
    return x * lax.rsqrt(ms + EPS) * gain


def _sigmoid(x):
    return 0.5 * jnp.tanh(0.5 * x) + 0.5


def _silu(x):
    return x * _sigmoid(x)


def _tile_row_order(tm, d):
    j = lax.broadcasted_iota(jnp.int32, (tm, LANES), 0)
    per = tm // d
    return (j % per) * d + j // per


def _inproj_attn_kernel(x_ref, g_ref, w_ref, invf_ref, *refs, tm, n_tiles, pos_base, pos_step, tails, row_dil,
                        rider_body, n_rider_in, n_rider_out):
    rider_in, refs = refs[:n_rider_in], refs[n_rider_in:]
    q_refs = refs[0:3]
    k_refs = refs[3:6]
    v_refs = refs[6:9]
    t_refs = refs[9:12]
    rider_out = refs[12:12 + n_rider_out]
    cos_tab, sin_tab, xa_ref, xb_ref, xp_ref = refs[12 + n_rider_out:]
    i = pl.program_id(0)
    n_slab = A_W // LANES
    n_xslab = D_MODEL // LANES
    dils = sorted(set(row_dil))

    @pl.when(i == 0)
    def _():
        for t_idx, d in enumerate(dils):
            ang0 = (_tile_row_order(tm, d) * pos_step).astype(F32) * invf_ref[...]
            cos_tab[t_idx] = jnp.cos(ang0)
            sin_tab[t_idx] = jnp.sin(ang0)

    rider_body(i, *rider_in, *rider_out)

    lane = lax.broadcasted_iota(jnp.int32, (tm, LANES), 1)
    first_half = (lane % A_HEAD_DIM) < (A_HEAD_DIM // 2)
    base =(pos_base + i * tm * pos_step).astype(F32) * invf_ref[...]
    cos_b = jnp.cos(base)
    sin_b = jnp.sin(base)

    def trig(d):
        t_idx = dils.index(d)
        c0, s0 = cos_tab[t_idx], sin_tab[t_idx]
        cos = c0 * cos_b - s0 * sin_b
        sin = s0 * cos_b + c0 * sin_b
        return cos, jnp.where(first_half, -sin, sin)

    def rope(acc, cos, sin_signed):
        outs = []
        for c in range(n_slab):
            xc = acc[:, c * LANES:(c + 1) * LANES]
            partner = jnp.where(first_half, pltpu.roll(xc, LANES - 32, 1), pltpu.roll(xc, 32, 1))
            outs.append(xc * cos + partner * sin_signed)
        return jnp.concatenate(outs, axis=-1)

    xn32 = _rmsnorm_rows(x_ref[...], g_ref[...])
    xn = xn32.astype(BF16)

    lhs = {1: xn}
    if max(row_dil) > 1:
        assert set(row_dil) <= {1, 4, 16}
        per4 = tm // 4
        for c in range(n_xslab):
            xa_ref[c] = xn32[:, c * LANES:(c + 1) * LANES]
        for c in range(n_xslab):
            for r in range(4):
                blk4 = xa_ref[c, pl.ds(r, per4, stride=4), :]
                xb_ref[c, r * per4:(r + 1) * per4, :] = blk4
                xp_ref[0, r * per4:(r + 1) * per4, c * LANES:(c + 1) * LANES] = blk4.astype(BF16)
        lhs[4] = xp_ref[0]
        if 16 in row_dil:
            per16 = tm // 16
            for c in range(n_xslab):
                for r1 in range(4):
                    for r2 in range(4):
                        res = 4 * r2 + r1
                        xp_ref[1, res * per16:(res + 1) * per16, c * LANES:(c + 1) * LANES] = (
                            xb_ref[c, pl.ds(r1 * per4 + r2, per16, stride=4), :].astype(BF16))
            lhs[16] = xp_ref[1]

    def store_rows(ref, d, val):
        per = tm // d
        for r in range(d):
            ref[:, r * A_W:(r + 1) * A_W] = val[r * per:(r + 1) * per, :].astype(ref.dtype)

    def tail_block(g, col0, natural_val):
        first_tile, rows = tails[g]
        cond = (i >= first_tile) if rows >= tm else (i == n_tiles - 1)

        @pl.when(cond)
        def _():
            val = natural_val()
            t_refs[g][:, col0:col0 + A_W] = val if rows >= tm else val[tm - rows:, :]

    cos_n, sin_n = trig(1)
    for g in range(A_GROUPS):
        d = row_dil[g]
        cos, sin_s = (cos_n, sin_n) if d == 1 else trig(d)
        wq = w_ref[:, g * A_W:(g + 1) * A_W]
        wk = w_ref[:, (A_GROUPS + g) * A_W:(A_GROUPS + g + 1) * A_W]
        wv = w_ref[:, (2 * A_GROUPS + g) * A_W:(2 * A_GROUPS + g + 1) * A_W]
        q = rope(jnp.dot(lhs[d], wq, preferred_element_type=F32), cos, sin_s) * (A_HEAD_DIM ** -0.5)
        store_rows(q_refs[g], d, q)
        kr = rope(jnp.dot(lhs[d], wk, preferred_element_type=F32), cos, sin_s)
        store_rows(k_refs[g], d, kr)
        v = jnp.dot(lhs[d], wv, preferred_element_type=F32)
        store_rows(v_refs[g], d, v)
        if d == 1:
            tail_block(g, 0, lambda kr=kr: kr)
            tail_block(g, A_W, lambda v=v: v)
        else:
            tail_block(g, 0, lambda wk=wk: rope(jnp.dot(xn, wk, preferred_element_type=F32), cos_n, sin_n))
            tail_block(g, A_W, lambda wv=wv: jnp.dot(xn, wv, preferred_element_type=F32))


def _inproj_attn(x, gain, w_att, invf, *, tm, pos_base, pos_step, tail_rows, out_dtype, row_dil, rider=_NO_RIDER):
    n = x.shape[0]
    n_tiles = n // tm
    assert 1 in row_dil
    n_orders = len(set(row_dil))
    tails = []
    tail_specs = []
    tail_shapes = []
    for g in range(A_GROUPS):
        rows = tail_rows[g]
        first_tile = (n - rows) // tm
        tails.append((first_tile, rows))
        blk = min(tm, rows)
        tail_specs.append(pl.BlockSpec((blk, 2 * A_W), functools.partial(
            lambda i, ft: (jnp.maximum(i - ft, 0), 0), ft=first_tile)))
        tail_shapes.append(jax.ShapeDtypeStruct((rows, 2 * A_W), F32))
    row_specs = [pl.BlockSpec((tm // d, d * A_W), lambda i: (i, 0)) for d in row_dil] * 3
    row_shapes = [jax.ShapeDtypeStruct((n // d, d * A_W), out_dtype) for d in row_dil] * 3
    kernel = functools.partial(_inproj_attn_kernel, tm=tm, n_tiles=n_tiles, pos_base=pos_base,
                               pos_step=pos_step, tails=tuple(tails), row_dil=tuple(row_dil),
                               rider_body=rider.body, n_rider_in=len(rider.inputs), n_rider_out=len(rider.out_shapes))
    outs = pl.pallas_call(
        kernel,
        out_shape=row_shapes + tail_shapes + list(rider.out_shapes),
        grid=(n_tiles,),
        in_specs=[pl.BlockSpec((tm, D_MODEL), lambda i: (i, 0)),
                  _resident((1, D_MODEL)),
                  _resident_columns(D_MODEL, 0, N_ATT),
                  _resident((1, LANES))] + list(rider.in_specs),
        out_specs=row_specs + tail_specs + list(rider.out_specs),
        scratch_shapes=[pltpu.VMEM((n_orders, tm, LANES), F32), pltpu.VMEM((n_orders, tm, LANES), F32),
                        pltpu.VMEM((D_MODEL // LANES, tm, LANES), F32), pltpu.VMEM((D_MODEL // LANES, tm, LANES), F32),
                        pltpu.VMEM((2, tm, D_MODEL), BF16)],
        compiler_params=_params(("arbitrary",), VMEM_LIMIT_BYTES),
        name="inproj_attn",
    )(x, gain, w_att, invf, *rider.inputs)
    return outs[0:3], outs[3:6], outs[6:9], outs[9:12], outs[12:]


def _inproj_rest_kernel(x_ref, g_ref, w_ref, lbl_ref, *refs, layer, rider_body, n_rider_in):
    rider_in, refs = refs[:n_rider_in], refs[n_rider_in:]
    (az_ref, bq_ref, lf_ref, kin_ref, bi_ref, bz_ref, cq_ref, cz_ref, ga_ref, gb_ref, gc_ref) = refs[:11]
    rider_body(pl.program_id(0), *rider_in, *refs[11:])
    xn = _rmsnorm_rows(x_ref[...], g_ref[...]).astype(BF16)

    def proj(col, width):
        c0 = col - N_ATT
        return jnp.dot(xn, w_ref[:, c0:c0 + width], preferred_element_type=F32)

    logits = lbl_ref[...]
    e = jnp.exp(logits - jnp.max(logits, axis=0, keepdims=True))
    lb = jnp.sum(e[0:layer + 1, :], axis=0, keepdims=True) / jnp.sum(e, axis=0, keepdims=True)

    az_ref[...] = _silu(proj(COL_AZ, A_W)).astype(az_ref.dtype)
    bq_ref[...] = proj(COL_BQ, HG_W).astype(bq_ref.dtype)
    bf = proj(COL_BF, HG_W)
    gate = _sigmoid(bf)
    lf_ref[...] = jnp.log(lb + (1.0 - lb) * gate)
    kin_ref[...] = ((1.0 - lb) * (1.0 - gate)).astype(kin_ref.dtype)
    bi_ref[...] = proj(COL_BI, HG_W).astype(bi_ref.dtype)
    bz_ref[...] = _silu(proj(COL_BZ, HG_W)).astype(bz_ref.dtype)
    cq_ref[...] = proj(COL_CQ, XA_W).astype(cq_ref.dtype)
    cz_ref[...] = _silu(proj(COL_CZ, XA_W)).astype(cz_ref.dtype)
    ga_ref[...] = _sigmoid(proj(COL_GA, D_MODEL)).astype(ga_ref.dtype)
    gb_ref[...] = _sigmoid(proj(COL_GB, D_MODEL)).astype(gb_ref.dtype)
    gc_ref[...] = _sigmoid(proj(COL_GC, D_MODEL)).astype(gc_ref.dtype)


def _inproj_rest(x, gain, w_rest, lb_logits, *, tm, layer, hg_dtype, cq_dtype, rider=_NO_RIDER):
    n = x.shape[0]
    n_rest = N_IN - N_ATT
    s512 = pl.BlockSpec((tm, A_W), lambda i: (i, 0))
    s1024 = pl.BlockSpec((tm, D_MODEL), lambda i: (i, 0))

    def sds(width, dt):
        return jax.ShapeDtypeStruct((n, width), dt)

    outs = pl.pallas_call(
        functools.partial(_inproj_rest_kernel, layer=layer, rider_body=rider.body, n_rider_in=len(rider.inputs)),
        out_shape=[sds(A_W, BF16), sds(HG_W, hg_dtype), sds(HG_W, F32), sds(HG_W, hg_dtype), sds(HG_W, hg_dtype),
                   sds(HG_W, BF16), sds(XA_W, cq_dtype), sds(XA_W, BF16),
                   sds(D_MODEL, BF16), sds(D_MODEL, BF16), sds(D_MODEL, BF16)] + list(rider.out_shapes),
        grid=(n // tm,),
        in_specs=[pl.BlockSpec((tm, D_MODEL), lambda i: (i, 0)),
                  _resident((1, D_MODEL)),
                  _resident_columns(D_MODEL, N_ATT, n_rest),
                  _resident(lb_logits.shape)] + list(rider.in_specs),
        out_specs=[s512] * 8 + [s1024] * 3 + list(rider.out_specs),
        compiler_params=_params(("arbitrary",), VMEM_LIMIT_BYTES),
        name="inproj_rest",
    )(x, gain, w_rest, lb_logits, *rider.inputs)
    return outs[:11], outs[11:]


def _attn_prompt_kernel(q_ref, kc_ref, vc_ref, kp_ref, vp_ref, o_ref, st_ref, kbuf, vbuf, o_acc, st_acc,
                        *, tmv, dil, rps):
    n = pl.program_id(0)
    rstep = pl.program_id(1)
    blk = A_NKEY

    qi = lax.broadcasted_iota(jnp.int32, (blk, 2 * blk), 0)
    ki = lax.broadcasted_iota(jnp.int32, (blk, 2 * blk), 1)
    dist = qi + blk - ki
    band = (dist >= 0) & (dist <= A_NKEY)
    bias = jnp.where(band, 0.0, NEG_INF).astype(F32)
    bias_first = jnp.where(band & (ki >= blk), 0.0, NEG_INF).astype(F32)
    lane = lax.broadcasted_iota(jnp.int32, (blk, LANES), 1)
    low_head = lane < A_HEAD_DIM
    head_mask = (jnp.where(low_head, 1.0, 0.0).astype(BF16), jnp.where(low_head, 0.0, 1.0).astype(BF16))

    for rr in range(rps):
        cs = slice(rr * A_W, (rr + 1) * A_W)
        kbuf[rr, 0:blk, :] = kp_ref[:, cs]
        kbuf[rr, blk:, :] = kc_ref[:, cs]
        vbuf[rr, 0:blk, :] = vp_ref[:, cs]
        vbuf[rr, blk:, :] = vc_ref[:, cs]
        res = rstep * rps + rr

        def body(b, carry, cs=cs, res=res, rr=rr):
            r0 = pl.multiple_of(b * blk, blk)
            is_first = jnp.logical_and(n == 0, b == 0)
            bias_b = jnp.where(is_first, bias_first, bias)
            qb = q_ref[pl.ds(r0, blk), cs]
            kb = kbuf[rr, pl.ds(r0, 2 * blk), :]
            vb = vbuf[rr, pl.ds(r0, 2 * blk), :]
            out_rows = pl.ds(r0 * dil + res, blk, stride=dil)
            stats = jnp.zeros((blk, LANES), F32)
            for pair in range(A_HEADS // 2):
                sl = slice(pair * LANES, (pair + 1) * LANES)
                qp = qb[:, sl]
                kp = kb[:, sl]
                vp = vb[:, sl]
                outs = []
                for hh in range(2):
                    qm = qp * head_mask[hh]
                    s = pl.dot(qm, kp, trans_b=True) + bias_b
                    m = jnp.max(s, axis=-1, keepdims=True)
                    p = jnp.exp(s - m)
                    l = jnp.sum(p, axis=-1, keepdims=True)
                    pv = jnp.dot(p.astype(BF16), vp, preferred_element_type=F32)
                    outs.append(pv * (1.0 / l))
                    lse = m + jnp.log(l)
                    stats = jnp.where(lane == 2 * pair + hh, lse, stats)
                o_acc[pair, out_rows, :] = jnp.where(low_head, outs[0], outs[1])
            st_acc[out_rows, :] = stats
            return carry

        lax.fori_loop(0, tmv // blk, body, 0, unroll=min(ATTN_BLOCKS_IN_FLIGHT, tmv // blk))

    @pl.when(rstep == pl.num_programs(1) - 1)
    def _():
        for pair in range(A_HEADS // 2):
            o_ref[:, pair * LANES:(pair + 1) * LANES] = o_acc[pair].astype(o_ref.dtype)
        st_ref[...] = st_acc[...]


def _attn_prompt(qv, kv, vv, dil, *, rows_per_step):
    l = qv.shape[0]
    t = l * dil
    blk = A_NKEY
    tmv = max(rows_per_step // dil, blk)
    rps = min(dil, max(1, ATTN_BLOCKS_IN_FLIGHT * blk // tmv))
    assert t % (tmv * dil) == 0 and dil % rps == 0
    cur = pl.BlockSpec((tmv, rps * A_W), lambda n, r: (n, r))
    prev = pl.BlockSpec((blk, rps * A_W), lambda n, r: (jnp.maximum(n * (tmv // blk) - 1, 0), r))
    rows = tmv * dil
    return pl.pallas_call(
        functools.partial(_attn_prompt_kernel, tmv=tmv, dil=dil, rps=rps),
        out_shape=[jax.ShapeDtypeStruct((t, A_W), BF16), jax.ShapeDtypeStruct((t, LANES), F32)],
        grid=(t // rows, dil // rps),
        in_specs=[cur, cur, cur, prev, prev],
        out_specs=[pl.BlockSpec((rows, A_W), lambda n, r: (n, 0)), pl.BlockSpec((rows, LANES), lambda n, r: (n, 0))],
        scratch_shapes=[pltpu.VMEM((rps, tmv + blk, A_W), BF16), pltpu.VMEM((rps, tmv + blk, A_W), BF16),
                        pltpu.VMEM((A_HEADS // 2, rows, LANES), F32), pltpu.VMEM((rows, LANES), F32)],
        compiler_params=_params(("parallel", "arbitrary"), VMEM_LIMIT_BYTES),
        name=f"attn_prompt_d{dil}",
    )(qv, kv, vv, kv, vv)


def _hgrn_prompt_kernel(q_ref, lf_ref, k_ref, v_ref, gain_ref, o_ref, s_out_ref, st_ref, *, th,
                        co_scheduled=lambda: None):
    i = pl.program_id(0)
    c_rows = HG_CHUNK
    nblk = c_rows // HG_BLOCK
    shape3 = (nblk, HG_BLOCK, HG_K)

    @pl.when(i == 0)
    def _():
        st_ref[...] = jnp.zeros_like(st_ref)

    co_scheduled()

    sub = lax.broadcasted_iota(jnp.int32, shape3, 1)
    a_row = lax.broadcasted_iota(jnp.int32, (c_rows, c_rows), 0) // HG_BLOCK
    a_col = lax.broadcasted_iota(jnp.int32, (c_rows, c_rows), 1) // HG_BLOCK

    def shift(x3, s):
        return pltpu.roll(x3, s, 1)

    def chunk(c, carry):
        r0 = pl.multiple_of(c * c_rows, c_rows)
        for h in range(HG_HEADS):
            sl = slice(h * HG_K, (h + 1) * HG_K)
            q = q_ref[pl.ds(r0, c_rows), sl].astype(F32)
            k = k_ref[pl.ds(r0, c_rows), sl].astype(F32)
            v = v_ref[pl.ds(r0, c_rows), sl].astype(F32)
            q3, k3, v3 = (a.reshape(shape3) for a in (q, k, v))
            cum = lf_ref[pl.ds(r0, c_rows), sl].reshape(shape3)
            s = 1
            while s < HG_BLOCK:
                cum = cum + jnp.where(sub >= s, shift(cum, s), 0.0)
                s *= 2
            anchors = [jnp.zeros((1, 1, HG_K), F32)]
            for j in range(nblk):
                anchors.append(anchors[j] + cum[j:j + 1, HG_BLOCK - 1:HG_BLOCK, :])
            b3 = cum + jnp.concatenate(anchors[:nblk], axis=0)
            b = b3.reshape(c_rows, HG_K)
            b_last = anchors[nblk].reshape(1, HG_K)

            o3 = jnp.sum(q3 * k3, axis=-1, keepdims=True) * v3
            for d in range(1, HG_BLOCK):
                dec = jnp.exp(jnp.where(sub >= d, cum - shift(cum, d), NEG_INF))
                a = jnp.sum(q3 * shift(k3, d) * dec, axis=-1, keepdims=True)
                o3 = o3 + a * shift(v3, d)
            o = o3.reshape(c_rows, HG_K)

            b_end = jnp.concatenate(anchors[1:], axis=0)
            k_hat = (k3 * jnp.exp(b_end - b3)).reshape(c_rows, HG_K).astype(BF16)
            q_ver = [(q3 * jnp.exp(jnp.minimum(b3 - anchors[j + 1], 0.0))).reshape(c_rows, HG_K).astype(BF16)
                     for j in range(nblk - 1)]
            r = pl.dot(jnp.concatenate(q_ver, axis=0), k_hat, trans_b=True)
            att = jnp.zeros((c_rows, c_rows), F32)
            for j in range(nblk - 1):
                att = jnp.where(a_col == j, r[j * c_rows:(j + 1) * c_rows, :], att)
            att = jnp.where(a_row > a_col, att, 0.0)
            vb = v.astype(BF16)
            o = o + jnp.dot(att.astype(BF16), vb, preferred_element_type=F32)

            st = st_ref[h]
            o = o + pl.dot((q * jnp.exp(b)).astype(BF16), st.astype(BF16), trans_b=True)
            k_end = (k * jnp.exp(b_last - b)).astype(BF16)
            st_ref[h] = st * jnp.exp(b_last) + pl.dot(vb, k_end, trans_a=True)

            on = o * lax.rsqrt(jnp.mean(o * o, axis=-1, keepdims=True) + EPS) * gain_ref[:, sl]
            o_ref[pl.ds(r0, c_rows), sl] = on.astype(o_ref.dtype)
        return carry

    n_chunks = th // c_rows
    lax.fori_loop(0, n_chunks, chunk, 0, unroll=n_chunks <= 4)

    @pl.when(i == pl.num_programs(0) - 1)
    def _():
        for h in range(HG_HEADS):
            s_out_ref[h] = st_ref[h].T


def _mem_kv_kernel(m_ref, g_ref, w_ref, o_ref):
    xn = _rmsnorm_rows(m_ref[...], g_ref[...]).astype(BF16)
    o_ref[...] = jnp.dot(xn, w_ref[...], preferred_element_type=F32)


def _mem_kv(mem, gain, w):
    m = mem.shape[0]
    return pl.pallas_call(
        _mem_kv_kernel,
        out_shape=jax.ShapeDtypeStruct((m, 2 * XA_W), F32),
        name="mem_kv",
    )(mem, gain, w)


def _softmax_rows(s):
    m = jnp.max(s, axis=-1, keepdims=True)
    p = jnp.exp(s - m)
    return p, jnp.sum(p, axis=-1, keepdims=True)


def _xattn_rows(cq, mk_ref, mv_ref):
    parts = []
    for h in range(XA_HEADS):
        sl = slice(h * XA_HEAD_DIM, (h + 1) * XA_HEAD_DIM)
        s = pl.dot(cq[:, sl], mk_ref[:, sl], trans_b=True) * (XA_HEAD_DIM ** -0.5)
        p, l = _softmax_rows(s)
        parts.append(jnp.dot(p.astype(BF16), mv_ref[:, sl], preferred_element_type=F32) * (1.0 / l))
    return jnp.concatenate(parts, axis=-1)


def _final_kernel(*refs, tm, merge, xattn):
    refs = list(refs)
    if merge:
        o_refs = [refs.pop(0) for _ in range(A_GROUPS)]
        st_refs = [refs.pop(0) for _ in range(A_GROUPS)]
    else:
        oa_ref = refs.pop(0)
    if xattn:
        cq_ref, mk_ref, mv_ref = (refs.pop(0) for _ in range(3))
    else:
        oc_ref = refs.pop(0)
    (ob_ref, az_ref, bz_ref, cz_ref, ga_ref, gb_ref, gc_ref, x_ref, pg_ref,
     wa_ref, wb_ref, wc_ref, wo_ref, out_ref) = refs

    if merge:
        lse = [r[...] for r in st_refs]
        mx = jnp.maximum(jnp.maximum(lse[0], lse[1]), lse[2])
        ex = [jnp.exp(x - mx) for x in lse]
        inv = 1.0 / (ex[0] + ex[1] + ex[2])
        e_row = lax.broadcasted_iota(jnp.int32, (LANES, A_W), 0)
        e_col = lax.broadcasted_iota(jnp.int32, (LANES, A_W), 1) // A_HEAD_DIM
        spread = jnp.where(e_row == e_col, 1.0, 0.0).astype(BF16)
        oa = jnp.zeros((tm, A_W), F32)
        for g in range(A_GROUPS):
            alpha = ex[g] * inv
            hi = alpha.astype(BF16)
            lo = (alpha - hi.astype(F32)).astype(BF16)
            w = (jnp.dot(hi, spread, preferred_element_type=F32)
                 + jnp.dot(lo, spread, preferred_element_type=F32))
            oa = oa + w * o_refs[g][...].astype(F32)
    else:
        oa = oa_ref[...].astype(F32)
    ya = (oa * az_ref[...].astype(F32)).astype(BF16)

    oc = _xattn_rows(cq_ref[...], mk_ref, mv_ref) if xattn else oc_ref[...].astype(F32)
    yc = (oc * cz_ref[...].astype(F32)).astype(BF16)

    yb = (ob_ref[...].astype(F32) * bz_ref[...].astype(F32)).astype(BF16)

    merged = (ga_ref[...].astype(F32) * jnp.dot(ya, wa_ref[...], preferred_element_type=F32)
              + gb_ref[...].astype(F32) * jnp.dot(yb, wb_ref[...], preferred_element_type=F32)
              + gc_ref[...].astype(F32) * jnp.dot(yc, wc_ref[...], preferred_element_type=F32))
    z = jnp.dot(merged.astype(BF16), wo_ref[...], preferred_element_type=F32)
    out_ref[...] = x_ref[...] + _rmsnorm_rows(z, pg_ref[...])


def _final(att_in, xa_in, ob, az, bz, cz, ga, gb, gc, x, post_g, w_pa, w_pb, w_pc, w_out, *, tm, merge, xattn):
    n = x.shape[0]

    def rows(width):
        return pl.BlockSpec((tm, width), lambda i: (i, 0))

    if merge:
        att_specs = [rows(A_W)] * A_GROUPS + [rows(LANES)] * A_GROUPS
    else:
        att_specs = [rows(A_W)]
    if xattn:
        xa_specs = [rows(XA_W), _resident((MEM_LEN, XA_W)), _resident((MEM_LEN, XA_W))]
    else:
        xa_specs = [rows(XA_W)]
    in_specs = (att_specs + xa_specs + [rows(HG_W), rows(A_W), rows(HG_W), rows(XA_W)]
                + [rows(D_MODEL)] * 4 + [_resident((1, D_MODEL))]
                + [_resident((A_W, D_MODEL)), _resident((HG_W, D_MODEL)), _resident((XA_W, D_MODEL)),
                   _resident((D_MODEL, D_MODEL))])
    return pl.pallas_call(
        functools.partial(_final_kernel, tm=tm, merge=merge, xattn=xattn),
        out_shape=jax.ShapeDtypeStruct((n, D_MODEL), F32),
        grid=(n // tm,),
        in_specs=in_specs,
        out_specs=rows(D_MODEL),
        compiler_params=_params(("parallel",), VMEM_LIMIT_BYTES),
        name="final_merge" if merge else "final",
    )(*att_in, *xa_in, ob, az, bz, cz, ga, gb, gc, x, post_g, w_pa, w_pb, w_pc, w_out)


def _head_rows(q_row, head_dim):
    w = q_row.shape[-1]
    assert w // head_dim <= SUBLANES
    sub = lax.broadcasted_iota(jnp.int32, (SUBLANES, w), 0)
    lane_head = lax.broadcasted_iota(jnp.int32, (SUBLANES, w), 1) // head_dim
    keep = sub == lane_head
    return jnp.where(keep, jnp.broadcast_to(q_row, (SUBLANES, w)), 0.0), keep


def _window_cache_view(cache, g):
    bd, wb = cache.shape[0], cache.shape[1]
    assert wb == A_NKEY * A_DILATIONS[g], "window buffer must hold exactly 128 dilated keys"
    return jnp.transpose(cache, (0, 2, 3, 4, 1)).reshape(bd, 2, A_W, wb)


def _attn_sample_group(dil, q_row, k_new, v_new, c_ref):
    wb = c_ref.shape[-1]
    qm, _ = _head_rows(q_row, A_HEAD_DIM)
    kt = c_ref[0].astype(BF16)
    vt = c_ref[1].astype(BF16)
    s = jnp.dot(qm.astype(BF16), kt, preferred_element_type=F32)
    w_pos = lax.broadcasted_iota(jnp.int32, (SUBLANES, wb), 1)
    s = jnp.where(w_pos % dil == 0, s, NEG_INF)
    s_new = jnp.sum(qm * k_new, axis=-1, keepdims=True)
    m = jnp.maximum(jnp.max(s, axis=-1, keepdims=True), s_new)
    p = jnp.exp(s - m)
    p_new = jnp.exp(s_new - m)
    l = jnp.sum(p, axis=-1, keepdims=True) + p_new
    o = (pl.dot(p.astype(BF16), vt, trans_b=True) + p_new * v_new) * (1.0 / l)
    return o, m + jnp.log(l)


def _attn_sample_kernel(q0, q1, q2, k0, k1, k2, v0, v1, v2, c0, c1, c2, o_ref, *, sb):
    q_refs, k_refs, v_refs, c_refs = (q0, q1, q2), (k0, k1, k2), (v0, v1, v2), (c0, c1, c2)
    _, keep = _head_rows(jnp.zeros((1, A_W), F32), A_HEAD_DIM)
    for b in range(sb):
        row = pl.ds(pl.program_id(0) * sb + b, 1)
        outs = []
        lses = []
        for g in range(A_GROUPS):
            o, lse = _attn_sample_group(A_DILATIONS[g], q_refs[g][row, :], k_refs[g][row, :], v_refs[g][row, :],
                                        c_refs[g].at[b])
            outs.append(o)
            lses.append(lse)
        mx = jnp.maximum(jnp.maximum(lses[0], lses[1]), lses[2])
        ex = [jnp.exp(x - mx) for x in lses]
        inv = 1.0 / (ex[0] + ex[1] + ex[2])
        oa = (ex[0] * outs[0] + ex[1] * outs[1] + ex[2] * outs[2]) * inv
        o_ref[row, :] = jnp.sum(jnp.where(keep, oa, 0.0), axis=0, keepdims=True)


def _hgrn_and_attn_sample_kernel(*refs, th, sb):
    hg_in, at_in = refs[0:5], refs[5:17]
    hg_out, at_out, hg_state = refs[17:19], refs[19], refs[20]
    _hgrn_prompt_kernel(*hg_in, *hg_out, hg_state, th=th,
                        co_scheduled=lambda: _attn_sample_kernel(*at_in, at_out, sb=sb))


def _hgrn_and_attn_sample(q, lf, k, v, gain, qs, ks, vs, caches, *, seqs_per_step):
    t = q.shape[0]
    bd = qs[0].shape[0]
    sb = seqs_per_step
    steps = bd // sb
    assert bd % sb == 0 and t % steps == 0 and (t // steps) % HG_CHUNK == 0
    th = t // steps
    views = [_window_cache_view(caches[g], g) for g in range(A_GROUPS)]
    c_specs = [pl.BlockSpec((sb,) + v_.shape[1:], lambda b: (b, 0, 0, 0)) for v_ in views]
    rows = pl.BlockSpec((th, HG_W), lambda i: (i, 0))
    full = pl.BlockSpec((bd, A_W), lambda b: (0, 0))
    state = pl.BlockSpec((HG_HEADS, HG_K, HG_V), lambda i: (0, 0, 0))
    return pl.pallas_call(
        functools.partial(_hgrn_and_attn_sample_kernel, th=th, sb=sb),
        out_shape=[jax.ShapeDtypeStruct((t, HG_W), BF16), jax.ShapeDtypeStruct((HG_HEADS, HG_K, HG_V), F32),
                   jax.ShapeDtypeStruct((bd, A_W), F32)],
        grid=(steps,),
        in_specs=[rows] * 4 + [_resident((1, HG_W))] + [full] * 9 + c_specs,
        out_specs=[rows, state, full],
        scratch_shapes=[pltpu.VMEM((HG_HEADS, HG_V, HG_K), F32)],
        compiler_params=_params(("arbitrary",), VMEM_LIMIT_BYTES),
        name="hgrn_prompt_attn_sample",
    )(q, lf, k, v, gain, *qs, *ks, *vs, *views)


def _xattn_sample_body(step, q_ref, c_ref, o_ref, *, sb):
    stride = 2 * XA_HEADS
    for b in range(sb):
        row = pl.ds(step * sb + b, 1)
        q_row = q_ref[row, :]
        outs = []
        for h in range(XA_HEADS):
            sl = slice(h * XA_HEAD_DIM, (h + 1) * XA_HEAD_DIM)
            kh = c_ref[b, pl.ds(h, MEM_LEN, stride=stride), :]
            vh = c_ref[b, pl.ds(XA_HEADS + h, MEM_LEN, stride=stride), :]
            s = jnp.sum(kh * q_row[:, sl], axis=-1, keepdims=True) * (XA_HEAD_DIM ** -0.5)
            m = jnp.max(s, axis=0, keepdims=True)
            p = jnp.exp(s - m)
            l = jnp.sum(p, axis=0, keepdims=True)
            outs.append(jnp.sum(p * vh, axis=0, keepdims=True) * (1.0 / l))
        o_ref[row, :] = jnp.concatenate(outs, axis=-1)


def _xattn_sample_rider(cq, mem_cache, *, steps):
    bd = cq.shape[0]
    assert bd % steps == 0
    sb = bd // steps
    rows = MEM_LEN * 2 * XA_HEADS
    view = mem_cache.reshape(bd, rows, XA_HEAD_DIM)
    full = pl.BlockSpec((bd, XA_W), lambda i: (0, 0))
    return _Rider(
        body=functools.partial(_xattn_sample_body, sb=sb),
        inputs=(cq, view),
        in_specs=(full, pl.BlockSpec((sb, rows, XA_HEAD_DIM), lambda i: (i, 0, 0))),
        out_shapes=(jax.ShapeDtypeStruct((bd, XA_W), F32),),
        out_specs=(full,))


def _hgrn_sample_body(step, qt_ref, lft_ref, kt_ref, v_ref, s0_ref, gain_ref, o_ref, s_ref, *, sb):
    for b in range(sb):
        row = pl.ds(step * sb + b, 1)
        v_all = v_ref[row, :]
        outs = []
        for h in range(HG_HEADS):
            sl = slice(h * HG_K, (h + 1) * HG_K)
            d_col = jnp.exp(lft_ref[0, sl, b:b + 1])
            k_col = kt_ref[0, sl, b:b + 1]
            q_col = qt_ref[0, sl, b:b + 1]
            s_new = d_col * s0_ref[b, h] + k_col * v_all[:, sl]
            s_ref[b, h] = s_new
            o = jnp.sum(q_col * s_new, axis=0, keepdims=True)
            outs.append(o * lax.rsqrt(jnp.mean(o * o, axis=-1, keepdims=True) + EPS) * gain_ref[:, sl])
        o_ref[row, :] = jnp.concatenate(outs, axis=-1)


def _hgrn_sample_rider(q, lf, kin, v, s0, gain, *, steps):
    bd = q.shape[0]
    assert bd % steps == 0
    sb = bd // steps

    def cols(a):
        return a.reshape(steps, sb, HG_W).transpose(0, 2, 1)

    col_spec = pl.BlockSpec((1, HG_W, sb), lambda i: (i, 0, 0))
    st_spec = pl.BlockSpec((sb, HG_HEADS, HG_K, HG_V), lambda i: (i, 0, 0, 0))
    full = pl.BlockSpec((bd, HG_W), lambda i: (0, 0))
    return _Rider(
        body=functools.partial(_hgrn_sample_body, sb=sb),
        inputs=(cols(q), cols(lf), cols(kin), v, s0, gain),
        in_specs=(col_spec, col_spec, col_spec, full, st_spec, _resident((1, HG_W))),
        out_shapes=(jax.ShapeDtypeStruct((bd, HG_W), F32), jax.ShapeDtypeStruct(s0.shape, F32)),
        out_specs=(full, st_spec))


def kernel(x_prompt, x_sample, mem_prompt, cache_win128_kv, cache_win512_kv, cache_win2048_kv, state_hgrn, cache_mem_kv, norm_pre, norm_post, w_in, hgrn_lb_logits, hgrn_out_norm, mem_norm, w_mem_kv, w_branch_a, w_branch_b, w_branch_c, w_out):
    depth = w_in.shape[0]
    assert depth == 1, "single-layer trunk"
    layer = 0
    bp, t, _ = x_prompt.shape
    bd, s_len, _ = x_sample.shape
    assert bp == 1 and s_len == 1
    caches = (cache_win128_kv[layer], cache_win512_kv[layer], cache_win2048_kv[layer])

    w_att = w_rest = w_in[layer].astype(BF16)
    pre_g = norm_pre[layer].reshape(1, D_MODEL)
    post_g = norm_post[layer].reshape(1, D_MODEL)
    hg_g = hgrn_out_norm[layer].reshape(1, HG_W)
    w_pa, w_pb, w_pc, w_o = (w[layer].astype(BF16) for w in (w_branch_a, w_branch_b, w_branch_c, w_out))
    half = A_HEAD_DIM // 2
    inv = ROPE_THETA ** (-jnp.arange(0, A_HEAD_DIM, 2, dtype=F32) / A_HEAD_DIM)
    invf = jnp.tile(inv, LANES // half).reshape(1, LANES)

    xs = x_sample.reshape(bd, D_MODEL)
    q_s, k_s, v_s, tails_s, _ = _inproj_attn(xs, pre_g, w_att, invf, tm=bd, pos_base=PAST_LEN, pos_step=0,
                                             tail_rows=(bd,) * A_GROUPS, out_dtype=F32, row_dil=(1,) * A_GROUPS)
    (az_s, bq_s, lf_s, kin_s, bi_s, bz_s, cq_s, cz_s, ga_s, gb_s, gc_s), _ = _inproj_rest(
        xs, pre_g, w_rest, hgrn_lb_logits, tm=bd, layer=layer, hg_dtype=F32, cq_dtype=F32)

    xp = x_prompt.reshape(t, D_MODEL)
    tile = _tiling(t, bd)
    tm_p = tile.proj_rows
    steps_p = t // tm_p
    tail_rows = tuple(min(w, t) for w in A_WINDOWS)
    q_p, k_p, v_p, tails_p, _ = _inproj_attn(
        xp, pre_g, w_att, invf, tm=tm_p, pos_base=0, pos_step=1, tail_rows=tail_rows, out_dtype=BF16,
        row_dil=A_DILATIONS)
    (az, bq, lf, kin, bi, bz, cq, cz, ga, gb, gc), (ob_s, s_s, oc_s) = _inproj_rest(
        xp, pre_g, w_rest, hgrn_lb_logits, tm=tm_p, layer=layer, hg_dtype=BF16, cq_dtype=BF16,
        rider=_join_riders(_hgrn_sample_rider(bq_s, lf_s, kin_s, bi_s, state_hgrn[layer], hg_g, steps=steps_p),
                           _xattn_sample_rider(cq_s, cache_mem_kv[layer], steps=steps_p)))
    att = [_attn_prompt(q_p[g], k_p[g], v_p[g], A_DILATIONS[g], rows_per_step=tile.attn_rows)
           for g in range(A_GROUPS)]
    mem_kv = _mem_kv(mem_prompt.reshape(MEM_LEN, D_MODEL), mem_norm[layer].reshape(1, D_MODEL),
                     w_mem_kv[layer].astype(BF16))
    mk = mem_kv[:, :XA_W].astype(BF16)
    mv = mem_kv[:, XA_W:].astype(BF16)
    ob_p, s_p, oa_s = _hgrn_and_attn_sample(bq, lf, kin, bi, hg_g, q_s, k_s, v_s, caches,
                                            seqs_per_step=tile.fused_seqs)
    y_p = _final([a[0] for a in att] + [a[1] for a in att], [cq, mk, mv], ob_p, az, bz, cz, ga, gb, gc,
                 xp, post_g, w_pa, w_pb, w_pc, w_o, tm=tile.final_rows, merge=True, xattn=True)

    y_s = _final([oa_s], [oc_s], ob_s, az_s, bz_s, cz_s, ga_s, gb_s, gc_s,
                 xs, post_g, w_pa, w_pb, w_pc, w_o, tm=bd, merge=False, xattn=False)

    new_win_p = [tails_p[g].reshape(1, 1, tail_rows[g], 2, A_HEADS, A_HEAD_DIM) for g in range(A_GROUPS)]
    new_win_s = [tails_s[g].reshape(1, bd, 1, 2, A_HEADS, A_HEAD_DIM) for g in range(A_GROUPS)]
    return (y_p.reshape(bp, t, D_MODEL), y_s.reshape(bd, 1, D_MODEL),
            new_win_p[0], new_win_p[1], new_win_p[2],
            s_p.reshape(1, 1, HG_HEADS, HG_K, HG_V),
            mem_kv.reshape(1, 1, MEM_LEN, 2, XA_HEADS, XA_HEAD_DIM),
            new_win_s[0], new_win_s[1], new_win_s[2],
            s_s.reshape(1, bd, HG_HEADS, HG_K, HG_V))
```

```python
import functools
from typing import Callable, NamedTuple

import jax
import jax.numpy as jnp
from jax import lax
from jax.experimental import pallas as pl
from jax.experimental.pallas import tpu as pltpu

D_MODEL = 1024
PAST_LEN = 8192
A_WINDOWS = (128, 512, 2048)
A_DILATIONS = (1, 4, 16)
A_GROUPS = 3
A_HEADS = 8
A_HEAD_DIM = 64
A_NKEY = 128
ROPE_THETA = 10000.0
HG_HEADS = 4
HG_K = 128
HG_V = 128
HG_CHUNK = 64
HG_BLOCK = 8
ATTN_BLOCKS_IN_FLIGHT = 4
MEM_LEN = 256
XA_HEADS = 4
XA_HEAD_DIM = 128
EPS = 1e-6

A_W = A_HEADS * A_HEAD_DIM
HG_W = HG_HEADS * HG_K
XA_W = XA_HEADS * XA_HEAD_DIM
N_ATT = 3 * A_GROUPS * A_W
COL_AZ = N_ATT
COL_BQ = COL_AZ + A_W
COL_BF = COL_BQ + HG_W
COL_BI = COL_BF + HG_W
COL_BZ = COL_BI + HG_W
COL_CQ = COL_BZ + HG_W
COL_CZ = COL_CQ + XA_W
COL_GA = COL_CZ + XA_W
COL_GB = COL_GA + D_MODEL
COL_GC = COL_GB + D_MODEL
N_IN = COL_GC + D_MODEL

LANES = 128
SUBLANES = 8
VMEM_LIMIT_BYTES = 56 * 1024 * 1024


class _Tiling(NamedTuple):
    proj_rows: int
    final_rows: int
    attn_rows: int
    fused_seqs: int


def _tiling(t, bd):
    tile = _Tiling(proj_rows=512, final_rows=512, attn_rows=2048, fused_seqs=2)
    assert t % tile.attn_rows == 0 and t % tile.proj_rows == 0 and t % tile.final_rows == 0
    assert bd % tile.fused_seqs == 0 and bd % (t // tile.proj_rows) == 0
    return tile

F32 = jnp.float32
BF16 = jnp.bfloat16
NEG_INF = float("-inf")


def _params(semantics, vmem=None):
    return pltpu.CompilerParams(dimension_semantics=semantics, vmem_limit_bytes=vmem)


def _resident(shape):
    return pl.BlockSpec(shape, lambda *_: (0,) * len(shape), pipeline_mode=pl.Buffered(1))


def _resident_columns(rows, col0, width):
    return pl.BlockSpec((pl.Element(rows), pl.Element(width)), lambda *_: (0, col0), pipeline_mode=pl.Buffered(1))


class _Rider(NamedTuple):
    body: Callable
    inputs: tuple = ()
    in_specs: tuple = ()
    out_shapes: tuple = ()
    out_specs: tuple = ()


_NO_RIDER = _Rider(body=lambda step: None)


def _join_riders(a, b):
    na_in, nb_in, na_out = len(a.inputs), len(b.inputs), len(a.out_shapes)

    def body(step, *refs):
        ins, outs = refs[:na_in + nb_in], refs[na_in + nb_in:]
        a.body(step, *ins[:na_in], *outs[:na_out])
        b.body(step, *ins[na_in:], *outs[na_out:])

    return _Rider(body, a.inputs + b.inputs, a.in_specs + b.in_specs, a.out_shapes + b.out_shapes,
                  a.out_specs + b.out_specs)


def _rmsnorm_rows(x, gain):
    ms = jnp.mean(x * x, axis=-1, keepdims=True)
    return x * lax.rsqrt(ms + EPS) * gain


def _sigmoid(x):
    return 0.5 * jnp.tanh(0.5 * x) + 0.5


def _silu(x):
    return x * _sigmoid(x)


def _tile_row_order(tm, d):
    j = lax.broadcasted_iota(jnp.int32, (tm, LANES), 0)
    per = tm // d
    return (j % per) * d + j // per


def _inproj_attn_kernel(x_ref, g_ref, w_ref, invf_ref, *refs, tm, n_tiles, pos_base, pos_step, tails, row_dil,
                        rider_body, n_rider_in, n_rider_out):
    rider_in, refs = refs[:n_rider_in], refs[n_rider_in:]
    q_refs = refs[0:3]
    k_refs = refs[3:6]
    v_refs = refs[6:9]
    t_refs = refs[9:12]
    rider_out = refs[12:12 + n_rider_out]
    cos_tab, sin_tab, xa_ref, xb_ref, xp_ref = refs[12 + n_rider_out:]
    i = pl.program_id(0)
    n_slab = A_W // LANES
    n_xslab = D_MODEL // LANES
    dils = sorted(set(row_dil))

    @pl.when(i == 0)
    def _():
        for t_idx, d in enumerate(dils):
            ang0 = (_tile_row_order(tm, d) * pos_step).astype(F32) * invf_ref[...]
            cos_tab[t_idx] = jnp.cos(ang0)
            sin_tab[t_idx] = jnp.sin(ang0)

    rider_body(i, *rider_in, *rider_out)

    lane = lax.broadcasted_iota(jnp.int32, (tm, LANES), 1)
    first_half = (lane % A_HEAD_DIM) < (A_HEAD_DIM // 2)
    base =(pos_base + i * tm * pos_step).astype(F32) * invf_ref[...]
    cos_b = jnp.cos(base)
    sin_b = jnp.sin(base)

    def trig(d):
        t_idx = dils.index(d)
        c0, s0 = cos_tab[t_idx], sin_tab[t_idx]
        cos = c0 * cos_b - s0 * sin_b
        sin = s0 * cos_b + c0 * sin_b
        return cos, jnp.where(first_half, -sin, sin)

    def rope(acc, cos, sin_signed):
        outs = []
        for c in range(n_slab):
            xc = acc[:, c * LANES:(c + 1) * LANES]
            partner = jnp.where(first_half, pltpu.roll(xc, LANES - 32, 1), pltpu.roll(xc, 32, 1))
            outs.append(xc * cos + partner * sin_signed)
        return jnp.concatenate(outs, axis=-1)

    xn32 = _rmsnorm_rows(x_ref[...], g_ref[...])
    xn = xn32.astype(BF16)

    lhs = {1: xn}
    if max(row_dil) > 1:
        assert set(row_dil) <= {1, 4, 16}
        per4 = tm // 4
        for c in range(n_xslab):
            xa_ref[c] = xn32[:, c * LANES:(c + 1) * LANES]
        for c in range(n_xslab):
            for r in range(4):
                blk4 = xa_ref[c, pl.ds(r, per4, stride=4), :]
                xb_ref[c, r * per4:(r + 1) * per4, :] = blk4
                xp_ref[0, r * per4:(r + 1) * per4, c * LANES:(c + 1) * LANES] = blk4.astype(BF16)
        lhs[4] = xp_ref[0]
        if 16 in row_dil:
            per16 = tm // 16
            for c in range(n_xslab):
                for r1 in range(4):
                    for r2 in range(4):
                        res = 4 * r2 + r1
                        xp_ref[1, res * per16:(res + 1) * per16, c * LANES:(c + 1) * LANES] = (
                            xb_ref[c, pl.ds(r1 * per4 + r2, per16, stride=4), :].astype(BF16))
            lhs[16] = xp_ref[1]

    def store_rows(ref, d, val):
        per = tm // d
        for r in range(d):
            ref[:, r * A_W:(r + 1) * A_W] = val[r * per:(r + 1) * per, :].astype(ref.dtype)

    def tail_block(g, col0, natural_val):
        first_tile, rows = tails[g]
        cond = (i >= first_tile) if rows >= tm else (i == n_tiles - 1)

        @pl.when(cond)
        def _():
            val = natural_val()
            t_refs[g][:, col0:col0 + A_W] = val if rows >= tm else val[tm - rows:, :]

    cos_n, sin_n = trig(1)
    for g in range(A_GROUPS):
        d = row_dil[g]
        cos, sin_s = (cos_n, sin_n) if d == 1 else trig(d)
        wq = w_ref[:, g * A_W:(g + 1) * A_W]
        wk = w_ref[:, (A_GROUPS + g) * A_W:(A_GROUPS + g + 1) * A_W]
        wv = w_ref[:, (2 * A_GROUPS + g) * A_W:(2 * A_GROUPS + g + 1) * A_W]
        q = rope(jnp.dot(lhs[d], wq, preferred_element_type=F32), cos, sin_s) * (A_HEAD_DIM ** -0.5)
        store_rows(q_refs[g], d, q)
        kr = rope(jnp.dot(lhs[d], wk, preferred_element_type=F32), cos, sin_s)
        store_rows(k_refs[g], d, kr)
        v = jnp.dot(lhs[d], wv, preferred_element_type=F32)
        store_rows(v_refs[g], d, v)
        if d == 1:
            tail_block(g, 0, lambda kr=kr: kr)
            tail_block(g, A_W, lambda v=v: v)
        else:
            tail_block(g, 0, lambda wk=wk: rope(jnp.dot(xn, wk, preferred_element_type=F32), cos_n, sin_n))
            tail_block(g, A_W, lambda wv=wv: jnp.dot(xn, wv, preferred_element_type=F32))


def _inproj_attn(x, gain, w_att, invf, *, tm, pos_base, pos_step, tail_rows, out_dtype, row_dil, rider=_NO_RIDER):
    n = x.shape[0]
    n_tiles = n // tm
    assert 1 in row_dil
    n_orders = len(set(row_dil))
    tails = []
    tail_specs = []
    tail_shapes = []
    for g in range(A_GROUPS):
        rows = tail_rows[g]
        first_tile = (n - rows) // tm
        tails.append((first_tile, rows))
        blk = min(tm, rows)
        tail_specs.append(pl.BlockSpec((blk, 2 * A_W), functools.partial(
            lambda i, ft: (jnp.maximum(i - ft, 0), 0), ft=first_tile)))
        tail_shapes.append(jax.ShapeDtypeStruct((rows, 2 * A_W), F32))
    row_specs = [pl.BlockSpec((tm // d, d * A_W), lambda i: (i, 0)) for d in row_dil] * 3
    row_shapes = [jax.ShapeDtypeStruct((n // d, d * A_W), out_dtype) for d in row_dil] * 3
    kernel = functools.partial(_inproj_attn_kernel, tm=tm, n_tiles=n_tiles, pos_base=pos_base,
                               pos_step=pos_step, tails=tuple(tails), row_dil=tuple(row_dil),
                               rider_body=rider.body, n_rider_in=len(rider.inputs), n_rider_out=len(rider.out_shapes))
    outs = pl.pallas_call(
        kernel,
        out_shape=row_shapes + tail_shapes + list(rider.out_shapes),
        grid=(n_tiles,),
        in_specs=[pl.BlockSpec((tm, D_MODEL), lambda i: (i, 0)),
                  _resident((1, D_MODEL)),
                  _resident_columns(D_MODEL, 0, N_ATT),
                  _resident((1, LANES))] + list(rider.in_specs),
        out_specs=row_specs + tail_specs + list(rider.out_specs),
        scratch_shapes=[pltpu.VMEM((n_orders, tm, LANES), F32), pltpu.VMEM((n_orders, tm, LANES), F32),
                        pltpu.VMEM((D_MODEL // LANES, tm, LANES), F32), pltpu.VMEM((D_MODEL // LANES, tm, LANES), F32),
                        pltpu.VMEM((2, tm, D_MODEL), BF16)],
        compiler_params=_params(("arbitrary",), VMEM_LIMIT_BYTES),
        name="inproj_attn",
    )(x, gain, w_att, invf, *rider.inputs)
    return outs[0:3], outs[3:6], outs[6:9], outs[9:12], outs[12:]


def _inproj_rest_kernel(x_ref, g_ref, w_ref, lbl_ref, *refs, layer, rider_body, n_rider_in):
    rider_in, refs = refs[:n_rider_in], refs[n_rider_in:]
    (az_ref, bq_ref, lf_ref, kin_ref, bi_ref, bz_ref, cq_ref, cz_ref, ga_ref, gb_ref, gc_ref) = refs[:11]
    rider_body(pl.program_id(0), *rider_in, *refs[11:])
    xn = _rmsnorm_rows(x_ref[...], g_ref[...]).astype(BF16)

    def proj(col, width):
        c0 = col - N_ATT
        return jnp.dot(xn, w_ref[:, c0:c0 + width], preferred_element_type=F32)

    logits = lbl_ref[...]
    e = jnp.exp(logits - jnp.max(logits, axis=0, keepdims=True))
    lb = jnp.sum(e[0:layer + 1, :], axis=0, keepdims=True) / jnp.sum(e, axis=0, keepdims=True)

    az_ref[...] = _silu(proj(COL_AZ, A_W)).astype(az_ref.dtype)
    bq_ref[...] = proj(COL_BQ, HG_W).astype(bq_ref.dtype)
    bf = proj(COL_BF, HG_W)
    gate = _sigmoid(bf)
    lf_ref[...] = jnp.log(lb + (1.0 - lb) * gate)
    kin_ref[...] = ((1.0 - lb) * (1.0 - gate)).astype(kin_ref.dtype)
    bi_ref[...] = proj(COL_BI, HG_W).astype(bi_ref.dtype)
    bz_ref[...] = _silu(proj(COL_BZ, HG_W)).astype(bz_ref.dtype)
    cq_ref[...] = proj(COL_CQ, XA_W).astype(cq_ref.dtype)
    cz_ref[...] = _silu(proj(COL_CZ, XA_W)).astype(cz_ref.dtype)
    ga_ref[...] = _sigmoid(proj(COL_GA, D_MODEL)).astype(ga_ref.dtype)
    gb_ref[...] = _sigmoid(proj(COL_GB, D_MODEL)).astype(gb_ref.dtype)
    gc_ref[...] = _sigmoid(proj(COL_GC, D_MODEL)).astype(gc_ref.dtype)


def _inproj_rest(x, gain, w_rest, lb_logits, *, tm, layer, hg_dtype, cq_dtype, rider=_NO_RIDER):
    n = x.shape[0]
    n_rest = N_IN - N_ATT
    s512 = pl.BlockSpec((tm, A_W), lambda i: (i, 0))
    s1024 = pl.BlockSpec((tm, D_MODEL), lambda i: (i, 0))

    def sds(width, dt):
        return jax.ShapeDtypeStruct((n, width), dt)

    outs = pl.pallas_call(
        functools.partial(_inproj_rest_kernel, layer=layer, rider_body=rider.body, n_rider_in=len(rider.inputs)),
        out_shape=[sds(A_W, BF16), sds(HG_W, hg_dtype), sds(HG_W, F32), sds(HG_W, hg_dtype), sds(HG_W, hg_dtype),
                   sds(HG_W, BF16), sds(XA_W, cq_dtype), sds(XA_W, BF16),
                   sds(D_MODEL, BF16), sds(D_MODEL, BF16), sds(D_MODEL, BF16)] + list(rider.out_shapes),
        grid=(n // tm,),
        in_specs=[pl.BlockSpec((tm, D_MODEL), lambda i: (i, 0)),
                  _resident((1, D_MODEL)),
                  _resident_columns(D_MODEL, N_ATT, n_rest),
                  _resident(lb_logits.shape)] + list(rider.in_specs),
        out_specs=[s512] * 8 + [s1024] * 3 + list(rider.out_specs),
        compiler_params=_params(("arbitrary",), VMEM_LIMIT_BYTES),
        name="inproj_rest",
    )(x, gain, w_rest, lb_logits, *rider.inputs)
    return outs[:11], outs[11:]


def _attn_prompt_kernel(q_ref, kc_ref, vc_ref, kp_ref, vp_ref, o_ref, st_ref, kbuf, vbuf, o_acc, st_acc,
                        *, tmv, dil, rps):
    n = pl.program_id(0)
    rstep = pl.program_id(1)
    blk = A_NKEY

    qi = lax.broadcasted_iota(jnp.int32, (blk, 2 * blk), 0)
    ki = lax.broadcasted_iota(jnp.int32, (blk, 2 * blk), 1)
    dist = qi + blk - ki
    band = (dist >= 0) & (dist <= A_NKEY)
    bias = jnp.where(band, 0.0, NEG_INF).astype(F32)
    bias_first = jnp.where(band & (ki >= blk), 0.0, NEG_INF).astype(F32)
    lane = lax.broadcasted_iota(jnp.int32, (blk, LANES), 1)
    low_head = lane < A_HEAD_DIM
    head_mask = (jnp.where(low_head, 1.0, 0.0).astype(BF16), jnp.where(low_head, 0.0, 1.0).astype(BF16))

    for rr in range(rps):
        cs = slice(rr * A_W, (rr + 1) * A_W)
        kbuf[rr, 0:blk, :] = kp_ref[:, cs]
        kbuf[rr, blk:, :] = kc_ref[:, cs]
        vbuf[rr, 0:blk, :] = vp_ref[:, cs]
        vbuf[rr, blk:, :] = vc_ref[:, cs]
        res = rstep * rps + rr

        def body(b, carry, cs=cs, res=res, rr=rr):
            r0 = pl.multiple_of(b * blk, blk)
            is_first = jnp.logical_and(n == 0, b == 0)
            bias_b = jnp.where(is_first, bias_first, bias)
            qb = q_ref[pl.ds(r0, blk), cs]
            kb = kbuf[rr, pl.ds(r0, 2 * blk), :]
            vb = vbuf[rr, pl.ds(r0, 2 * blk), :]
            out_rows = pl.ds(r0 * dil + res, blk, stride=dil)
            stats = jnp.zeros((blk, LANES), F32)
            for pair in range(A_HEADS // 2):
                sl = slice(pair * LANES, (pair + 1) * LANES)
                qp = qb[:, sl]
                kp = kb[:, sl]
                vp = vb[:, sl]
                outs = []
                for hh in range(2):
                    qm = qp * head_mask[hh]
                    s = pl.dot(qm, kp, trans_b=True) + bias_b
                    m = jnp.max(s, axis=-1, keepdims=True)
                    p = jnp.exp(s - m)
                    l = jnp.sum(p, axis=-1, keepdims=True)
                    pv = jnp.dot(p.astype(BF16), vp, preferred_element_type=F32)
                    outs.append(pv * (1.0 / l))
                    lse = m + jnp.log(l)
                    stats = jnp.where(lane == 2 * pair + hh, lse, stats)
                o_acc[pair, out_rows, :] = jnp.where(low_head, outs[0], outs[1])
            st_acc[out_rows, :] = stats
            return carry

        lax.fori_loop(0, tmv // blk, body, 0, unroll=min(ATTN_BLOCKS_IN_FLIGHT, tmv // blk))

    @pl.when(rstep == pl.num_programs(1) - 1)
    def _():
        for pair in range(A_HEADS // 2):
            o_ref[:, pair * LANES:(pair + 1) * LANES] = o_acc[pair].astype(o_ref.dtype)
        st_ref[...] = st_acc[...]


def _attn_prompt(qv, kv, vv, dil, *, rows_per_step):
    l = qv.shape[0]
    t = l * dil
    blk = A_NKEY
    tmv = max(rows_per_step // dil, blk)
    rps = min(dil, max(1, ATTN_BLOCKS_IN_FLIGHT * blk // tmv))
    assert t % (tmv * dil) == 0 and dil % rps == 0
    cur = pl.BlockSpec((tmv, rps * A_W), lambda n, r: (n, r))
    prev = pl.BlockSpec((blk, rps * A_W), lambda n, r: (jnp.maximum(n * (tmv // blk) - 1, 0), r))
    rows = tmv * dil
    return pl.pallas_call(
        functools.partial(_attn_prompt_kernel, tmv=tmv, dil=dil, rps=rps),
        out_shape=[jax.ShapeDtypeStruct((t, A_W), BF16), jax.ShapeDtypeStruct((t, LANES), F32)],
        grid=(t // rows, dil // rps),
        in_specs=[cur, cur, cur, prev, prev],
        out_specs=[pl.BlockSpec((rows, A_W), lambda n, r: (n, 0)), pl.BlockSpec((rows, LANES), lambda n, r: (n, 0))],
        scratch_shapes=[pltpu.VMEM((rps, tmv + blk, A_W), BF16), pltpu.VMEM((rps, tmv + blk, A_W), BF16),
                        pltpu.VMEM((A_HEADS // 2, rows, LANES), F32), pltpu.VMEM((rows, LANES), F32)],
        compiler_params=_params(("parallel", "arbitrary"), VMEM_LIMIT_BYTES),
        name=f"attn_prompt_d{dil}",
    )(qv, kv, vv, kv, vv)


def _hgrn_prompt_kernel(q_ref, lf_ref, k_ref, v_ref, gain_ref, o_ref, s_out_ref, st_ref, *, th,
                        co_scheduled=lambda: None):
    i = pl.program_id(0)
    c_rows = HG_CHUNK
    nblk = c_rows // HG_BLOCK
    shape3 = (nblk, HG_BLOCK, HG_K)

    @pl.when(i == 0)
    def _():
        st_ref[...] = jnp.zeros_like(st_ref)

    co_scheduled()

    sub = lax.broadcasted_iota(jnp.int32, shape3, 1)
    a_row = lax.broadcasted_iota(jnp.int32, (c_rows, c_rows), 0) // HG_BLOCK
    a_col = lax.broadcasted_iota(jnp.int32, (c_rows, c_rows), 1) // HG_BLOCK

    def shift(x3, s):
        return pltpu.roll(x3, s, 1)

    def chunk(c, carry):
        r0 = pl.multiple_of(c * c_rows, c_rows)
        for h in range(HG_HEADS):
            sl = slice(h * HG_K, (h + 1) * HG_K)
            q = q_ref[pl.ds(r0, c_rows), sl].astype(F32)
            k = k_ref[pl.ds(r0, c_rows), sl].astype(F32)
            v = v_ref[pl.ds(r0, c_rows), sl].astype(F32)
            q3, k3, v3 = (a.reshape(shape3) for a in (q, k, v))
            cum = lf_ref[pl.ds(r0, c_rows), sl].reshape(shape3)
            s = 1
            while s < HG_BLOCK:
                cum = cum + jnp.where(sub >= s, shift(cum, s), 0.0)
                s *= 2
            anchors = [jnp.zeros((1, 1, HG_K), F32)]
            for j in range(nblk):
                anchors.append(anchors[j] + cum[j:j + 1, HG_BLOCK - 1:HG_BLOCK, :])
            b3 = cum + jnp.concatenate(anchors[:nblk], axis=0)
            b = b3.reshape(c_rows, HG_K)
            b_last = anchors[nblk].reshape(1, HG_K)

            o3 = jnp.sum(q3 * k3, axis=-1, keepdims=True) * v3
            for d in range(1, HG_BLOCK):
                dec = jnp.exp(jnp.where(sub >= d, cum - shift(cum, d), NEG_INF))
                a = jnp.sum(q3 * shift(k3, d) * dec, axis=-1, keepdims=True)
                o3 = o3 + a * shift(v3, d)
            o = o3.reshape(c_rows, HG_K)

            b_end = jnp.concatenate(anchors[1:], axis=0)
            k_hat = (k3 * jnp.exp(b_end - b3)).reshape(c_rows, HG_K).astype(BF16)
            q_ver = [(q3 * jnp.exp(jnp.minimum(b3 - anchors[j + 1], 0.0))).reshape(c_rows, HG_K).astype(BF16)
                     for j in range(nblk - 1)]
            r = pl.dot(jnp.concatenate(q_ver, axis=0), k_hat, trans_b=True)
            att = jnp.zeros((c_rows, c_rows), F32)
            for j in range(nblk - 1):
                att = jnp.where(a_col == j, r[j * c_rows:(j + 1) * c_rows, :], att)
            att = jnp.where(a_row > a_col, att, 0.0)
            vb = v.astype(BF16)
            o = o + jnp.dot(att.astype(BF16), vb, preferred_element_type=F32)

            st = st_ref[h]
            o = o + pl.dot((q * jnp.exp(b)).astype(BF16), st.astype(BF16), trans_b=True)
            k_end = (k * jnp.exp(b_last - b)).astype(BF16)
            st_ref[h] = st * jnp.exp(b_last) + pl.dot(vb, k_end, trans_a=True)

            on = o * lax.rsqrt(jnp.mean(o * o, axis=-1, keepdims=True) + EPS) * gain_ref[:, sl]
            o_ref[pl.ds(r0, c_rows), sl] = on.astype(o_ref.dtype)
        return carry

    n_chunks = th // c_rows
    lax.fori_loop(0, n_chunks, chunk, 0, unroll=n_chunks <= 4)

    @pl.when(i == pl.num_programs(0) - 1)
    def _():
        for h in range(HG_HEADS):
            s_out_ref[h] = st_ref[h].T


def _mem_kv_kernel(m_ref, g_ref, w_ref, o_ref):
    xn = _rmsnorm_rows(m_ref[...], g_ref[...]).astype(BF16)
    o_ref[...] = jnp.dot(xn, w_ref[...], preferred_element_type=F32)


def _mem_kv(mem, gain, w):
    m = mem.shape[0]
    return pl.pallas_call(
        _mem_kv_kernel,
        out_shape=jax.ShapeDtypeStruct((m, 2 * XA_W), F32),
        name="mem_kv",
    )(mem, gain, w)


def _softmax_rows(s):
    m = jnp.max(s, axis=-1, keepdims=True)
    p = jnp.exp(s - m)
    return p, jnp.sum(p, axis=-1, keepdims=True)


def _xattn_rows(cq, mk_ref, mv_ref):
    parts = []
    for h in range(XA_HEADS):
        sl = slice(h * XA_HEAD_DIM, (h + 1) * XA_HEAD_DIM)
        s = pl.dot(cq[:, sl], mk_ref[:, sl], trans_b=True) * (XA_HEAD_DIM ** -0.5)
        p, l = _softmax_rows(s)
        parts.append(jnp.dot(p.astype(BF16), mv_ref[:, sl], preferred_element_type=F32) * (1.0 / l))
    return jnp.concatenate(parts, axis=-1)


def _final_kernel(*refs, tm, merge, xattn):
    refs = list(refs)
    if merge:
        o_refs = [refs.pop(0) for _ in range(A_GROUPS)]
        st_refs = [refs.pop(0) for _ in range(A_GROUPS)]
    else:
        oa_ref = refs.pop(0)
    if xattn:
        cq_ref, mk_ref, mv_ref = (refs.pop(0) for _ in range(3))
    else:
        oc_ref = refs.pop(0)
    (ob_ref, az_ref, bz_ref, cz_ref, ga_ref, gb_ref, gc_ref, x_ref, pg_ref,
     wa_ref, wb_ref, wc_ref, wo_ref, out_ref) = refs

    if merge:
        lse = [r[...] for r in st_refs]
        mx = jnp.maximum(jnp.maximum(lse[0], lse[1]), lse[2])
        ex = [jnp.exp(x - mx) for x in lse]
        inv = 1.0 / (ex[0] + ex[1] + ex[2])
        e_row = lax.broadcasted_iota(jnp.int32, (LANES, A_W), 0)
        e_col = lax.broadcasted_iota(jnp.int32, (LANES, A_W), 1) // A_HEAD_DIM
        spread = jnp.where(e_row == e_col, 1.0, 0.0).astype(BF16)
        oa = jnp.zeros((tm, A_W), F32)
        for g in range(A_GROUPS):
            alpha = (ex[g] * inv).astype(BF16)
            w = jnp.dot(alpha, spread, preferred_element_type=F32)
            oa = oa + w * o_refs[g][...].astype(F32)
    else:
        oa = oa_ref[...].astype(F32)
    ya = (oa * az_ref[...].astype(F32)).astype(BF16)

    oc = _xattn_rows(cq_ref[...], mk_ref, mv_ref) if xattn else oc_ref[...].astype(F32)
    yc = (oc * cz_ref[...].astype(F32)).astype(BF16)

    yb = (ob_ref[...].astype(F32) * bz_ref[...].astype(F32)).astype(BF16)

    merged = (ga_ref[...].astype(F32) * jnp.dot(ya, wa_ref[...], preferred_element_type=F32)
              + gb_ref[...].astype(F32) * jnp.dot(yb, wb_ref[...], preferred_element_type=F32)
              + gc_ref[...].astype(F32) * jnp.dot(yc, wc_ref[...], preferred_element_type=F32))
    z = jnp.dot(merged.astype(BF16), wo_ref[...], preferred_element_type=F32)
    out_ref[...] = x_ref[...] + _rmsnorm_rows(z, pg_ref[...])


def _final(att_in, xa_in, ob, az, bz, cz, ga, gb, gc, x, post_g, w_pa, w_pb, w_pc, w_out, *, tm, merge, xattn):
    n = x.shape[0]

    def rows(width):
        return pl.BlockSpec((tm, width), lambda i: (i, 0))

    if merge:
        att_specs = [rows(A_W)] * A_GROUPS + [rows(LANES)] * A_GROUPS
    else:
        att_specs = [rows(A_W)]
    if xattn:
        xa_specs = [rows(XA_W), _resident((MEM_LEN, XA_W)), _resident((MEM_LEN, XA_W))]
    else:
        xa_specs = [rows(XA_W)]
    in_specs = (att_specs + xa_specs + [rows(HG_W), rows(A_W), rows(HG_W), rows(XA_W)]
                + [rows(D_MODEL)] * 4 + [_resident((1, D_MODEL))]
                + [_resident((A_W, D_MODEL)), _resident((HG_W, D_MODEL)), _resident((XA_W, D_MODEL)),
                   _resident((D_MODEL, D_MODEL))])
    return pl.pallas_call(
        functools.partial(_final_kernel, tm=tm, merge=merge, xattn=xattn),
        out_shape=jax.ShapeDtypeStruct((n, D_MODEL), F32),
        grid=(n // tm,),
        in_specs=in_specs,
        out_specs=rows(D_MODEL),
        compiler_params=_params(("parallel",), VMEM_LIMIT_BYTES),
        name="final_merge" if merge else "final",
    )(*att_in, *xa_in, ob, az, bz, cz, ga, gb, gc, x, post_g, w_pa, w_pb, w_pc, w_out)


def _head_rows(q_row, head_dim):
    w = q_row.shape[-1]
    assert w // head_dim <= SUBLANES
    sub = lax.broadcasted_iota(jnp.int32, (SUBLANES, w), 0)
    lane_head = lax.broadcasted_iota(jnp.int32, (SUBLANES, w), 1) // head_dim
    keep = sub == lane_head
    return jnp.where(keep, jnp.broadcast_to(q_row, (SUBLANES, w)), 0.0), keep


def _window_cache_view(cache, g):
    bd, wb = cache.shape[0], cache.shape[1]
    assert wb == A_NKEY * A_DILATIONS[g], "window buffer must hold exactly 128 dilated keys"
    return jnp.transpose(cache, (0, 2, 3, 4, 1)).reshape(bd, 2, A_W, wb)


def _attn_sample_group(dil, q_row, k_new, v_new, c_ref):
    wb = c_ref.shape[-1]
    qm, _ = _head_rows(q_row, A_HEAD_DIM)
    kt = c_ref[0].astype(BF16)
    vt = c_ref[1].astype(BF16)
    s = jnp.dot(qm.astype(BF16), kt, preferred_element_type=F32)
    w_pos = lax.broadcasted_iota(jnp.int32, (SUBLANES, wb), 1)
    s = jnp.where(w_pos % dil == 0, s, NEG_INF)
    s_new = jnp.sum(qm * k_new, axis=-1, keepdims=True)
    m = jnp.maximum(jnp.max(s, axis=-1, keepdims=True), s_new)
    p = jnp.exp(s - m)
    p_new = jnp.exp(s_new - m)
    l = jnp.sum(p, axis=-1, keepdims=True) + p_new
    o = (pl.dot(p.astype(BF16), vt, trans_b=True) + p_new * v_new) * (1.0 / l)
    return o, m + jnp.log(l)


def _attn_sample_kernel(q0, q1, q2, k0, k1, k2, v0, v1, v2, c0, c1, c2, o_ref, *, sb):
    q_refs, k_refs, v_refs, c_refs = (q0, q1, q2), (k0, k1, k2), (v0, v1, v2), (c0, c1, c2)
    _, keep = _head_rows(jnp.zeros((1, A_W), F32), A_HEAD_DIM)
    for b in range(sb):
        row = pl.ds(pl.program_id(0) * sb + b, 1)
        outs = []
        lses = []
        for g in range(A_GROUPS):
            o, lse = _attn_sample_group(A_DILATIONS[g], q_refs[g][row, :], k_refs[g][row, :], v_refs[g][row, :],
                                        c_refs[g].at[b])
            outs.append(o)
            lses.append(lse)
        mx = jnp.maximum(jnp.maximum(lses[0], lses[1]), lses[2])
        ex = [jnp.exp(x - mx) for x in lses]
        inv = 1.0 / (ex[0] + ex[1] + ex[2])
        oa = (ex[0] * outs[0] + ex[1] * outs[1] + ex[2] * outs[2]) * inv
        o_ref[row, :] = jnp.sum(jnp.where(keep, oa, 0.0), axis=0, keepdims=True)


def _hgrn_and_attn_sample_kernel(*refs, th, sb):
    hg_in, at_in = refs[0:5], refs[5:17]
    hg_out, at_out, hg_state = refs[17:19], refs[19], refs[20]
    _hgrn_prompt_kernel(*hg_in, *hg_out, hg_state, th=th,
                        co_scheduled=lambda: _attn_sample_kernel(*at_in, at_out, sb=sb))


def _hgrn_and_attn_sample(q, lf, k, v, gain, qs, ks, vs, caches, *, seqs_per_step):
    t = q.shape[0]
    bd = qs[0].shape[0]
    sb = seqs_per_step
    steps = bd // sb
    assert bd % sb == 0 and t % steps == 0 and (t // steps) % HG_CHUNK == 0
    th = t // steps
    views = [_window_cache_view(caches[g], g) for g in range(A_GROUPS)]
    c_specs = [pl.BlockSpec((sb,) + v_.shape[1:], lambda b: (b, 0, 0, 0)) for v_ in views]
    rows = pl.BlockSpec((th, HG_W), lambda i: (i, 0))
    full = pl.BlockSpec((bd, A_W), lambda b: (0, 0))
    state = pl.BlockSpec((HG_HEADS, HG_K, HG_V), lambda i: (0, 0, 0))
    return pl.pallas_call(
        functools.partial(_hgrn_and_attn_sample_kernel, th=th, sb=sb),
        out_shape=[jax.ShapeDtypeStruct((t, HG_W), BF16), jax.ShapeDtypeStruct((HG_HEADS, HG_K, HG_V), F32),
                   jax.ShapeDtypeStruct((bd, A_W), F32)],
        grid=(steps,),
        in_specs=[rows] * 4 + [_resident((1, HG_W))] + [full] * 9 + c_specs,
        out_specs=[rows, state, full],
        scratch_shapes=[pltpu.VMEM((HG_HEADS, HG_V, HG_K), F32)],
        compiler_params=_params(("arbitrary",), VMEM_LIMIT_BYTES),
        name="hgrn_prompt_attn_sample",
    )(q, lf, k, v, gain, *qs, *ks, *vs, *views)


def _xattn_sample_body(step, q_ref, c_ref, o_ref, *, sb):
    stride = 2 * XA_HEADS
    for b in range(sb):
        row = pl.ds(step * sb + b, 1)
        q_row = q_ref[row, :]
        outs = []
        for h in range(XA_HEADS):
            sl = slice(h * XA_HEAD_DIM, (h + 1) * XA_HEAD_DIM)
            kh = c_ref[b, pl.ds(h, MEM_LEN, stride=stride), :]
            vh = c_ref[b, pl.ds(XA_HEADS + h, MEM_LEN, stride=stride), :]
            s = jnp.sum(kh * q_row[:, sl], axis=-1, keepdims=True) * (XA_HEAD_DIM ** -0.5)
            m = jnp.max(s, axis=0, keepdims=True)
            p = jnp.exp(s - m)
            l = jnp.sum(p, axis=0, keepdims=True)
            outs.append(jnp.sum(p * vh, axis=0, keepdims=True) * (1.0 / l))
        o_ref[row, :] = jnp.concatenate(outs, axis=-1)


def _xattn_sample_rider(cq, mem_cache, *, steps):
    bd = cq.shape[0]
    assert bd % steps == 0
    sb = bd // steps
    rows = MEM_LEN * 2 * XA_HEADS
    view = mem_cache.reshape(bd, rows, XA_HEAD_DIM)
    full = pl.BlockSpec((bd, XA_W), lambda i: (0, 0))
    return _Rider(
        body=functools.partial(_xattn_sample_body, sb=sb),
        inputs=(cq, view),
        in_specs=(full, pl.BlockSpec((sb, rows, XA_HEAD_DIM), lambda i: (i, 0, 0))),
        out_shapes=(jax.ShapeDtypeStruct((bd, XA_W), F32),),
        out_specs=(full,))


def _hgrn_sample_body(step, qt_ref, lft_ref, kt_ref, v_ref, s0_ref, gain_ref, o_ref, s_ref, *, sb):
    for b in range(sb):
        row = pl.ds(step * sb + b, 1)
        v_all = v_ref[row, :]
        outs = []
        for h in range(HG_HEADS):
            sl = slice(h * HG_K, (h + 1) * HG_K)
            d_col = jnp.exp(lft_ref[0, sl, b:b + 1])
            k_col = kt_ref[0, sl, b:b + 1]
            q_col = qt_ref[0, sl, b:b + 1]
            s_new = d_col * s0_ref[b, h] + k_col * v_all[:, sl]
            s_ref[b, h] = s_new
            o = jnp.sum(q_col * s_new, axis=0, keepdims=True)
            outs.append(o * lax.rsqrt(jnp.mean(o * o, axis=-1, keepdims=True) + EPS) * gain_ref[:, sl])
        o_ref[row, :] = jnp.concatenate(outs, axis=-1)


def _hgrn_sample_rider(q, lf, kin, v, s0, gain, *, steps):
    bd = q.shape[0]
    assert bd % steps == 0
    sb = bd // steps

    def cols(a):
        return a.reshape(steps, sb, HG_W).transpose(0, 2, 1)

    col_spec = pl.BlockSpec((1, HG_W, sb), lambda i: (i, 0, 0))
    st_spec = pl.BlockSpec((sb, HG_HEADS, HG_K, HG_V), lambda i: (i, 0, 0, 0))
    full = pl.BlockSpec((bd, HG_W), lambda i: (0, 0))
    return _Rider(
        body=functools.partial(_hgrn_sample_body, sb=sb),
        inputs=(cols(q), cols(lf), cols(kin), v, s0, gain),
        in_specs=(col_spec, col_spec, col_spec, full, st_spec, _resident((1, HG_W))),
        out_shapes=(jax.ShapeDtypeStruct((bd, HG_W), F32), jax.ShapeDtypeStruct(s0.shape, F32)),
        out_specs=(full, st_spec))


def kernel(x_prompt, x_sample, mem_prompt, cache_win128_kv, cache_win512_kv, cache_win2048_kv, state_hgrn, cache_mem_kv, norm_pre, norm_post, w_in, hgrn_lb_logits, hgrn_out_norm, mem_norm, w_mem_kv, w_branch_a, w_branch_b, w_branch_c, w_out):
    depth = w_in.shape[0]
    assert depth == 1, "single-layer trunk"
    layer = 0
    bp, t, _ = x_prompt.shape
    bd, s_len, _ = x_sample.shape
    assert bp == 1 and s_len == 1
    caches = (cache_win128_kv[layer], cache_win512_kv[layer], cache_win2048_kv[layer])

    w_att = w_rest = w_in[layer].astype(BF16)
    pre_g = norm_pre[layer].reshape(1, D_MODEL)
    post_g = norm_post[layer].reshape(1, D_MODEL)
    hg_g = hgrn_out_norm[layer].reshape(1, HG_W)
    w_pa, w_pb, w_pc, w_o = (w[layer].astype(BF16) for w in (w_branch_a, w_branch_b, w_branch_c, w_out))
    half = A_HEAD_DIM // 2
    inv = ROPE_THETA ** (-jnp.arange(0, A_HEAD_DIM, 2, dtype=F32) / A_HEAD_DIM)
    invf = jnp.tile(inv, LANES // half).reshape(1, LANES)

    xs = x_sample.reshape(bd, D_MODEL)
    q_s, k_s, v_s, tails_s, _ = _inproj_attn(xs, pre_g, w_att, invf, tm=bd, pos_base=PAST_LEN, pos_step=0,
                                             tail_rows=(bd,) * A_GROUPS, out_dtype=F32, row_dil=(1,) * A_GROUPS)
    (az_s, bq_s, lf_s, kin_s, bi_s, bz_s, cq_s, cz_s, ga_s, gb_s, gc_s), _ = _inproj_rest(
        xs, pre_g, w_rest, hgrn_lb_logits, tm=bd, layer=layer, hg_dtype=F32, cq_dtype=F32)

    xp = x_prompt.reshape(t, D_MODEL)
    tile = _tiling(t, bd)
    tm_p = tile.proj_rows
    steps_p = t // tm_p
    tail_rows = tuple(min(w, t) for w in A_WINDOWS)
    q_p, k_p, v_p, tails_p, _ = _inproj_attn(
        xp, pre_g, w_att, invf, tm=tm_p, pos_base=0, pos_step=1, tail_rows=tail_rows, out_dtype=BF16,
        row_dil=A_DILATIONS)
    (az, bq, lf, kin, bi, bz, cq, cz, ga, gb, gc), (ob_s, s_s, oc_s) = _inproj_rest(
        xp, pre_g, w_rest, hgrn_lb_logits, tm=tm_p, layer=layer, hg_dtype=BF16, cq_dtype=BF16,
        rider=_join_riders(_hgrn_sample_rider(bq_s, lf_s, kin_s, bi_s, state_hgrn[layer], hg_g, steps=steps_p),
                           _xattn_sample_rider(cq_s, cache_mem_kv[layer], steps=steps_p)))
    att = [_attn_prompt(q_p[g], k_p[g], v_p[g], A_DILATIONS[g], rows_per_step=tile.attn_rows)
           for g in range(A_GROUPS)]
    mem_kv = _mem_kv(mem_prompt.reshape(MEM_LEN, D_MODEL), mem_norm[layer].reshape(1, D_MODEL),
                     w_mem_kv[layer].astype(BF16))
    mk = mem_kv[:, :XA_W].astype(BF16)
    mv = mem_kv[:, XA_W:].astype(BF16)
    ob_p, s_p, oa_s = _hgrn_and_attn_sample(bq, lf, kin, bi, hg_g, q_s, k_s, v_s, caches,
                                            seqs_per_step=tile.fused_seqs)
    y_p = _final([a[0] for a in att] + [a[1] for a in att], [cq, mk, mv], ob_p, az, bz, cz, ga, gb, gc,
                 xp, post_g, w_pa, w_pb, w_pc, w_o, tm=tile.final_rows, merge=True, xattn=True)

    y_s = _final([oa_s], [oc_s], ob_s, az_s, bz_s, cz_s, ga_s, gb_s, gc_s,
                 xs, post_g, w_pa, w_pb, w_pc, w_o, tm=bd, merge=False, xattn=False)

    new_win_p = [tails_p[g].reshape(1, 1, tail_rows[g], 2, A_HEADS, A_HEAD_DIM) for g in range(A_GROUPS)]
    new_win_s = [tails_s[g].reshape(1, bd, 1, 2, A_HEADS, A_HEAD_DIM) for g in range(A_GROUPS)]
    return (y_p.reshape(bp, t, D_MODEL), y_s.reshape(bd, 1, D_MODEL),
            new_win_p[0], new_win_p[1], new_win_p[2],
            s_p.reshape(1, 1, HG_HEADS, HG_K, HG_V),
            mem_kv.reshape(1, 1, MEM_LEN, 2, XA_HEADS, XA_HEAD_DIM),
            new_win_s[0], new_win_s[1], new_win_s[2],
            s_s.reshape(1, bd, HG_HEADS, HG_K, HG_V))
```

```python
import functools
from typing import Callable, NamedTuple

import jax
import jax.numpy as jnp
from jax import lax
from jax.experimental import pallas as pl
from jax.experimental.pallas import tpu as pltpu

D_MODEL = 1024
PAST_LEN = 8192
A_WINDOWS = (128, 512, 2048)
A_DILATIONS = (1, 4, 16)
A_GROUPS = 3
A_HEADS = 8
A_HEAD_DIM = 64
A_NKEY = 128
ROPE_THETA = 10000.0
HG_HEADS = 4
HG_K = 128
HG_V = 128
HG_CHUNK = 64
HG_BLOCK = 8
ATTN_BLOCKS_IN_FLIGHT = 8
MEM_LEN = 256
XA_HEADS = 4
XA_HEAD_DIM = 128
EPS = 1e-6

A_W = A_HEADS * A_HEAD_DIM
HG_W = HG_HEADS * HG_K
XA_W = XA_HEADS * XA_HEAD_DIM
N_ATT = 3 * A_GROUPS * A_W
COL_AZ = N_ATT
COL_BQ = COL_AZ + A_W
COL_BF = COL_BQ + HG_W
COL_BI = COL_BF + HG_W
COL_BZ = COL_BI + HG_W
COL_CQ = COL_BZ + HG_W
COL_CZ = COL_CQ + XA_W
COL_GA = COL_CZ + XA_W
COL_GB = COL_GA + D_MODEL
COL_GC = COL_GB + D_MODEL
N_IN = COL_GC + D_MODEL

LANES = 128
SUBLANES = 8
VMEM_LIMIT_BYTES = 56 * 1024 * 1024


class _Tiling(NamedTuple):
    proj_rows: int
    final_rows: int
    attn_rows: int
    fused_seqs: int


def _tiling(t, bd):
    tile = _Tiling(proj_rows=512, final_rows=512, attn_rows=2048, fused_seqs=2)
    assert t % tile.attn_rows == 0 and t % tile.proj_rows == 0 and t % tile.final_rows == 0
    assert bd % tile.fused_seqs == 0 and bd % (t // tile.proj_rows) == 0
    return tile

F32 = jnp.float32
BF16 = jnp.bfloat16
NEG_INF = float("-inf")


def _params(semantics, vmem=None):
    return pltpu.CompilerParams(dimension_semantics=semantics, vmem_limit_bytes=vmem)


def _resident(shape):
    return pl.BlockSpec(shape, lambda *_: (0,) * len(shape), pipeline_mode=pl.Buffered(1))


def _resident_columns(rows, col0, width):
    return pl.BlockSpec((pl.Element(rows), pl.Element(width)), lambda *_: (0, col0), pipeline_mode=pl.Buffered(1))


class _Rider(NamedTuple):
    body: Callable
    inputs: tuple = ()
    in_specs: tuple = ()
    out_shapes: tuple = ()
    out_specs: tuple = ()


_NO_RIDER = _Rider(body=lambda step: None)


def _join_riders(a, b):
    na_in, nb_in, na_out = len(a.inputs), len(b.inputs), len(a.out_shapes)

    def body(step, *refs):
        ins, outs = refs[:na_in + nb_in], refs[na_in + nb_in:]
        a.body(step, *ins[:na_in], *outs[:na_out])
        b.body(step, *ins[na_in:], *outs[na_out:])

    return _Rider(body, a.inputs + b.inputs, a.in_specs + b.in_specs, a.out_shapes + b.out_shapes,
                  a.out_specs + b.out_specs)


def _rmsnorm_rows(x, gain):
    ms = jnp.mean(x * x, axis=-1, keepdims=True)
    return x * lax.rsqrt(ms + EPS) * gain


def _sigmoid(x):
    return 0.5 * jnp.tanh(0.5 * x) + 0.5


def _silu(x):
    return x * _sigmoid(x)


def _tile_row_order(tm, d):
    j = lax.broadcasted_iota(jnp.int32, (tm, LANES), 0)
    per = tm // d
    return (j % per) * d + j // per


def _inproj_attn_kernel(x_ref, g_ref, w_ref, invf_ref, *refs, tm, n_tiles, pos_base, pos_step, tails, row_dil,
                        rider_body, n_rider_in, n_rider_out):
    rider_in, refs = refs[:n_rider_in], refs[n_rider_in:]
    q_refs = refs[0:3]
    k_refs = refs[3:6]
    v_refs = refs[6:9]
    t_refs = refs[9:12]
    rider_out = refs[12:12 + n_rider_out]
    cos_tab, sin_tab, xa_ref, xb_ref, xp_ref = refs[12 + n_rider_out:]
    i = pl.program_id(0)
    n_slab = A_W // LANES
    n_xslab = D_MODEL // LANES
    dils = sorted(set(row_dil))

    @pl.when(i == 0)
    def _():
        for t_idx, d in enumerate(dils):
            ang0 = (_tile_row_order(tm, d) * pos_step).astype(F32) * invf_ref[...]
            cos_tab[t_idx] = jnp.cos(ang0)
            sin_tab[t_idx] = jnp.sin(ang0)

    rider_body(i, *rider_in, *rider_out)

    lane = lax.broadcasted_iota(jnp.int32, (tm, LANES), 1)
    first_half = (lane % A_HEAD_DIM) < (A_HEAD_DIM // 2)
    base =(pos_base + i * tm * pos_step).astype(F32) * invf_ref[...]
    cos_b = jnp.cos(base)
    sin_b = jnp.sin(base)

    def trig(d):
        t_idx = dils.index(d)
        c0, s0 = cos_tab[t_idx], sin_tab[t_idx]
        cos = c0 * cos_b - s0 * sin_b
        sin = s0 * cos_b + c0 * sin_b
        return cos, jnp.where(first_half, -sin, sin)

    def rope(acc, cos, sin_signed):
        outs = []
        for c in range(n_slab):
            xc = acc[:, c * LANES:(c + 1) * LANES]
            partner = jnp.where(first_half, pltpu.roll(xc, LANES - 32, 1), pltpu.roll(xc, 32, 1))
            outs.append(xc * cos + partner * sin_signed)
        return jnp.concatenate(outs, axis=-1)

    xn32 = _rmsnorm_rows(x_ref[...], g_ref[...])
    xn = xn32.astype(BF16)

    lhs = {1: xn}
    if max(row_dil) > 1:
        assert set(row_dil) <= {1, 4, 16}
        per4 = tm // 4
        for c in range(n_xslab):
            xa_ref[c] = xn32[:, c * LANES:(c + 1) * LANES]
        for c in range(n_xslab):
            for r in range(4):
                blk4 = xa_ref[c, pl.ds(r, per4, stride=4), :]
                xb_ref[c, r * per4:(r + 1) * per4, :] = blk4
                xp_ref[0, r * per4:(r + 1) * per4, c * LANES:(c + 1) * LANES] = blk4.astype(BF16)
        lhs[4] = xp_ref[0]
        if 16 in row_dil:
            per16 = tm // 16
            for c in range(n_xslab):
                for r1 in range(4):
                    for r2 in range(4):
                        res = 4 * r2 + r1
                        xp_ref[1, res * per16:(res + 1) * per16, c * LANES:(c + 1) * LANES] = (
                            xb_ref[c, pl.ds(r1 * per4 + r2, per16, stride=4), :].astype(BF16))
            lhs[16] = xp_ref[1]

    def store_rows(ref, d, val):
        per = tm // d
        for r in range(d):
            ref[:, r * A_W:(r + 1) * A_W] = val[r * per:(r + 1) * per, :].astype(ref.dtype)

    def tail_block(g, col0, natural_val):
        first_tile, rows = tails[g]
        cond = (i >= first_tile) if rows >= tm else (i == n_tiles - 1)

        @pl.when(cond)
        def _():
            val = natural_val()
            t_refs[g][:, col0:col0 + A_W] = val if rows >= tm else val[tm - rows:, :]

    cos_n, sin_n = trig(1)
    for g in range(A_GROUPS):
        d = row_dil[g]
        cos, sin_s = (cos_n, sin_n) if d == 1 else trig(d)
        wq = w_ref[:, g * A_W:(g + 1) * A_W]
        wk = w_ref[:, (A_GROUPS + g) * A_W:(A_GROUPS + g + 1) * A_W]
        wv = w_ref[:, (2 * A_GROUPS + g) * A_W:(2 * A_GROUPS + g + 1) * A_W]
        q = rope(jnp.dot(lhs[d], wq, preferred_element_type=F32), cos, sin_s) * (A_HEAD_DIM ** -0.5)
        store_rows(q_refs[g], d, q)
        kr = rope(jnp.dot(lhs[d], wk, preferred_element_type=F32), cos, sin_s)
        store_rows(k_refs[g], d, kr)
        v = jnp.dot(lhs[d], wv, preferred_element_type=F32)
        store_rows(v_refs[g], d, v)
        if d == 1:
            tail_block(g, 0, lambda kr=kr: kr)
            tail_block(g, A_W, lambda v=v: v)
        else:
            tail_block(g, 0, lambda wk=wk: rope(jnp.dot(xn, wk, preferred_element_type=F32), cos_n, sin_n))
            tail_block(g, A_W, lambda wv=wv: jnp.dot(xn, wv, preferred_element_type=F32))


def _inproj_attn(x, gain, w_att, invf, *, tm, pos_base, pos_step, tail_rows, out_dtype, row_dil, rider=_NO_RIDER):
    n = x.shape[0]
    n_tiles = n // tm
    assert 1 in row_dil
    n_orders = len(set(row_dil))
    tails = []
    tail_specs = []
    tail_shapes = []
    for g in range(A_GROUPS):
        rows = tail_rows[g]
        first_tile = (n - rows) // tm
        tails.append((first_tile, rows))
        blk = min(tm, rows)
        tail_specs.append(pl.BlockSpec((blk, 2 * A_W), functools.partial(
            lambda i, ft: (jnp.maximum(i - ft, 0), 0), ft=first_tile)))
        tail_shapes.append(jax.ShapeDtypeStruct((rows, 2 * A_W), F32))
    row_specs = [pl.BlockSpec((tm // d, d * A_W), lambda i: (i, 0)) for d in row_dil] * 3
    row_shapes = [jax.ShapeDtypeStruct((n // d, d * A_W), out_dtype) for d in row_dil] * 3
    kernel = functools.partial(_inproj_attn_kernel, tm=tm, n_tiles=n_tiles, pos_base=pos_base,
                               pos_step=pos_step, tails=tuple(tails), row_dil=tuple(row_dil),
                               rider_body=rider.body, n_rider_in=len(rider.inputs), n_rider_out=len(rider.out_shapes))
    outs = pl.pallas_call(
        kernel,
        out_shape=row_shapes + tail_shapes + list(rider.out_shapes),
        grid=(n_tiles,),
        in_specs=[pl.BlockSpec((tm, D_MODEL), lambda i: (i, 0)),
                  _resident((1, D_MODEL)),
                  _resident_columns(D_MODEL, 0, N_ATT),
                  _resident((1, LANES))] + list(rider.in_specs),
        out_specs=row_specs + tail_specs + list(rider.out_specs),
        scratch_shapes=[pltpu.VMEM((n_orders, tm, LANES), F32), pltpu.VMEM((n_orders, tm, LANES), F32),
                        pltpu.VMEM((D_MODEL // LANES, tm, LANES), F32), pltpu.VMEM((D_MODEL // LANES, tm, LANES), F32),
                        pltpu.VMEM((2, tm, D_MODEL), BF16)],
        compiler_params=_params(("arbitrary",), VMEM_LIMIT_BYTES),
        name="inproj_attn",
    )(x, gain, w_att, invf, *rider.inputs)
    return outs[0:3], outs[3:6], outs[6:9], outs[9:12], outs[12:]


def _inproj_rest_kernel(x_ref, g_ref, w_ref, lbl_ref, *refs, layer, rider_body, n_rider_in):
    rider_in, refs = refs[:n_rider_in], refs[n_rider_in:]
    (az_ref, bq_ref, lf_ref, kin_ref, bi_ref, bz_ref, cq_ref, cz_ref, ga_ref, gb_ref, gc_ref) = refs[:11]
    rider_body(pl.program_id(0), *rider_in, *refs[11:])
    xn = _rmsnorm_rows(x_ref[...], g_ref[...]).astype(BF16)

    def proj(col, width):
        c0 = col - N_ATT
        return jnp.dot(xn, w_ref[:, c0:c0 + width], preferred_element_type=F32)

    logits = lbl_ref[...]
    e = jnp.exp(logits - jnp.max(logits, axis=0, keepdims=True))
    lb = jnp.sum(e[0:layer + 1, :], axis=0, keepdims=True) / jnp.sum(e, axis=0, keepdims=True)

    az_ref[...] = _silu(proj(COL_AZ, A_W)).astype(az_ref.dtype)
    bq_ref[...] = proj(COL_BQ, HG_W).astype(bq_ref.dtype)
    bf = proj(COL_BF, HG_W)
    gate = _sigmoid(bf)
    lf_ref[...] = jnp.log(lb + (1.0 - lb) * gate)
    kin_ref[...] = ((1.0 - lb) * (1.0 - gate)).astype(kin_ref.dtype)
    bi_ref[...] = proj(COL_BI, HG_W).astype(bi_ref.dtype)
    bz_ref[...] = _silu(proj(COL_BZ, HG_W)).astype(bz_ref.dtype)
    cq_ref[...] = proj(COL_CQ, XA_W).astype(cq_ref.dtype)
    cz_ref[...] = _silu(proj(COL_CZ, XA_W)).astype(cz_ref.dtype)
    ga_ref[...] = _sigmoid(proj(COL_GA, D_MODEL)).astype(ga_ref.dtype)
    gb_ref[...] = _sigmoid(proj(COL_GB, D_MODEL)).astype(gb_ref.dtype)
    gc_ref[...] = _sigmoid(proj(COL_GC, D_MODEL)).astype(gc_ref.dtype)


def _inproj_rest(x, gain, w_rest, lb_logits, *, tm, layer, hg_dtype, cq_dtype, rider=_NO_RIDER):
    n = x.shape[0]
    n_rest = N_IN - N_ATT
    s512 = pl.BlockSpec((tm, A_W), lambda i: (i, 0))
    s1024 = pl.BlockSpec((tm, D_MODEL), lambda i: (i, 0))

    def sds(width, dt):
        return jax.ShapeDtypeStruct((n, width), dt)

    outs = pl.pallas_call(
        functools.partial(_inproj_rest_kernel, layer=layer, rider_body=rider.body, n_rider_in=len(rider.inputs)),
        out_shape=[sds(A_W, BF16), sds(HG_W, hg_dtype), sds(HG_W, F32), sds(HG_W, hg_dtype), sds(HG_W, hg_dtype),
                   sds(HG_W, BF16), sds(XA_W, cq_dtype), sds(XA_W, BF16),
                   sds(D_MODEL, BF16), sds(D_MODEL, BF16), sds(D_MODEL, BF16)] + list(rider.out_shapes),
        grid=(n // tm,),
        in_specs=[pl.BlockSpec((tm, D_MODEL), lambda i: (i, 0)),
                  _resident((1, D_MODEL)),
                  _resident_columns(D_MODEL, N_ATT, n_rest),
                  _resident(lb_logits.shape)] + list(rider.in_specs),
        out_specs=[s512] * 8 + [s1024] * 3 + list(rider.out_specs),
        compiler_params=_params(("arbitrary",), VMEM_LIMIT_BYTES),
        name="inproj_rest",
    )(x, gain, w_rest, lb_logits, *rider.inputs)
    return outs[:11], outs[11:]


def _attn_prompt_kernel(q_ref, kc_ref, vc_ref, kp_ref, vp_ref, o_ref, st_ref, kbuf, vbuf, o_acc, st_acc,
                        *, tmv, dil, rps):
    n = pl.program_id(0)
    rstep = pl.program_id(1)
    blk = A_NKEY

    qi = lax.broadcasted_iota(jnp.int32, (blk, 2 * blk), 0)
    ki = lax.broadcasted_iota(jnp.int32, (blk, 2 * blk), 1)
    dist = qi + blk - ki
    band = (dist >= 0) & (dist <= A_NKEY)
    bias = jnp.where(band, 0.0, NEG_INF).astype(F32)
    bias_first = jnp.where(band & (ki >= blk), 0.0, NEG_INF).astype(F32)
    lane = lax.broadcasted_iota(jnp.int32, (blk, LANES), 1)
    low_head = lane < A_HEAD_DIM
    head_mask = (jnp.where(low_head, 1.0, 0.0).astype(BF16), jnp.where(low_head, 0.0, 1.0).astype(BF16))

    for rr in range(rps):
        cs = slice(rr * A_W, (rr + 1) * A_W)
        kbuf[rr, 0:blk, :] = kp_ref[:, cs]
        kbuf[rr, blk:, :] = kc_ref[:, cs]
        vbuf[rr, 0:blk, :] = vp_ref[:, cs]
        vbuf[rr, blk:, :] = vc_ref[:, cs]
        res = rstep * rps + rr

        def body(b, carry, cs=cs, res=res, rr=rr):
            r0 = pl.multiple_of(b * blk, blk)
            is_first = jnp.logical_and(n == 0, b == 0)
            bias_b = jnp.where(is_first, bias_first, bias)
            qb = q_ref[pl.ds(r0, blk), cs]
            kb = kbuf[rr, pl.ds(r0, 2 * blk), :]
            vb = vbuf[rr, pl.ds(r0, 2 * blk), :]
            out_rows = pl.ds(r0 * dil + res, blk, stride=dil)
            stats = jnp.zeros((blk, LANES), F32)
            for pair in range(A_HEADS // 2):
                sl = slice(pair * LANES, (pair + 1) * LANES)
                qp = qb[:, sl]
                kp = kb[:, sl]
                vp = vb[:, sl]
                outs = []
                for hh in range(2):
                    qm = qp * head_mask[hh]
                    s = pl.dot(qm, kp, trans_b=True) + bias_b
                    m = jnp.max(s, axis=-1, keepdims=True)
                    p = jnp.exp(s - m)
                    l = jnp.sum(p, axis=-1, keepdims=True)
                    pv = jnp.dot(p.astype(BF16), vp, preferred_element_type=F32)
                    outs.append(pv * (1.0 / l))
                    lse = m + jnp.log(l)
                    stats = jnp.where(lane == 2 * pair + hh, lse, stats)
                o_acc[pair, out_rows, :] = jnp.where(low_head, outs[0], outs[1])
            st_acc[out_rows, :] = stats
            return carry

        lax.fori_loop(0, tmv // blk, body, 0, unroll=min(ATTN_BLOCKS_IN_FLIGHT, tmv // blk))

    @pl.when(rstep == pl.num_programs(1) - 1)
    def _():
        for pair in range(A_HEADS // 2):
            o_ref[:, pair * LANES:(pair + 1) * LANES] = o_acc[pair].astype(o_ref.dtype)
        st_ref[...] = st_acc[...]


def _attn_prompt(qv, kv, vv, dil, *, rows_per_step):
    l = qv.shape[0]
    t = l * dil
    blk = A_NKEY
    tmv = max(rows_per_step // dil, blk)
    rps = min(dil, max(1, ATTN_BLOCKS_IN_FLIGHT * blk // tmv))
    assert t % (tmv * dil) == 0 and dil % rps == 0
    cur = pl.BlockSpec((tmv, rps * A_W), lambda n, r: (n, r))
    prev = pl.BlockSpec((blk, rps * A_W), lambda n, r: (jnp.maximum(n * (tmv // blk) - 1, 0), r))
    rows = tmv * dil
    return pl.pallas_call(
        functools.partial(_attn_prompt_kernel, tmv=tmv, dil=dil, rps=rps),
        out_shape=[jax.ShapeDtypeStruct((t, A_W), BF16), jax.ShapeDtypeStruct((t, LANES), F32)],
        grid=(t // rows, dil // rps),
        in_specs=[cur, cur, cur, prev, prev],
        out_specs=[pl.BlockSpec((rows, A_W), lambda n, r: (n, 0)), pl.BlockSpec((rows, LANES), lambda n, r: (n, 0))],
        scratch_shapes=[pltpu.VMEM((rps, tmv + blk, A_W), BF16), pltpu.VMEM((rps, tmv + blk, A_W), BF16),
                        pltpu.VMEM((A_HEADS // 2, rows, LANES), F32), pltpu.VMEM((rows, LANES), F32)],
        compiler_params=_params(("parallel", "arbitrary"), VMEM_LIMIT_BYTES),
        name=f"attn_prompt_d{dil}",
    )(qv, kv, vv, kv, vv)


def _hgrn_prompt_kernel(q_ref, lf_ref, k_ref, v_ref, gain_ref, o_ref, s_out_ref, st_ref, *, th,
                        co_scheduled=lambda: None):
    i = pl.program_id(0)
    c_rows = HG_CHUNK
    nblk = c_rows // HG_BLOCK
    shape3 = (nblk, HG_BLOCK, HG_K)

    @pl.when(i == 0)
    def _():
        st_ref[...] = jnp.zeros_like(st_ref)

    co_scheduled()

    sub = lax.broadcasted_iota(jnp.int32, shape3, 1)
    a_row = lax.broadcasted_iota(jnp.int32, (c_rows, c_rows), 0) // HG_BLOCK
    a_col = lax.broadcasted_iota(jnp.int32, (c_rows, c_rows), 1) // HG_BLOCK

    def shift(x3, s):
        return pltpu.roll(x3, s, 1)

    def chunk(c, carry):
        r0 = pl.multiple_of(c * c_rows, c_rows)
        for h in range(HG_HEADS):
            sl = slice(h * HG_K, (h + 1) * HG_K)
            q = q_ref[pl.ds(r0, c_rows), sl].astype(F32)
            k = k_ref[pl.ds(r0, c_rows), sl].astype(F32)
            v = v_ref[pl.ds(r0, c_rows), sl].astype(F32)
            q3, k3, v3 = (a.reshape(shape3) for a in (q, k, v))
            cum = lf_ref[pl.ds(r0, c_rows), sl].reshape(shape3)
            s = 1
            while s < HG_BLOCK:
                cum = cum + jnp.where(sub >= s, shift(cum, s), 0.0)
                s *= 2
            anchors = [jnp.zeros((1, 1, HG_K), F32)]
            for j in range(nblk):
                anchors.append(anchors[j] + cum[j:j + 1, HG_BLOCK - 1:HG_BLOCK, :])
            b3 = cum + jnp.concatenate(anchors[:nblk], axis=0)
            b = b3.reshape(c_rows, HG_K)
            b_last = anchors[nblk].reshape(1, HG_K)

            o3 = jnp.sum(q3 * k3, axis=-1, keepdims=True) * v3
            for d in range(1, HG_BLOCK):
                dec = jnp.exp(jnp.where(sub >= d, cum - shift(cum, d), NEG_INF))
                a = jnp.sum(q3 * shift(k3, d) * dec, axis=-1, keepdims=True)
                o3 = o3 + a * shift(v3, d)
            o = o3.reshape(c_rows, HG_K)

            b_end = jnp.concatenate(anchors[1:], axis=0)
            k_hat = (k3 * jnp.exp(b_end - b3)).reshape(c_rows, HG_K).astype(BF16)
            q_ver = [(q3 * jnp.exp(jnp.minimum(b3 - anchors[j + 1], 0.0))).reshape(c_rows, HG_K).astype(BF16)
                     for j in range(nblk - 1)]
            r = pl.dot(jnp.concatenate(q_ver, axis=0), k_hat, trans_b=True)
            att = jnp.zeros((c_rows, c_rows), F32)
            for j in range(nblk - 1):
                att = jnp.where(a_col == j, r[j * c_rows:(j + 1) * c_rows, :], att)
            att = jnp.where(a_row > a_col, att, 0.0)
            vb = v.astype(BF16)
            o = o + jnp.dot(att.astype(BF16), vb, preferred_element_type=F32)

            st = st_ref[h]
            o = o + pl.dot((q * jnp.exp(b)).astype(BF16), st.astype(BF16), trans_b=True)
            k_end = (k * jnp.exp(b_last - b)).astype(BF16)
            st_ref[h] = st * jnp.exp(b_last) + pl.dot(vb, k_end, trans_a=True)

            on = o * lax.rsqrt(jnp.mean(o * o, axis=-1, keepdims=True) + EPS) * gain_ref[:, sl]
            o_ref[pl.ds(r0, c_rows), sl] = on.astype(o_ref.dtype)
        return carry

    n_chunks = th // c_rows
    lax.fori_loop(0, n_chunks, chunk, 0, unroll=n_chunks <= 4)

    @pl.when(i == pl.num_programs(0) - 1)
    def _():
        for h in range(HG_HEADS):
            s_out_ref[h] = st_ref[h].T


def _mem_kv_kernel(m_ref, g_ref, w_ref, o_ref):
    xn = _rmsnorm_rows(m_ref[...], g_ref[...]).astype(BF16)
    o_ref[...] = jnp.dot(xn, w_ref[...], preferred_element_type=F32)


def _mem_kv(mem, gain, w):
    m = mem.shape[0]
    return pl.pallas_call(
        _mem_kv_kernel,
        out_shape=jax.ShapeDtypeStruct((m, 2 * XA_W), F32),
        name="mem_kv",
    )(mem, gain, w)


def _softmax_rows(s):
    m = jnp.max(s, axis=-1, keepdims=True)
    p = jnp.exp(s - m)
    return p, jnp.sum(p, axis=-1, keepdims=True)


def _xattn_rows(cq, mk_ref, mv_ref):
    parts = []
    for h in range(XA_HEADS):
        sl = slice(h * XA_HEAD_DIM, (h + 1) * XA_HEAD_DIM)
        s = pl.dot(cq[:, sl], mk_ref[:, sl], trans_b=True) * (XA_HEAD_DIM ** -0.5)
        p, l = _softmax_rows(s)
        parts.append(jnp.dot(p.astype(BF16), mv_ref[:, sl], preferred_element_type=F32) * (1.0 / l))
    return jnp.concatenate(parts, axis=-1)


def _final_kernel(*refs, tm, merge, xattn):
    refs = list(refs)
    if merge:
        o_refs = [refs.pop(0) for _ in range(A_GROUPS)]
        st_refs = [refs.pop(0) for _ in range(A_GROUPS)]
    else:
        oa_ref = refs.pop(0)
    if xattn:
        cq_ref, mk_ref, mv_ref = (refs.pop(0) for _ in range(3))
    else:
        oc_ref = refs.pop(0)
    (ob_ref, az_ref, bz_ref, cz_ref, ga_ref, gb_ref, gc_ref, x_ref, pg_ref,
     wa_ref, wb_ref, wc_ref, wo_ref, out_ref) = refs

    if merge:
        lse = [r[...] for r in st_refs]
        mx = jnp.maximum(jnp.maximum(lse[0], lse[1]), lse[2])
        ex = [jnp.exp(x - mx) for x in lse]
        inv = 1.0 / (ex[0] + ex[1] + ex[2])
        e_row = lax.broadcasted_iota(jnp.int32, (LANES, A_W), 0)
        e_col = lax.broadcasted_iota(jnp.int32, (LANES, A_W), 1) // A_HEAD_DIM
        spread = jnp.where(e_row == e_col, 1.0, 0.0).astype(BF16)
        oa = jnp.zeros((tm, A_W), F32)
        for g in range(A_GROUPS):
            alpha = (ex[g] * inv).astype(BF16)
            w = jnp.dot(alpha, spread, preferred_element_type=F32)
            oa = oa + w * o_refs[g][...].astype(F32)
    else:
        oa = oa_ref[...].astype(F32)
    ya = (oa * az_ref[...].astype(F32)).astype(BF16)

    oc = _xattn_rows(cq_ref[...], mk_ref, mv_ref) if xattn else oc_ref[...].astype(F32)
    yc = (oc * cz_ref[...].astype(F32)).astype(BF16)

    yb = (ob_ref[...].astype(F32) * bz_ref[...].astype(F32)).astype(BF16)

    merged = (ga_ref[...].astype(F32) * jnp.dot(ya, wa_ref[...], preferred_element_type=F32)
              + gb_ref[...].astype(F32) * jnp.dot(yb, wb_ref[...], preferred_element_type=F32)
              + gc_ref[...].astype(F32) * jnp.dot(yc, wc_ref[...], preferred_element_type=F32))
    z = jnp.dot(merged.astype(BF16), wo_ref[...], preferred_element_type=F32)
    out_ref[...] = x_ref[...] + _rmsnorm_rows(z, pg_ref[...])


def _final(att_in, xa_in, ob, az, bz, cz, ga, gb, gc, x, post_g, w_pa, w_pb, w_pc, w_out, *, tm, merge, xattn):
    n = x.shape[0]

    def rows(width):
        return pl.BlockSpec((tm, width), lambda i: (i, 0))

    if merge:
        att_specs = [rows(A_W)] * A_GROUPS + [rows(LANES)] * A_GROUPS
    else:
        att_specs = [rows(A_W)]
    if xattn:
        xa_specs = [rows(XA_W), _resident((MEM_LEN, XA_W)), _resident((MEM_LEN, XA_W))]
    else:
        xa_specs = [rows(XA_W)]
    in_specs = (att_specs + xa_specs + [rows(HG_W), rows(A_W), rows(HG_W), rows(XA_W)]
                + [rows(D_MODEL)] * 4 + [_resident((1, D_MODEL))]
                + [_resident((A_W, D_MODEL)), _resident((HG_W, D_MODEL)), _resident((XA_W, D_MODEL)),
                   _resident((D_MODEL, D_MODEL))])
    return pl.pallas_call(
        functools.partial(_final_kernel, tm=tm, merge=merge, xattn=xattn),
        out_shape=jax.ShapeDtypeStruct((n, D_MODEL), F32),
        grid=(n // tm,),
        in_specs=in_specs,
        out_specs=rows(D_MODEL),
        compiler_params=_params(("parallel",), VMEM_LIMIT_BYTES),
        name="final_merge" if merge else "final",
    )(*att_in, *xa_in, ob, az, bz, cz, ga, gb, gc, x, post_g, w_pa, w_pb, w_pc, w_out)


def _head_rows(q_row, head_dim):
    w = q_row.shape[-1]
    assert w // head_dim <= SUBLANES
    sub = lax.broadcasted_iota(jnp.int32, (SUBLANES, w), 0)
    lane_head = lax.broadcasted_iota(jnp.int32, (SUBLANES, w), 1) // head_dim
    keep = sub == lane_head
    return jnp.where(keep, jnp.broadcast_to(q_row, (SUBLANES, w)), 0.0), keep


def _window_cache_view(cache, g):
    bd, wb = cache.shape[0], cache.shape[1]
    assert wb == A_NKEY * A_DILATIONS[g], "window buffer must hold exactly 128 dilated keys"
    return jnp.transpose(cache, (0, 2, 3, 4, 1)).reshape(bd, 2, A_W, wb)


def _attn_sample_group(dil, q_row, k_new, v_new, c_ref):
    wb = c_ref.shape[-1]
    qm, _ = _head_rows(q_row, A_HEAD_DIM)
    kt = c_ref[0].astype(BF16)
    vt = c_ref[1].astype(BF16)
    s = jnp.dot(qm.astype(BF16), kt, preferred_element_type=F32)
    w_pos = lax.broadcasted_iota(jnp.int32, (SUBLANES, wb), 1)
    s = jnp.where(w_pos % dil == 0, s, NEG_INF)
    s_new = jnp.sum(qm * k_new, axis=-1, keepdims=True)
    m = jnp.maximum(jnp.max(s, axis=-1, keepdims=True), s_new)
    p = jnp.exp(s - m)
    p_new = jnp.exp(s_new - m)
    l = jnp.sum(p, axis=-1, keepdims=True) + p_new
    o = (pl.dot(p.astype(BF16), vt, trans_b=True) + p_new * v_new) * (1.0 / l)
    return o, m + jnp.log(l)


def _attn_sample_kernel(q0, q1, q2, k0, k1, k2, v0, v1, v2, c0, c1, c2, o_ref, *, sb):
    q_refs, k_refs, v_refs, c_refs = (q0, q1, q2), (k0, k1, k2), (v0, v1, v2), (c0, c1, c2)
    _, keep = _head_rows(jnp.zeros((1, A_W), F32), A_HEAD_DIM)
    for b in range(sb):
        row = pl.ds(pl.program_id(0) * sb + b, 1)
        outs = []
        lses = []
        for g in range(A_GROUPS):
            o, lse = _attn_sample_group(A_DILATIONS[g], q_refs[g][row, :], k_refs[g][row, :], v_refs[g][row, :],
                                        c_refs[g].at[b])
            outs.append(o)
            lses.append(lse)
        mx = jnp.maximum(jnp.maximum(lses[0], lses[1]), lses[2])
        ex = [jnp.exp(x - mx) for x in lses]
        inv = 1.0 / (ex[0] + ex[1] + ex[2])
        oa = (ex[0] * outs[0] + ex[1] * outs[1] + ex[2] * outs[2]) * inv
        o_ref[row, :] = jnp.sum(jnp.where(keep, oa, 0.0), axis=0, keepdims=True)


def _hgrn_and_attn_sample_kernel(*refs, th, sb):
    hg_in, at_in = refs[0:5], refs[5:17]
    hg_out, at_out, hg_state = refs[17:19], refs[19], refs[20]
    _hgrn_prompt_kernel(*hg_in, *hg_out, hg_state, th=th,
                        co_scheduled=lambda: _attn_sample_kernel(*at_in, at_out, sb=sb))


def _hgrn_and_attn_sample(q, lf, k, v, gain, qs, ks, vs, caches, *, seqs_per_step):
    t = q.shape[0]
    bd = qs[0].shape[0]
    sb = seqs_per_step
    steps = bd // sb
    assert bd % sb == 0 and t % steps == 0 and (t // steps) % HG_CHUNK == 0
    th = t // steps
    views = [_window_cache_view(caches[g], g) for g in range(A_GROUPS)]
    c_specs = [pl.BlockSpec((sb,) + v_.shape[1:], lambda b: (b, 0, 0, 0)) for v_ in views]
    rows = pl.BlockSpec((th, HG_W), lambda i: (i, 0))
    full = pl.BlockSpec((bd, A_W), lambda b: (0, 0))
    state = pl.BlockSpec((HG_HEADS, HG_K, HG_V), lambda i: (0, 0, 0))
    return pl.pallas_call(
        functools.partial(_hgrn_and_attn_sample_kernel, th=th, sb=sb),
        out_shape=[jax.ShapeDtypeStruct((t, HG_W), BF16), jax.ShapeDtypeStruct((HG_HEADS, HG_K, HG_V), F32),
                   jax.ShapeDtypeStruct((bd, A_W), F32)],
        grid=(steps,),
        in_specs=[rows] * 4 + [_resident((1, HG_W))] + [full] * 9 + c_specs,
        out_specs=[rows, state, full],
        scratch_shapes=[pltpu.VMEM((HG_HEADS, HG_V, HG_K), F32)],
        compiler_params=_params(("arbitrary",), VMEM_LIMIT_BYTES),
        name="hgrn_prompt_attn_sample",
    )(q, lf, k, v, gain, *qs, *ks, *vs, *views)


def _xattn_sample_body(step, q_ref, c_ref, o_ref, *, sb):
    stride = 2 * XA_HEADS
    for b in range(sb):
        row = pl.ds(step * sb + b, 1)
        q_row = q_ref[row, :]
        outs = []
        for h in range(XA_HEADS):
            sl = slice(h * XA_HEAD_DIM, (h + 1) * XA_HEAD_DIM)
            kh = c_ref[b, pl.ds(h, MEM_LEN, stride=stride), :]
            vh = c_ref[b, pl.ds(XA_HEADS + h, MEM_LEN, stride=stride), :]
            s = jnp.sum(kh * q_row[:, sl], axis=-1, keepdims=True) * (XA_HEAD_DIM ** -0.5)
            m = jnp.max(s, axis=0, keepdims=True)
            p = jnp.exp(s - m)
            l = jnp.sum(p, axis=0, keepdims=True)
            outs.append(jnp.sum(p * vh, axis=0, keepdims=True) * (1.0 / l))
        o_ref[row, :] = jnp.concatenate(outs, axis=-1)


def _xattn_sample_rider(cq, mem_cache, *, steps):
    bd = cq.shape[0]
    assert bd % steps == 0
    sb = bd // steps
    rows = MEM_LEN * 2 * XA_HEADS
    view = mem_cache.reshape(bd, rows, XA_HEAD_DIM)
    full = pl.BlockSpec((bd, XA_W), lambda i: (0, 0))
    return _Rider(
        body=functools.partial(_xattn_sample_body, sb=sb),
        inputs=(cq, view),
        in_specs=(full, pl.BlockSpec((sb, rows, XA_HEAD_DIM), lambda i: (i, 0, 0))),
        out_shapes=(jax.ShapeDtypeStruct((bd, XA_W), F32),),
        out_specs=(full,))


def _hgrn_sample_body(step, qt_ref, lft_ref, kt_ref, v_ref, s0_ref, gain_ref, o_ref, s_ref, *, sb):
    for b in range(sb):
        row = pl.ds(step * sb + b, 1)
        v_all = v_ref[row, :]
        outs = []
        for h in range(HG_HEADS):
            sl = slice(h * HG_K, (h + 1) * HG_K)
            d_col = jnp.exp(lft_ref[0, sl, b:b + 1])
            k_col = kt_ref[0, sl, b:b + 1]
            q_col = qt_ref[0, sl, b:b + 1]
            s_new = d_col * s0_ref[b, h] + k_col * v_all[:, sl]
            s_ref[b, h] = s_new
            o = jnp.sum(q_col * s_new, axis=0, keepdims=True)
            outs.append(o * lax.rsqrt(jnp.mean(o * o, axis=-1, keepdims=True) + EPS) * gain_ref[:, sl])
        o_ref[row, :] = jnp.concatenate(outs, axis=-1)


def _hgrn_sample_rider(q, lf, kin, v, s0, gain, *, steps):
    bd = q.shape[0]
    assert bd % steps == 0
    sb = bd // steps

    def cols(a):
        return a.reshape(steps, sb, HG_W).transpose(0, 2, 1)

    col_spec = pl.BlockSpec((1, HG_W, sb), lambda i: (i, 0, 0))
    st_spec = pl.BlockSpec((sb, HG_HEADS, HG_K, HG_V), lambda i: (i, 0, 0, 0))
    full = pl.BlockSpec((bd, HG_W), lambda i: (0, 0))
    return _Rider(
        body=functools.partial(_hgrn_sample_body, sb=sb),
        inputs=(cols(q), cols(lf), cols(kin), v, s0, gain),
        in_specs=(col_spec, col_spec, col_spec, full, st_spec, _resident((1, HG_W))),
        out_shapes=(jax.ShapeDtypeStruct((bd, HG_W), F32), jax.ShapeDtypeStruct(s0.shape, F32)),
        out_specs=(full, st_spec))


def kernel(x_prompt, x_sample, mem_prompt, cache_win128_kv, cache_win512_kv, cache_win2048_kv, state_hgrn, cache_mem_kv, norm_pre, norm_post, w_in, hgrn_lb_logits, hgrn_out_norm, mem_norm, w_mem_kv, w_branch_a, w_branch_b, w_branch_c, w_out):
    depth = w_in.shape[0]
    assert depth == 1, "single-layer trunk"
    layer = 0
    bp, t, _ = x_prompt.shape
    bd, s_len, _ = x_sample.shape
    assert bp == 1 and s_len == 1
    caches = (cache_win128_kv[layer], cache_win512_kv[layer], cache_win2048_kv[layer])

    w_att = w_rest = w_in[layer].astype(BF16)
    pre_g = norm_pre[layer].reshape(1, D_MODEL)
    post_g = norm_post[layer].reshape(1, D_MODEL)
    hg_g = hgrn_out_norm[layer].reshape(1, HG_W)
    w_pa, w_pb, w_pc, w_o = (w[layer].astype(BF16) for w in (w_branch_a, w_branch_b, w_branch_c, w_out))
    half = A_HEAD_DIM // 2
    inv = ROPE_THETA ** (-jnp.arange(0, A_HEAD_DIM, 2, dtype=F32) / A_HEAD_DIM)
    invf = jnp.tile(inv, LANES // half).reshape(1, LANES)

    xs = x_sample.reshape(bd, D_MODEL)
    q_s, k_s, v_s, tails_s, _ = _inproj_attn(xs, pre_g, w_att, invf, tm=bd, pos_base=PAST_LEN, pos_step=0,
                                             tail_rows=(bd,) * A_GROUPS, out_dtype=F32, row_dil=(1,) * A_GROUPS)
    (az_s, bq_s, lf_s, kin_s, bi_s, bz_s, cq_s, cz_s, ga_s, gb_s, gc_s), _ = _inproj_rest(
        xs, pre_g, w_rest, hgrn_lb_logits, tm=bd, layer=layer, hg_dtype=F32, cq_dtype=F32)

    xp = x_prompt.reshape(t, D_MODEL)
    tile = _tiling(t, bd)
    tm_p = tile.proj_rows
    steps_p = t // tm_p
    tail_rows = tuple(min(w, t) for w in A_WINDOWS)
    q_p, k_p, v_p, tails_p, _ = _inproj_attn(
        xp, pre_g, w_att, invf, tm=tm_p, pos_base=0, pos_step=1, tail_rows=tail_rows, out_dtype=BF16,
        row_dil=A_DILATIONS)
    (az, bq, lf, kin, bi, bz, cq, cz, ga, gb, gc), (ob_s, s_s, oc_s) = _inproj_rest(
        xp, pre_g, w_rest, hgrn_lb_logits, tm=tm_p, layer=layer, hg_dtype=BF16, cq_dtype=BF16,
        rider=_join_riders(_hgrn_sample_rider(bq_s, lf_s, kin_s, bi_s, state_hgrn[layer], hg_g, steps=steps_p),
                           _xattn_sample_rider(cq_s, cache_mem_kv[layer], steps=steps_p)))
    att = [_attn_prompt(q_p[g], k_p[g], v_p[g], A_DILATIONS[g], rows_per_step=tile.attn_rows)
           for g in range(A_GROUPS)]
    mem_kv = _mem_kv(mem_prompt.reshape(MEM_LEN, D_MODEL), mem_norm[layer].reshape(1, D_MODEL),
                     w_mem_kv[layer].astype(BF16))
    mk = mem_kv[:, :XA_W].astype(BF16)
    mv = mem_kv[:, XA_W:].astype(BF16)
    ob_p, s_p, oa_s = _hgrn_and_attn_sample(bq, lf, kin, bi, hg_g, q_s, k_s, v_s, caches,
                                            seqs_per_step=tile.fused_seqs)
    y_p = _final([a[0] for a in att] + [a[1] for a in att], [cq, mk, mv], ob_p, az, bz, cz, ga, gb, gc,
                 xp, post_g, w_pa, w_pb, w_pc, w_o, tm=tile.final_rows, merge=True, xattn=True)

    y_s = _final([oa_s], [oc_s], ob_s, az_s, bz_s, cz_s, ga_s, gb_s, gc_s,
                 xs, post_g, w_pa, w_pb, w_pc, w_o, tm=bd, merge=False, xattn=False)

    new_win_p = [tails_p[g].reshape(1, 1, tail_rows[g], 2, A_HEADS, A_HEAD_DIM) for g in range(A_GROUPS)]
    new_win_s = [tails_s[g].reshape(1, bd, 1, 2, A_HEADS, A_HEAD_DIM) for g in range(A_GROUPS)]
    return (y_p.reshape(bp, t, D_MODEL), y_s.reshape(bd, 1, D_MODEL),
            new_win_p[0], new_win_p[1], new_win_p[2],
            s_p.reshape(1, 1, HG_HEADS, HG_K, HG_V),
            mem_kv.reshape(1, 1, MEM_LEN, 2, XA_HEADS, XA_HEAD_DIM),
            new_win_s[0], new_win_s[1], new_win_s[2],
            s_s.reshape(1, bd, HG_HEADS, HG_K, HG_V))
```

```python
import functools
from typing import Callable, NamedTuple

import jax
import jax.numpy as jnp
from jax import lax
from jax.experimental import pallas as pl
from jax.experimental.pallas import tpu as pltpu

D_MODEL = 1024
PAST_LEN = 8192
A_WINDOWS = (128, 512, 2048)
A_DILATIONS = (1, 4, 16)
A_GROUPS = 3
A_HEADS = 8
A_HEAD_DIM = 64
A_NKEY = 128
ROPE_THETA = 10000.0
HG_HEADS = 4
HG_K = 128
HG_V = 128
HG_CHUNK = 64
HG_BLOCK = 8
ATTN_BLOCKS_IN_FLIGHT = 16
MEM_LEN = 256
XA_HEADS = 4
XA_HEAD_DIM = 128
EPS = 1e-6

A_W = A_HEADS * A_HEAD_DIM
HG_W = HG_HEADS * HG_K
XA_W = XA_HEADS * XA_HEAD_DIM
N_ATT = 3 * A_GROUPS * A_W
COL_AZ = N_ATT
COL_BQ = COL_AZ + A_W
COL_BF = COL_BQ + HG_W
COL_BI = COL_BF + HG_W
COL_BZ = COL_BI + HG_W
COL_CQ = COL_BZ + HG_W
COL_CZ = COL_CQ + XA_W
COL_GA = COL_CZ + XA_W
COL_GB = COL_GA + D_MODEL
COL_GC = COL_GB + D_MODEL
N_IN = COL_GC + D_MODEL

LANES = 128
SUBLANES = 8
VMEM_LIMIT_BYTES = 56 * 1024 * 1024


class _Tiling(NamedTuple):
    proj_rows: int
    final_rows: int
    attn_rows: int
    fused_seqs: int


def _tiling(t, bd):
    tile = _Tiling(proj_rows=512, final_rows=512, attn_rows=2048, fused_seqs=2)
    assert t % tile.attn_rows == 0 and t % tile.proj_rows == 0 and t % tile.final_rows == 0
    assert bd % tile.fused_seqs == 0 and bd % (t // tile.proj_rows) == 0
    return tile

F32 = jnp.float32
BF16 = jnp.bfloat16
NEG_INF = float("-inf")


def _params(semantics, vmem=None):
    return pltpu.CompilerParams(dimension_semantics=semantics, vmem_limit_bytes=vmem)


def _resident(shape):
    return pl.BlockSpec(shape, lambda *_: (0,) * len(shape), pipeline_mode=pl.Buffered(1))


def _resident_columns(rows, col0, width):
    return pl.BlockSpec((pl.Element(rows), pl.Element(width)), lambda *_: (0, col0), pipeline_mode=pl.Buffered(1))


class _Rider(NamedTuple):
    body: Callable
    inputs: tuple = ()
    in_specs: tuple = ()
    out_shapes: tuple = ()
    out_specs: tuple = ()


_NO_RIDER = _Rider(body=lambda step: None)


def _join_riders(a, b):
    na_in, nb_in, na_out = len(a.inputs), len(b.inputs), len(a.out_shapes)

    def body(step, *refs):
        ins, outs = refs[:na_in + nb_in], refs[na_in + nb_in:]
        a.body(step, *ins[:na_in], *outs[:na_out])
        b.body(step, *ins[na_in:], *outs[na_out:])

    return _Rider(body, a.inputs + b.inputs, a.in_specs + b.in_specs, a.out_shapes + b.out_shapes,
                  a.out_specs + b.out_specs)


def _rmsnorm_rows(x, gain):
    ms = jnp.mean(x * x, axis=-1, keepdims=True)
    return x * lax.rsqrt(ms + EPS) * gain


def _sigmoid(x):
    return 0.5 * jnp.tanh(0.5 * x) + 0.5


def _silu(x):
    return x * _sigmoid(x)


def _tile_row_order(tm, d):
    j = lax.broadcasted_iota(jnp.int32, (tm, LANES), 0)
    per = tm // d
    return (j % per) * d + j // per


def _inproj_attn_kernel(x_ref, g_ref, w_ref, invf_ref, *refs, tm, n_tiles, pos_base, pos_step, tails, row_dil,
                        rider_body, n_rider_in, n_rider_out):
    rider_in, refs = refs[:n_rider_in], refs[n_rider_in:]
    q_refs = refs[0:3]
    k_refs = refs[3:6]
    v_refs = refs[6:9]
    t_refs = refs[9:12]
    rider_out = refs[12:12 + n_rider_out]
    cos_tab, sin_tab, xa_ref, xb_ref, xp_ref = refs[12 + n_rider_out:]
    i = pl.program_id(0)
    n_slab = A_W // LANES
    n_xslab = D_MODEL // LANES
    dils = sorted(set(row_dil))

    @pl.when(i == 0)
    def _():
        for t_idx, d in enumerate(dils):
            ang0 = (_tile_row_order(tm, d) * pos_step).astype(F32) * invf_ref[...]
            cos_tab[t_idx] = jnp.cos(ang0)
            sin_tab[t_idx] = jnp.sin(ang0)

    rider_body(i, *rider_in, *rider_out)

    lane = lax.broadcasted_iota(jnp.int32, (tm, LANES), 1)
    first_half = (lane % A_HEAD_DIM) < (A_HEAD_DIM // 2)
    base =(pos_base + i * tm * pos_step).astype(F32) * invf_ref[...]
    cos_b = jnp.cos(base)
    sin_b = jnp.sin(base)

    def trig(d):
        t_idx = dils.index(d)
        c0, s0 = cos_tab[t_idx], sin_tab[t_idx]
        cos = c0 * cos_b - s0 * sin_b
        sin = s0 * cos_b + c0 * sin_b
        return cos, jnp.where(first_half, -sin, sin)

    def rope(acc, cos, sin_signed):
        outs = []
        for c in range(n_slab):
            xc = acc[:, c * LANES:(c + 1) * LANES]
            partner = jnp.where(first_half, pltpu.roll(xc, LANES - 32, 1), pltpu.roll(xc, 32, 1))
            outs.append(xc * cos + partner * sin_signed)
        return jnp.concatenate(outs, axis=-1)

    xn32 = _rmsnorm_rows(x_ref[...], g_ref[...])
    xn = xn32.astype(BF16)

    lhs = {1: xn}
    if max(row_dil) > 1:
        assert set(row_dil) <= {1, 4, 16}
        per4 = tm // 4
        for c in range(n_xslab):
            xa_ref[c] = xn32[:, c * LANES:(c + 1) * LANES]
        for c in range(n_xslab):
            for r in range(4):
                blk4 = xa_ref[c, pl.ds(r, per4, stride=4), :]
                xb_ref[c, r * per4:(r + 1) * per4, :] = blk4
                xp_ref[0, r * per4:(r + 1) * per4, c * LANES:(c + 1) * LANES] = blk4.astype(BF16)
        lhs[4] = xp_ref[0]
        if 16 in row_dil:
            per16 = tm // 16
            for c in range(n_xslab):
                for r1 in range(4):
                    for r2 in range(4):
                        res = 4 * r2 + r1
                        xp_ref[1, res * per16:(res + 1) * per16, c * LANES:(c + 1) * LANES] = (
                            xb_ref[c, pl.ds(r1 * per4 + r2, per16, stride=4), :].astype(BF16))
            lhs[16] = xp_ref[1]

    def store_rows(ref, d, val):
        per = tm // d
        for r in range(d):
            ref[:, r * A_W:(r + 1) * A_W] = val[r * per:(r + 1) * per, :].astype(ref.dtype)

    def tail_block(g, col0, natural_val):
        first_tile, rows = tails[g]
        cond = (i >= first_tile) if rows >= tm else (i == n_tiles - 1)

        @pl.when(cond)
        def _():
            val = natural_val()
            t_refs[g][:, col0:col0 + A_W] = val if rows >= tm else val[tm - rows:, :]

    cos_n, sin_n = trig(1)
    for g in range(A_GROUPS):
        d = row_dil[g]
        cos, sin_s = (cos_n, sin_n) if d == 1 else trig(d)
        wq = w_ref[:, g * A_W:(g + 1) * A_W]
        wk = w_ref[:, (A_GROUPS + g) * A_W:(A_GROUPS + g + 1) * A_W]
        wv = w_ref[:, (2 * A_GROUPS + g) * A_W:(2 * A_GROUPS + g + 1) * A_W]
        q = rope(jnp.dot(lhs[d], wq, preferred_element_type=F32), cos, sin_s) * (A_HEAD_DIM ** -0.5)
        store_rows(q_refs[g], d, q)
        kr = rope(jnp.dot(lhs[d], wk, preferred_element_type=F32), cos, sin_s)
        store_rows(k_refs[g], d, kr)
        v = jnp.dot(lhs[d], wv, preferred_element_type=F32)
        store_rows(v_refs[g], d, v)
        if d == 1:
            tail_block(g, 0, lambda kr=kr: kr)
            tail_block(g, A_W, lambda v=v: v)
        else:
            tail_block(g, 0, lambda wk=wk: rope(jnp.dot(xn, wk, preferred_element_type=F32), cos_n, sin_n))
            tail_block(g, A_W, lambda wv=wv: jnp.dot(xn, wv, preferred_element_type=F32))


def _inproj_attn(x, gain, w_att, invf, *, tm, pos_base, pos_step, tail_rows, out_dtype, row_dil, rider=_NO_RIDER):
    n = x.shape[0]
    n_tiles = n // tm
    assert 1 in row_dil
    n_orders = len(set(row_dil))
    tails = []
    tail_specs = []
    tail_shapes = []
    for g in range(A_GROUPS):
        rows = tail_rows[g]
        first_tile = (n - rows) // tm
        tails.append((first_tile, rows))
        blk = min(tm, rows)
        tail_specs.append(pl.BlockSpec((blk, 2 * A_W), functools.partial(
            lambda i, ft: (jnp.maximum(i - ft, 0), 0), ft=first_tile)))
        tail_shapes.append(jax.ShapeDtypeStruct((rows, 2 * A_W), F32))
    row_specs = [pl.BlockSpec((tm // d, d * A_W), lambda i: (i, 0)) for d in row_dil] * 3
    row_shapes = [jax.ShapeDtypeStruct((n // d, d * A_W), out_dtype) for d in row_dil] * 3
    kernel = functools.partial(_inproj_attn_kernel, tm=tm, n_tiles=n_tiles, pos_base=pos_base,
                               pos_step=pos_step, tails=tuple(tails), row_dil=tuple(row_dil),
                               rider_body=rider.body, n_rider_in=len(rider.inputs), n_rider_out=len(rider.out_shapes))
    outs = pl.pallas_call(
        kernel,
        out_shape=row_shapes + tail_shapes + list(rider.out_shapes),
        grid=(n_tiles,),
        in_specs=[pl.BlockSpec((tm, D_MODEL), lambda i: (i, 0)),
                  _resident((1, D_MODEL)),
                  _resident_columns(D_MODEL, 0, N_ATT),
                  _resident((1, LANES))] + list(rider.in_specs),
        out_specs=row_specs + tail_specs + list(rider.out_specs),
        scratch_shapes=[pltpu.VMEM((n_orders, tm, LANES), F32), pltpu.VMEM((n_orders, tm, LANES), F32),
                        pltpu.VMEM((D_MODEL // LANES, tm, LANES), F32), pltpu.VMEM((D_MODEL // LANES, tm, LANES), F32),
                        pltpu.VMEM((2, tm, D_MODEL), BF16)],
        compiler_params=_params(("arbitrary",), VMEM_LIMIT_BYTES),
        name="inproj_attn",
    )(x, gain, w_att, invf, *rider.inputs)
    return outs[0:3], outs[3:6], outs[6:9], outs[9:12], outs[12:]


def _inproj_rest_kernel(x_ref, g_ref, w_ref, lbl_ref, *refs, layer, rider_body, n_rider_in):
    rider_in, refs = refs[:n_rider_in], refs[n_rider_in:]
    (az_ref, bq_ref, lf_ref, kin_ref, bi_ref, bz_ref, cq_ref, cz_ref, ga_ref, gb_ref, gc_ref) = refs[:11]
    rider_body(pl.program_id(0), *rider_in, *refs[11:])
    xn = _rmsnorm_rows(x_ref[...], g_ref[...]).astype(BF16)

    def proj(col, width):
        c0 = col - N_ATT
        return jnp.dot(xn, w_ref[:, c0:c0 + width], preferred_element_type=F32)

    logits = lbl_ref[...]
    e = jnp.exp(logits - jnp.max(logits, axis=0, keepdims=True))
    lb = jnp.sum(e[0:layer + 1, :], axis=0, keepdims=True) / jnp.sum(e, axis=0, keepdims=True)

    az_ref[...] = _silu(proj(COL_AZ, A_W)).astype(az_ref.dtype)
    bq_ref[...] = proj(COL_BQ, HG_W).astype(bq_ref.dtype)
    bf = proj(COL_BF, HG_W)
    gate = _sigmoid(bf)
    lf_ref[...] = jnp.log(lb + (1.0 - lb) * gate)
    kin_ref[...] = ((1.0 - lb) * (1.0 - gate)).astype(kin_ref.dtype)
    bi_ref[...] = proj(COL_BI, HG_W).astype(bi_ref.dtype)
    bz_ref[...] = _silu(proj(COL_BZ, HG_W)).astype(bz_ref.dtype)
    cq_ref[...] = proj(COL_CQ, XA_W).astype(cq_ref.dtype)
    cz_ref[...] = _silu(proj(COL_CZ, XA_W)).astype(cz_ref.dtype)
    ga_ref[...] = _sigmoid(proj(COL_GA, D_MODEL)).astype(ga_ref.dtype)
    gb_ref[...] = _sigmoid(proj(COL_GB, D_MODEL)).astype(gb_ref.dtype)
    gc_ref[...] = _sigmoid(proj(COL_GC, D_MODEL)).astype(gc_ref.dtype)


def _inproj_rest(x, gain, w_rest, lb_logits, *, tm, layer, hg_dtype, cq_dtype, rider=_NO_RIDER):
    n = x.shape[0]
    n_rest = N_IN - N_ATT
    s512 = pl.BlockSpec((tm, A_W), lambda i: (i, 0))
    s1024 = pl.BlockSpec((tm, D_MODEL), lambda i: (i, 0))

    def sds(width, dt):
        return jax.ShapeDtypeStruct((n, width), dt)

    outs = pl.pallas_call(
        functools.partial(_inproj_rest_kernel, layer=layer, rider_body=rider.body, n_rider_in=len(rider.inputs)),
        out_shape=[sds(A_W, BF16), sds(HG_W, hg_dtype), sds(HG_W, F32), sds(HG_W, hg_dtype), sds(HG_W, hg_dtype),
                   sds(HG_W, BF16), sds(XA_W, cq_dtype), sds(XA_W, BF16),
                   sds(D_MODEL, BF16), sds(D_MODEL, BF16), sds(D_MODEL, BF16)] + list(rider.out_shapes),
        grid=(n // tm,),
        in_specs=[pl.BlockSpec((tm, D_MODEL), lambda i: (i, 0)),
                  _resident((1, D_MODEL)),
                  _resident_columns(D_MODEL, N_ATT, n_rest),
                  _resident(lb_logits.shape)] + list(rider.in_specs),
        out_specs=[s512] * 8 + [s1024] * 3 + list(rider.out_specs),
        compiler_params=_params(("arbitrary",), VMEM_LIMIT_BYTES),
        name="inproj_rest",
    )(x, gain, w_rest, lb_logits, *rider.inputs)
    return outs[:11], outs[11:]


def _attn_prompt_kernel(q_ref, kc_ref, vc_ref, kp_ref, vp_ref, o_ref, st_ref, kbuf, vbuf, o_acc, st_acc,
                        *, tmv, dil, rps):
    n = pl.program_id(0)
    rstep = pl.program_id(1)
    blk = A_NKEY

    qi = lax.broadcasted_iota(jnp.int32, (blk, 2 * blk), 0)
    ki = lax.broadcasted_iota(jnp.int32, (blk, 2 * blk), 1)
    dist = qi + blk - ki
    band = (dist >= 0) & (dist <= A_NKEY)
    bias = jnp.where(band, 0.0, NEG_INF).astype(F32)
    bias_first = jnp.where(band & (ki >= blk), 0.0, NEG_INF).astype(F32)
    lane = lax.broadcasted_iota(jnp.int32, (blk, LANES), 1)
    low_head = lane < A_HEAD_DIM
    head_mask = (jnp.where(low_head, 1.0, 0.0).astype(BF16), jnp.where(low_head, 0.0, 1.0).astype(BF16))

    for rr in range(rps):
        cs = slice(rr * A_W, (rr + 1) * A_W)
        kbuf[rr, 0:blk, :] = kp_ref[:, cs]
        kbuf[rr, blk:, :] = kc_ref[:, cs]
        vbuf[rr, 0:blk, :] = vp_ref[:, cs]
        vbuf[rr, blk:, :] = vc_ref[:, cs]
        res = rstep * rps + rr

        def body(b, carry, cs=cs, res=res, rr=rr):
            r0 = pl.multiple_of(b * blk, blk)
            is_first = jnp.logical_and(n == 0, b == 0)
            bias_b = jnp.where(is_first, bias_first, bias)
            qb = q_ref[pl.ds(r0, blk), cs]
            kb = kbuf[rr, pl.ds(r0, 2 * blk), :]
            vb = vbuf[rr, pl.ds(r0, 2 * blk), :]
            out_rows = pl.ds(r0 * dil + res, blk, stride=dil)
            stats = jnp.zeros((blk, LANES), F32)
            for pair in range(A_HEADS // 2):
                sl = slice(pair * LANES, (pair + 1) * LANES)
                qp = qb[:, sl]
                kp = kb[:, sl]
                vp = vb[:, sl]
                outs = []
                for hh in range(2):
                    qm = qp * head_mask[hh]
                    s = pl.dot(qm, kp, trans_b=True) + bias_b
                    m = jnp.max(s, axis=-1, keepdims=True)
                    p = jnp.exp(s - m)
                    l = jnp.sum(p, axis=-1, keepdims=True)
                    pv = jnp.dot(p.astype(BF16), vp, preferred_element_type=F32)
                    outs.append(pv * (1.0 / l))
                    lse = m + jnp.log(l)
                    stats = jnp.where(lane == 2 * pair + hh, lse, stats)
                o_acc[pair, out_rows, :] = jnp.where(low_head, outs[0], outs[1])
            st_acc[out_rows, :] = stats
            return carry

        lax.fori_loop(0, tmv // blk, body, 0, unroll=min(ATTN_BLOCKS_IN_FLIGHT, tmv // blk))

    @pl.when(rstep == pl.num_programs(1) - 1)
    def _():
        for pair in range(A_HEADS // 2):
            o_ref[:, pair * LANES:(pair + 1) * LANES] = o_acc[pair].astype(o_ref.dtype)
        st_ref[...] = st_acc[...]


def _attn_prompt(qv, kv, vv, dil, *, rows_per_step):
    l = qv.shape[0]
    t = l * dil
    blk = A_NKEY
    tmv = max(rows_per_step // dil, blk)
    rps = min(dil, max(1, ATTN_BLOCKS_IN_FLIGHT * blk // tmv))
    assert t % (tmv * dil) == 0 and dil % rps == 0
    cur = pl.BlockSpec((tmv, rps * A_W), lambda n, r: (n, r))
    prev = pl.BlockSpec((blk, rps * A_W), lambda n, r: (jnp.maximum(n * (tmv // blk) - 1, 0), r))
    rows = tmv * dil
    return pl.pallas_call(
        functools.partial(_attn_prompt_kernel, tmv=tmv, dil=dil, rps=rps),
        out_shape=[jax.ShapeDtypeStruct((t, A_W), BF16), jax.ShapeDtypeStruct((t, LANES), F32)],
        grid=(t // rows, dil // rps),
        in_specs=[cur, cur, cur, prev, prev],
        out_specs=[pl.BlockSpec((rows, A_W), lambda n, r: (n, 0)), pl.BlockSpec((rows, LANES), lambda n, r: (n, 0))],
        scratch_shapes=[pltpu.VMEM((rps, tmv + blk, A_W), BF16), pltpu.VMEM((rps, tmv + blk, A_W), BF16),
                        pltpu.VMEM((A_HEADS // 2, rows, LANES), F32), pltpu.VMEM((rows, LANES), F32)],
        compiler_params=_params(("parallel", "arbitrary"), VMEM_LIMIT_BYTES),
        name=f"attn_prompt_d{dil}",
    )(qv, kv, vv, kv, vv)


def _hgrn_prompt_kernel(q_ref, lf_ref, k_ref, v_ref, gain_ref, o_ref, s_out_ref, st_ref, *, th,
                        co_scheduled=lambda: None):
    i = pl.program_id(0)
    c_rows = HG_CHUNK
    nblk = c_rows // HG_BLOCK
    shape3 = (nblk, HG_BLOCK, HG_K)

    @pl.when(i == 0)
    def _():
        st_ref[...] = jnp.zeros_like(st_ref)

    co_scheduled()

    sub = lax.broadcasted_iota(jnp.int32, shape3, 1)
    a_row = lax.broadcasted_iota(jnp.int32, (c_rows, c_rows), 0) // HG_BLOCK
    a_col = lax.broadcasted_iota(jnp.int32, (c_rows, c_rows), 1) // HG_BLOCK

    def shift(x3, s):
        return pltpu.roll(x3, s, 1)

    def chunk(c, carry):
        r0 = pl.multiple_of(c * c_rows, c_rows)
        for h in range(HG_HEADS):
            sl = slice(h * HG_K, (h + 1) * HG_K)
            q = q_ref[pl.ds(r0, c_rows), sl].astype(F32)
            k = k_ref[pl.ds(r0, c_rows), sl].astype(F32)
            v = v_ref[pl.ds(r0, c_rows), sl].astype(F32)
            q3, k3, v3 = (a.reshape(shape3) for a in (q, k, v))
            cum = lf_ref[pl.ds(r0, c_rows), sl].reshape(shape3)
            s = 1
            while s < HG_BLOCK:
                cum = cum + jnp.where(sub >= s, shift(cum, s), 0.0)
                s *= 2
            anchors = [jnp.zeros((1, 1, HG_K), F32)]
            for j in range(nblk):
                anchors.append(anchors[j] + cum[j:j + 1, HG_BLOCK - 1:HG_BLOCK, :])
            b3 = cum + jnp.concatenate(anchors[:nblk], axis=0)
            b = b3.reshape(c_rows, HG_K)
            b_last = anchors[nblk].reshape(1, HG_K)

            o3 = jnp.sum(q3 * k3, axis=-1, keepdims=True) * v3
            for d in range(1, HG_BLOCK):
                dec = jnp.exp(jnp.where(sub >= d, cum - shift(cum, d), NEG_INF))
                a = jnp.sum(q3 * shift(k3, d) * dec, axis=-1, keepdims=True)
                o3 = o3 + a * shift(v3, d)
            o = o3.reshape(c_rows, HG_K)

            b_end = jnp.concatenate(anchors[1:], axis=0)
            k_hat = (k3 * jnp.exp(b_end - b3)).reshape(c_rows, HG_K).astype(BF16)
            q_ver = [(q3 * jnp.exp(jnp.minimum(b3 - anchors[j + 1], 0.0))).reshape(c_rows, HG_K).astype(BF16)
                     for j in range(nblk - 1)]
            r = pl.dot(jnp.concatenate(q_ver, axis=0), k_hat, trans_b=True)
            att = jnp.zeros((c_rows, c_rows), F32)
            for j in range(nblk - 1):
                att = jnp.where(a_col == j, r[j * c_rows:(j + 1) * c_rows, :], att)
            att = jnp.where(a_row > a_col, att, 0.0)
            vb = v.astype(BF16)
            o = o + jnp.dot(att.astype(BF16), vb, preferred_element_type=F32)

            st = st_ref[h]
            o = o + pl.dot((q * jnp.exp(b)).astype(BF16), st.astype(BF16), trans_b=True)
            k_end = (k * jnp.exp(b_last - b)).astype(BF16)
            st_ref[h] = st * jnp.exp(b_last) + pl.dot(vb, k_end, trans_a=True)

            on = o * lax.rsqrt(jnp.mean(o * o, axis=-1, keepdims=True) + EPS) * gain_ref[:, sl]
            o_ref[pl.ds(r0, c_rows), sl] = on.astype(o_ref.dtype)
        return carry

    n_chunks = th // c_rows
    lax.fori_loop(0, n_chunks, chunk, 0, unroll=n_chunks <= 4)

    @pl.when(i == pl.num_programs(0) - 1)
    def _():
        for h in range(HG_HEADS):
            s_out_ref[h] = st_ref[h].T


def _mem_kv_kernel(m_ref, g_ref, w_ref, o_ref):
    xn = _rmsnorm_rows(m_ref[...], g_ref[...]).astype(BF16)
    o_ref[...] = jnp.dot(xn, w_ref[...], preferred_element_type=F32)


def _mem_kv(mem, gain, w):
    m = mem.shape[0]
    return pl.pallas_call(
        _mem_kv_kernel,
        out_shape=jax.ShapeDtypeStruct((m, 2 * XA_W), F32),
        name="mem_kv",
    )(mem, gain, w)


def _softmax_rows(s):
    m = jnp.max(s, axis=-1, keepdims=True)
    p = jnp.exp(s - m)
    return p, jnp.sum(p, axis=-1, keepdims=True)


def _xattn_rows(cq, mk_ref, mv_ref):
    parts = []
    for h in range(XA_HEADS):
        sl = slice(h * XA_HEAD_DIM, (h + 1) * XA_HEAD_DIM)
        s = pl.dot(cq[:, sl], mk_ref[:, sl], trans_b=True) * (XA_HEAD_DIM ** -0.5)
        p, l = _softmax_rows(s)
        parts.append(jnp.dot(p.astype(BF16), mv_ref[:, sl], preferred_element_type=F32) * (1.0 / l))
    return jnp.concatenate(parts, axis=-1)


def _final_kernel(*refs, tm, merge, xattn):
    refs = list(refs)
    if merge:
        o_refs = [refs.pop(0) for _ in range(A_GROUPS)]
        st_refs = [refs.pop(0) for _ in range(A_GROUPS)]
    else:
        oa_ref = refs.pop(0)
    if xattn:
        cq_ref, mk_ref, mv_ref = (refs.pop(0) for _ in range(3))
    else:
        oc_ref = refs.pop(0)
    (ob_ref, az_ref, bz_ref, cz_ref, ga_ref, gb_ref, gc_ref, x_ref, pg_ref,
     wa_ref, wb_ref, wc_ref, wo_ref, out_ref) = refs

    if merge:
        lse = [r[...] for r in st_refs]
        mx = jnp.maximum(jnp.maximum(lse[0], lse[1]), lse[2])
        ex = [jnp.exp(x - mx) for x in lse]
        inv = 1.0 / (ex[0] + ex[1] + ex[2])
        e_row = lax.broadcasted_iota(jnp.int32, (LANES, A_W), 0)
        e_col = lax.broadcasted_iota(jnp.int32, (LANES, A_W), 1) // A_HEAD_DIM
        spread = jnp.where(e_row == e_col, 1.0, 0.0).astype(BF16)
        oa = jnp.zeros((tm, A_W), F32)
        for g in range(A_GROUPS):
            alpha = (ex[g] * inv).astype(BF16)
            w = jnp.dot(alpha, spread, preferred_element_type=F32)
            oa = oa + w * o_refs[g][...].astype(F32)
    else:
        oa = oa_ref[...].astype(F32)
    ya = (oa * az_ref[...].astype(F32)).astype(BF16)

    oc = _xattn_rows(cq_ref[...], mk_ref, mv_ref) if xattn else oc_ref[...].astype(F32)
    yc = (oc * cz_ref[...].astype(F32)).astype(BF16)

    yb = (ob_ref[...].astype(F32) * bz_ref[...].astype(F32)).astype(BF16)

    merged = (ga_ref[...].astype(F32) * jnp.dot(ya, wa_ref[...], preferred_element_type=F32)
              + gb_ref[...].astype(F32) * jnp.dot(yb, wb_ref[...], preferred_element_type=F32)
              + gc_ref[...].astype(F32) * jnp.dot(yc, wc_ref[...], preferred_element_type=F32))
    z = jnp.dot(merged.astype(BF16), wo_ref[...], preferred_element_type=F32)
    out_ref[...] = x_ref[...] + _rmsnorm_rows(z, pg_ref[...])


def _final(att_in, xa_in, ob, az, bz, cz, ga, gb, gc, x, post_g, w_pa, w_pb, w_pc, w_out, *, tm, merge, xattn):
    n = x.shape[0]

    def rows(width):
        return pl.BlockSpec((tm, width), lambda i: (i, 0))

    if merge:
        att_specs = [rows(A_W)] * A_GROUPS + [rows(LANES)] * A_GROUPS
    else:
        att_specs = [rows(A_W)]
    if xattn:
        xa_specs = [rows(XA_W), _resident((MEM_LEN, XA_W)), _resident((MEM_LEN, XA_W))]
    else:
        xa_specs = [rows(XA_W)]
    in_specs = (att_specs + xa_specs + [rows(HG_W), rows(A_W), rows(HG_W), rows(XA_W)]
                + [rows(D_MODEL)] * 4 + [_resident((1, D_MODEL))]
                + [_resident((A_W, D_MODEL)), _resident((HG_W, D_MODEL)), _resident((XA_W, D_MODEL)),
                   _resident((D_MODEL, D_MODEL))])
    return pl.pallas_call(
        functools.partial(_final_kernel, tm=tm, merge=merge, xattn=xattn),
        out_shape=jax.ShapeDtypeStruct((n, D_MODEL), F32),
        grid=(n // tm,),
        in_specs=in_specs,
        out_specs=rows(D_MODEL),
        compiler_params=_params(("parallel",), VMEM_LIMIT_BYTES),
        name="final_merge" if merge else "final",
    )(*att_in, *xa_in, ob, az, bz, cz, ga, gb, gc, x, post_g, w_pa, w_pb, w_pc, w_out)


def _head_rows(q_row, head_dim):
    w = q_row.shape[-1]
    assert w // head_dim <= SUBLANES
    sub = lax.broadcasted_iota(jnp.int32, (SUBLANES, w), 0)
    lane_head = lax.broadcasted_iota(jnp.int32, (SUBLANES, w), 1) // head_dim
    keep = sub == lane_head
    return jnp.where(keep, jnp.broadcast_to(q_row, (SUBLANES, w)), 0.0), keep


def _window_cache_view(cache, g):
    bd, wb = cache.shape[0], cache.shape[1]
    assert wb == A_NKEY * A_DILATIONS[g], "window buffer must hold exactly 128 dilated keys"
    return jnp.transpose(cache, (0, 2, 3, 4, 1)).reshape(bd, 2, A_W, wb)


def _attn_sample_group(dil, q_row, k_new, v_new, c_ref):
    wb = c_ref.shape[-1]
    qm, _ = _head_rows(q_row, A_HEAD_DIM)
    kt = c_ref[0].astype(BF16)
    vt = c_ref[1].astype(BF16)
    s = jnp.dot(qm.astype(BF16), kt, preferred_element_type=F32)
    w_pos = lax.broadcasted_iota(jnp.int32, (SUBLANES, wb), 1)
    s = jnp.where(w_pos % dil == 0, s, NEG_INF)
    s_new = jnp.sum(qm * k_new, axis=-1, keepdims=True)
    m = jnp.maximum(jnp.max(s, axis=-1, keepdims=True), s_new)
    p = jnp.exp(s - m)
    p_new = jnp.exp(s_new - m)
    l = jnp.sum(p, axis=-1, keepdims=True) + p_new
    o = (pl.dot(p.astype(BF16), vt, trans_b=True) + p_new * v_new) * (1.0 / l)
    return o, m + jnp.log(l)


def _attn_sample_kernel(q0, q1, q2, k0, k1, k2, v0, v1, v2, c0, c1, c2, o_ref, *, sb):
    q_refs, k_refs, v_refs, c_refs = (q0, q1, q2), (k0, k1, k2), (v0, v1, v2), (c0, c1, c2)
    _, keep = _head_rows(jnp.zeros((1, A_W), F32), A_HEAD_DIM)
    for b in range(sb):
        row = pl.ds(pl.program_id(0) * sb + b, 1)
        outs = []
        lses = []
        for g in range(A_GROUPS):
            o, lse = _attn_sample_group(A_DILATIONS[g], q_refs[g][row, :], k_refs[g][row, :], v_refs[g][row, :],
                                        c_refs[g].at[b])
            outs.append(o)
            lses.append(lse)
        mx = jnp.maximum(jnp.maximum(lses[0], lses[1]), lses[2])
        ex = [jnp.exp(x - mx) for x in lses]
        inv = 1.0 / (ex[0] + ex[1] + ex[2])
        oa = (ex[0] * outs[0] + ex[1] * outs[1] + ex[2] * outs[2]) * inv
        o_ref[row, :] = jnp.sum(jnp.where(keep, oa, 0.0), axis=0, keepdims=True)


def _hgrn_and_attn_sample_kernel(*refs, th, sb):
    hg_in, at_in = refs[0:5], refs[5:17]
    hg_out, at_out, hg_state = refs[17:19], refs[19], refs[20]
    _hgrn_prompt_kernel(*hg_in, *hg_out, hg_state, th=th,
                        co_scheduled=lambda: _attn_sample_kernel(*at_in, at_out, sb=sb))


def _hgrn_and_attn_sample(q, lf, k, v, gain, qs, ks, vs, caches, *, seqs_per_step):
    t = q.shape[0]
    bd = qs[0].shape[0]
    sb = seqs_per_step
    steps = bd // sb
    assert bd % sb == 0 and t % steps == 0 and (t // steps) % HG_CHUNK == 0
    th = t // steps
    views = [_window_cache_view(caches[g], g) for g in range(A_GROUPS)]
    c_specs = [pl.BlockSpec((sb,) + v_.shape[1:], lambda b: (b, 0, 0, 0)) for v_ in views]
    rows = pl.BlockSpec((th, HG_W), lambda i: (i, 0))
    full = pl.BlockSpec((bd, A_W), lambda b: (0, 0))
    state = pl.BlockSpec((HG_HEADS, HG_K, HG_V), lambda i: (0, 0, 0))
    return pl.pallas_call(
        functools.partial(_hgrn_and_attn_sample_kernel, th=th, sb=sb),
        out_shape=[jax.ShapeDtypeStruct((t, HG_W), BF16), jax.ShapeDtypeStruct((HG_HEADS, HG_K, HG_V), F32),
                   jax.ShapeDtypeStruct((bd, A_W), F32)],
        grid=(steps,),
        in_specs=[rows] * 4 + [_resident((1, HG_W))] + [full] * 9 + c_specs,
        out_specs=[rows, state, full],
        scratch_shapes=[pltpu.VMEM((HG_HEADS, HG_V, HG_K), F32)],
        compiler_params=_params(("arbitrary",), VMEM_LIMIT_BYTES),
        name="hgrn_prompt_attn_sample",
    )(q, lf, k, v, gain, *qs, *ks, *vs, *views)


def _xattn_sample_body(step, q_ref, c_ref, o_ref, *, sb):
    stride = 2 * XA_HEADS
    for b in range(sb):
        row = pl.ds(step * sb + b, 1)
        q_row = q_ref[row, :]
        outs = []
        for h in range(XA_HEADS):
            sl = slice(h * XA_HEAD_DIM, (h + 1) * XA_HEAD_DIM)
            kh = c_ref[b, pl.ds(h, MEM_LEN, stride=stride), :]
            vh = c_ref[b, pl.ds(XA_HEADS + h, MEM_LEN, stride=stride), :]
            s = jnp.sum(kh * q_row[:, sl], axis=-1, keepdims=True) * (XA_HEAD_DIM ** -0.5)
            m = jnp.max(s, axis=0, keepdims=True)
            p = jnp.exp(s - m)
            l = jnp.sum(p, axis=0, keepdims=True)
            outs.append(jnp.sum(p * vh, axis=0, keepdims=True) * (1.0 / l))
        o_ref[row, :] = jnp.concatenate(outs, axis=-1)


def _xattn_sample_rider(cq, mem_cache, *, steps):
    bd = cq.shape[0]
    assert bd % steps == 0
    sb = bd // steps
    rows = MEM_LEN * 2 * XA_HEADS
    view = mem_cache.reshape(bd, rows, XA_HEAD_DIM)
    full = pl.BlockSpec((bd, XA_W), lambda i: (0, 0))
    return _Rider(
        body=functools.partial(_xattn_sample_body, sb=sb),
        inputs=(cq, view),
        in_specs=(full, pl.BlockSpec((sb, rows, XA_HEAD_DIM), lambda i: (i, 0, 0))),
        out_shapes=(jax.ShapeDtypeStruct((bd, XA_W), F32),),
        out_specs=(full,))


def _hgrn_sample_body(step, qt_ref, lft_ref, kt_ref, v_ref, s0_ref, gain_ref, o_ref, s_ref, *, sb):
    for b in range(sb):
        row = pl.ds(step * sb + b, 1)
        v_all = v_ref[row, :]
        outs = []
        for h in range(HG_HEADS):
            sl = slice(h * HG_K, (h + 1) * HG_K)
            d_col = jnp.exp(lft_ref[0, sl, b:b + 1])
            k_col = kt_ref[0, sl, b:b + 1]
            q_col = qt_ref[0, sl, b:b + 1]
            s_new = d_col * s0_ref[b, h] + k_col * v_all[:, sl]
            s_ref[b, h] = s_new
            o = jnp.sum(q_col * s_new, axis=0, keepdims=True)
            outs.append(o * lax.rsqrt(jnp.mean(o * o, axis=-1, keepdims=True) + EPS) * gain_ref[:, sl])
        o_ref[row, :] = jnp.concatenate(outs, axis=-1)


def _hgrn_sample_rider(q, lf, kin, v, s0, gain, *, steps):
    bd = q.shape[0]
    assert bd % steps == 0
    sb = bd // steps

    def cols(a):
        return a.reshape(steps, sb, HG_W).transpose(0, 2, 1)

    col_spec = pl.BlockSpec((1, HG_W, sb), lambda i: (i, 0, 0))
    st_spec = pl.BlockSpec((sb, HG_HEADS, HG_K, HG_V), lambda i: (i, 0, 0, 0))
    full = pl.BlockSpec((bd, HG_W), lambda i: (0, 0))
    return _Rider(
        body=functools.partial(_hgrn_sample_body, sb=sb),
        inputs=(cols(q), cols(lf), cols(kin), v, s0, gain),
        in_specs=(col_spec, col_spec, col_spec, full, st_spec, _resident((1, HG_W))),
        out_shapes=(jax.ShapeDtypeStruct((bd, HG_W), F32), jax.ShapeDtypeStruct(s0.shape, F32)),
        out_specs=(full, st_spec))


def kernel(x_prompt, x_sample, mem_prompt, cache_win128_kv, cache_win512_kv, cache_win2048_kv, state_hgrn, cache_mem_kv, norm_pre, norm_post, w_in, hgrn_lb_logits, hgrn_out_norm, mem_norm, w_mem_kv, w_branch_a, w_branch_b, w_branch_c, w_out):
    depth = w_in.shape[0]
    assert depth == 1, "single-layer trunk"
    layer = 0
    bp, t, _ = x_prompt.shape
    bd, s_len, _ = x_sample.shape
    assert bp == 1 and s_len == 1
    caches = (cache_win128_kv[layer], cache_win512_kv[layer], cache_win2048_kv[layer])

    w_att = w_rest = w_in[layer].astype(BF16)
    pre_g = norm_pre[layer].reshape(1, D_MODEL)
    post_g = norm_post[layer].reshape(1, D_MODEL)
    hg_g = hgrn_out_norm[layer].reshape(1, HG_W)
    w_pa, w_pb, w_pc, w_o = (w[layer].astype(BF16) for w in (w_branch_a, w_branch_b, w_branch_c, w_out))
    half = A_HEAD_DIM // 2
    inv = ROPE_THETA ** (-jnp.arange(0, A_HEAD_DIM, 2, dtype=F32) / A_HEAD_DIM)
    invf = jnp.tile(inv, LANES // half).reshape(1, LANES)

    xs = x_sample.reshape(bd, D_MODEL)
    q_s, k_s, v_s, tails_s, _ = _inproj_attn(xs, pre_g, w_att, invf, tm=bd, pos_base=PAST_LEN, pos_step=0,
                                             tail_rows=(bd,) * A_GROUPS, out_dtype=F32, row_dil=(1,) * A_GROUPS)
    (az_s, bq_s, lf_s, kin_s, bi_s, bz_s, cq_s, cz_s, ga_s, gb_s, gc_s), _ = _inproj_rest(
        xs, pre_g, w_rest, hgrn_lb_logits, tm=bd, layer=layer, hg_dtype=F32, cq_dtype=F32)

    xp = x_prompt.reshape(t, D_MODEL)
    tile = _tiling(t, bd)
    tm_p = tile.proj_rows
    steps_p = t // tm_p
    tail_rows = tuple(min(w, t) for w in A_WINDOWS)
    q_p, k_p, v_p, tails_p, _ = _inproj_attn(
        xp, pre_g, w_att, invf, tm=tm_p, pos_base=0, pos_step=1, tail_rows=tail_rows, out_dtype=BF16,
        row_dil=A_DILATIONS)
    (az, bq, lf, kin, bi, bz, cq, cz, ga, gb, gc), (ob_s, s_s, oc_s) = _inproj_rest(
        xp, pre_g, w_rest, hgrn_lb_logits, tm=tm_p, layer=layer, hg_dtype=BF16, cq_dtype=BF16,
        rider=_join_riders(_hgrn_sample_rider(bq_s, lf_s, kin_s, bi_s, state_hgrn[layer], hg_g, steps=steps_p),
                           _xattn_sample_rider(cq_s, cache_mem_kv[layer], steps=steps_p)))
    att = [_attn_prompt(q_p[g], k_p[g], v_p[g], A_DILATIONS[g], rows_per_step=tile.attn_rows)
           for g in range(A_GROUPS)]
    mem_kv = _mem_kv(mem_prompt.reshape(MEM_LEN, D_MODEL), mem_norm[layer].reshape(1, D_MODEL),
                     w_mem_kv[layer].astype(BF16))
    mk = mem_kv[:, :XA_W].astype(BF16)
    mv = mem_kv[:, XA_W:].astype(BF16)
    ob_p, s_p, oa_s = _hgrn_and_attn_sample(bq, lf, kin, bi, hg_g, q_s, k_s, v_s, caches,
                                            seqs_per_step=tile.fused_seqs)
    y_p = _final([a[0] for a in att] + [a[1] for a in att], [cq, mk, mv], ob_p, az, bz, cz, ga, gb, gc,
                 xp, post_g, w_pa, w_pb, w_pc, w_o, tm=tile.final_rows, merge=True, xattn=True)

    y_s = _final([oa_s], [oc_s], ob_s, az_s, bz_s, cz_s, ga_s, gb_s, gc_s,
                 xs, post_g, w_pa, w_pb, w_pc, w_o, tm=bd, merge=False, xattn=False)

    new_win_p = [tails_p[g].reshape(1, 1, tail_rows[g], 2, A_HEADS, A_HEAD_DIM) for g in range(A_GROUPS)]
    new_win_s = [tails_s[g].reshape(1, bd, 1, 2, A_HEADS, A_HEAD_DIM) for g in range(A_GROUPS)]
    return (y_p.reshape(bp, t, D_MODEL), y_s.reshape(bd, 1, D_MODEL),
            new_win_p[0], new_win_p[1], new_win_p[2],
            s_p.reshape(1, 1, HG_HEADS, HG_K, HG_V),
            mem_kv.reshape(1, 1, MEM_LEN, 2, XA_HEADS, XA_HEAD_DIM),
            new_win_s[0], new_win_s[1], new_win_s[2],
            s_s.reshape(1, bd, HG_HEADS, HG_K, HG_V))
```
